```python
import math
import jax
import jax.numpy as jnp
from jax import lax
import numpy as np

D_MODEL = 2048
BATCH = 2
SEQ = 4096
DEPTH = 2

N_MIXERS = 2
ATT_HEADS = 16
ATT_HEAD_DIM = D_MODEL // ATT_HEADS
DILATED_PATTERNS = ((128, 1), (512, 4), (2048, 16))
ATT_BLOCK = 128
LSTM_HEADS = 4
LSTM_V_DIM = D_MODEL // LSTM_HEADS
LSTM_QK_DIM = LSTM_V_DIM // 2
LSTM_QK_WIDTH = LSTM_HEADS * LSTM_QK_DIM
LSTM_IN_WIDTH = 2 * LSTM_QK_WIDTH + 2 * D_MODEL + 2 * LSTM_HEADS
LSTM_CHUNK = 64
LSTM_CONV = 4
FFN_DIM = ((8 * D_MODEL // 3 + 255) // 256) * 256
FFN_CONV = 3
NORM_EPS = 1e-6

kernel_name = 'hybrid_dilated_attn_mlstm_convffn'


def rms_norm(x, g):
    xf = x.astype(jnp.float32)
    y = xf * lax.rsqrt(jnp.mean(xf * xf, axis=-1, keepdims=True) + NORM_EPS)
    return y * g.astype(jnp.float32)


def causal_dwconv(x, w, b):
    K = w.shape[0]
    S = x.shape[1]
    xp = jnp.pad(x, ((0, 0), (K - 1, 0), (0, 0)))
    y = b
    for j in range(K):
        y = y + w[j] * xp[:, j:j + S]
    return y


def dilated_branch(q, k, v, window, dil):
    B, H, S, hd = q.shape
    n_sub = -(-S // dil)
    nb = -(-n_sub // ATT_BLOCK)
    lp = nb * ATT_BLOCK
    pad = lp * dil - S

    def to_blocks(t):
        t = jnp.pad(t, ((0, 0), (0, 0), (0, pad), (0, 0)))
        t = t.reshape(B, H, lp, dil, hd).transpose(0, 1, 3, 2, 4)
        return t.reshape(B, H, dil, nb, ATT_BLOCK, hd)

    def with_prev(t):
        prev = jnp.pad(t[:, :, :, :-1], ((0, 0), (0, 0), (0, 0), (1, 0), (0, 0), (0, 0)))
        return jnp.concatenate([prev, t], axis=4)

    qb = to_blocks(q)
    kb = with_prev(to_blocks(k))
    vb = with_prev(to_blocks(v))
    s = jnp.einsum('bhrnid,bhrnjd->bhrnij', qb, kb)
    i = jnp.arange(ATT_BLOCK)[:, None]
    j = jnp.arange(2 * ATT_BLOCK)[None, :]
    dist = ATT_BLOCK + i - j
    band = (dist >= 0) & (dist <= window // dil)
    has_prev = (jnp.arange(nb) > 0)[:, None, None] | (j >= ATT_BLOCK)[None]
    valid = band[None] & has_prev
    s = jnp.where(valid, s, -jnp.inf)
    mx = jnp.max(s, axis=-1, keepdims=True)
    p = jnp.exp(s - mx)
    den = jnp.sum(p, axis=-1)
    o = jnp.einsum('bhrnij,bhrnjd->bhrnid', p, vb) / den[..., None]
    lse = mx[..., 0] + jnp.log(den)
    o = o.reshape(B, H, dil, lp, hd).transpose(0, 1, 3, 2, 4).reshape(B, H, lp * dil, hd)[:, :, :S]
    lse = lse.reshape(B, H, dil, lp).transpose(0, 1, 3, 2).reshape(B, H, lp * dil)[:, :, :S]
    return o, lse


def dilated_attention(x, norm_g, w_qkv, q_gain, k_gain, w_o):
    B, S, _ = x.shape
    h = rms_norm(x, norm_g).astype(x.dtype)
    z = (h @ w_qkv).reshape(B, S, 3, ATT_HEADS, ATT_HEAD_DIM)
    z = z.astype(jnp.float32).transpose(2, 0, 3, 1, 4)
    q = rms_norm(z[0], q_gain) * (ATT_HEAD_DIM ** -0.5)
    k = rms_norm(z[1], k_gain)
    v = z[2]
    outs = []
    lses = []
    for window, dil in DILATED_PATTERNS:
        o, l = dilated_branch(q, k, v, window, dil)
        outs.append(o)
        lses.append(l)
    wts = jax.nn.softmax(jnp.stack(lses), axis=0)
    o = jnp.einsum('gbhs,gbhsd->bshd', wts, jnp.stack(outs)).reshape(B, S, D_MODEL)
    return o.astype(x.dtype) @ w_o


def mlstm_mixer(x, norm_g, w_in, gate_bias, conv_w, conv_b, head_gain, w_out):
    B, S, _ = x.shape
    H, dk, dv, L = LSTM_HEADS, LSTM_QK_DIM, LSTM_V_DIM, LSTM_CHUNK
    QKW = LSTM_QK_WIDTH
    h = rms_norm(x, norm_g).astype(x.dtype)
    z = h @ w_in
    qk = jax.nn.silu(causal_dwconv(z[..., :2 * QKW], conv_w, conv_b)).astype(jnp.float32)
    q = qk[..., :QKW].reshape(B, S, H, dk).transpose(0, 2, 1, 3)
    k = qk[..., QKW:].reshape(B, S, H, dk).transpose(0, 2, 1, 3) * (dk ** -0.5)
    v = z[..., 2 * QKW:2 * QKW + D_MODEL].astype(jnp.float32).reshape(B, S, H, dv).transpose(0, 2, 1, 3)
    o_gate = jax.nn.sigmoid(z[..., 2 * QKW + D_MODEL:2 * QKW + 2 * D_MODEL].astype(jnp.float32))
    gates = (z[..., 2 * QKW + 2 * D_MODEL:].astype(jnp.float32) + gate_bias.astype(jnp.float32)).transpose(0, 2, 1)
    log_i = gates[:, :H]
    log_f = jax.nn.log_sigmoid(gates[:, H:])
    nc = S // L

    def chunks(t):
        return jnp.moveaxis(t.reshape((B, H, nc, L) + t.shape[3:]), 2, 0)

    causal = jnp.tril(jnp.ones((L, L), dtype=bool))

    def step(carry, inp):
        C, n, m = carry
        qc, kc, vc, lic, lfc = inp
        b = jnp.cumsum(lfc, axis=-1)
        D = jnp.where(causal, b[..., :, None] - b[..., None, :] + lic[..., None, :], -jnp.inf)
        g = b + m[..., None]
        m_t = jnp.maximum(g, jnp.max(D, axis=-1))
        P = jnp.exp(D - m_t[..., None])
        inter = jnp.exp(g - m_t)
        W = P * jnp.einsum('bhld,bhsd->bhls', qc, kc)
        num = inter[..., None] * jnp.einsum('bhld,bhde->bhle', qc, C) + jnp.einsum('bhls,bhse->bhle', W, vc)
        den = inter * jnp.einsum('bhld,bhd->bhl', qc, n) + jnp.sum(W, axis=-1)
        h_c = num / jnp.maximum(jnp.abs(den), jnp.exp(-m_t))[..., None]
        bL = b[..., -1]
        a = bL[..., None] - b + lic
        m_new = jnp.maximum(bL + m, jnp.max(a, axis=-1))
        decay = jnp.exp(bL + m - m_new)
        wts = jnp.exp(a - m_new[..., None])
        C_new = decay[..., None, None] * C + jnp.einsum('bhs,bhsd,bhse->bhde', wts, kc, vc)
        n_new = decay[..., None] * n + jnp.einsum('bhs,bhsd->bhd', wts, kc)
        return (C_new, n_new, m_new), h_c

    init = (jnp.zeros((B, H, dk, dv), jnp.float32), jnp.zeros((B, H, dk), jnp.float32), jnp.zeros((B, H), jnp.float32))
    _, hs = lax.scan(step, init, (chunks(q), chunks(k), chunks(v), chunks(log_i), chunks(log_f)))
    hs = jnp.moveaxis(hs, 0, 2).reshape(B, H, S, dv).transpose(0, 2, 1, 3)
    hs = rms_norm(hs, head_gain.reshape(H, dv)).reshape(B, S, D_MODEL) * o_gate
    return hs.astype(x.dtype) @ w_out


def conv_ffn(x, norm_g, w_up, conv_w, conv_b, w_down):
    h = rms_norm(x, norm_g).astype(x.dtype)
    u = causal_dwconv(h @ w_up, conv_w, conv_b)
    gate = u[..., :FFN_DIM]
    up = u[..., FFN_DIM:]
    return (jax.nn.silu(gate) * up) @ w_down


def setup_inputs(seed: int = 0) -> dict:
    key = jax.random.key(seed)
    ks = jax.random.split(key, 24)
    n_a = (DEPTH + 1) // 2
    n_b = DEPTH // 2
    res = (2 * DEPTH) ** -0.5
    f32 = jnp.float32

    def nrm(k, shape, scale):
        return scale * jax.random.normal(k, shape, f32)

    def gain(k, shape):
        return 1.0 + nrm(k, shape, 0.05)

    def conv_init(k, n, width, ch):
        ident = jnp.zeros((width, ch), f32).at[width - 1].set(1.0)
        return ident[None] + nrm(k, (n, width, ch), 0.3)

    x = jax.random.normal(ks[0], (BATCH, SEQ, D_MODEL), f32)
    lstm_gate_bias = jnp.concatenate([
        nrm(ks[8], (n_b, LSTM_HEADS), 0.1),
        jnp.linspace(3.0, 6.0, LSTM_HEADS, dtype=f32)[None] + nrm(ks[9], (n_b, LSTM_HEADS), 0.1)], axis=1)
    return {
        'x': x,
        'attn_norm': gain(ks[1], (n_a, D_MODEL)),
        'attn_w_qkv': nrm(ks[2], (n_a, D_MODEL, 3 * D_MODEL), D_MODEL ** -0.5),
        'attn_q_gain': gain(ks[3], (n_a, ATT_HEAD_DIM)),
        'attn_k_gain': gain(ks[4], (n_a, ATT_HEAD_DIM)),
        'attn_w_o': nrm(ks[5], (n_a, D_MODEL, D_MODEL), res * D_MODEL ** -0.5),
        'lstm_norm': gain(ks[6], (n_b, D_MODEL)),
        'lstm_w_in': nrm(ks[7], (n_b, D_MODEL, LSTM_IN_WIDTH), D_MODEL ** -0.5),
        'lstm_gate_bias': lstm_gate_bias,
        'lstm_conv_w': conv_init(ks[10], n_b, LSTM_CONV, 2 * LSTM_QK_WIDTH),
        'lstm_conv_b': nrm(ks[11], (n_b, 2 * LSTM_QK_WIDTH), 0.01),
        'lstm_head_gain': gain(ks[12], (n_b, D_MODEL)),
        'lstm_w_out': nrm(ks[13], (n_b, D_MODEL, D_MODEL), res * D_MODEL ** -0.5),
        'ffn_norm': gain(ks[14], (DEPTH, D_MODEL)),
        'ffn_w_up': nrm(ks[15], (DEPTH, D_MODEL, 2 * FFN_DIM), D_MODEL ** -0.5),
        'ffn_conv_w': conv_init(ks[16], DEPTH, FFN_CONV, 2 * FFN_DIM),
        'ffn_conv_b': nrm(ks[17], (DEPTH, 2 * FFN_DIM), 0.01),
        'ffn_w_down': nrm(ks[18], (DEPTH, FFN_DIM, D_MODEL), res * FFN_DIM ** -0.5),
    }


def reference(x, attn_norm, attn_w_qkv, attn_q_gain, attn_k_gain, attn_w_o,
              lstm_norm, lstm_w_in, lstm_gate_bias, lstm_conv_w, lstm_conv_b, lstm_head_gain, lstm_w_out,
              ffn_norm, ffn_w_up, ffn_conv_w, ffn_conv_b, ffn_w_down):
    for i in range(DEPTH):
        j = i // N_MIXERS
        if i % N_MIXERS == 0:
            x = x + dilated_attention(x, attn_norm[j], attn_w_qkv[j], attn_q_gain[j], attn_k_gain[j], attn_w_o[j])
        else:
            x = x + mlstm_mixer(x, lstm_norm[j], lstm_w_in[j], lstm_gate_bias[j], lstm_conv_w[j],
                                lstm_conv_b[j], lstm_head_gain[j], lstm_w_out[j])
        x = x + conv_ffn(x, ffn_norm[i], ffn_w_up[i], ffn_conv_w[i], ffn_conv_b[i], ffn_w_down[i])
    return x
```

```python
import functools

import jax
import jax.numpy as jnp
from jax import lax
from jax.experimental import pallas as pl
from jax.experimental.pallas import tpu as pltpu

F32 = jnp.float32
BF16 = jnp.bfloat16

D_MODEL = 2048
ATT_HEADS = 16
ATT_HEAD_DIM = D_MODEL // ATT_HEADS
DILATED_PATTERNS = ((128, 1), (512, 4), (2048, 16))
ATT_BLOCK = 128
LSTM_HEADS = 4
LSTM_V_DIM = D_MODEL // LSTM_HEADS
LSTM_QK_DIM = LSTM_V_DIM // 2
LSTM_QK_WIDTH = LSTM_HEADS * LSTM_QK_DIM
LSTM_MAIN_WIDTH = 2 * LSTM_QK_WIDTH + 2 * D_MODEL
LSTM_CONV = 4
FFN_DIM = ((8 * D_MODEL // 3 + 255) // 256) * 256
FFN_CONV = 3
NORM_EPS = 1e-6

LANES = 128
SUBLANES = 8
HALO = SUBLANES
VMEM_LIMIT = 56 * 1024 * 1024

LSTM_CHUNK = 256


def _rms(x, g):
    ms = jnp.mean(x * x, axis=-1, keepdims=True)
    return x * lax.rsqrt(ms + NORM_EPS) * g


def _dot(a, b):
    return jnp.dot(a, b, preferred_element_type=F32)


def _dot_nt(a, b):
    return lax.dot_general(a, b, (((1,), (1,)), ((), ())), preferred_element_type=F32)


def _dot_tn(a, b):
    return lax.dot_general(a, b, (((0,), (0,)), ((), ())), preferred_element_type=F32)


def _params(sem):
    return pltpu.CompilerParams(dimension_semantics=sem, vmem_limit_bytes=VMEM_LIMIT)


def _qkv_kernel(x_ref, g_ref, w_ref, qg_ref, kg_ref, o_ref, h_ref, *, n_head_blocks, heads_per_block):
    j = pl.program_id(1)

    @pl.when(j == 0)
    def _():
        h_ref[...] = _rms(x_ref[...], g_ref[...]).astype(BF16)

    z = _dot(h_ref[...], w_ref[...])

    @pl.when(j < 2 * n_head_blocks)
    def _():
        gain = jnp.where(j < n_head_blocks, qg_ref[...] * (ATT_HEAD_DIM ** -0.5), kg_ref[...])
        for hh in range(heads_per_block):
            sl = slice(hh * ATT_HEAD_DIM, (hh + 1) * ATT_HEAD_DIM)
            o_ref[:, sl] = _rms(z[:, sl], gain)

    @pl.when(j >= 2 * n_head_blocks)
    def _():
        o_ref[...] = z


def _qkv_call(x, g, w, qg, kg, *, tm=512, tn=512):
    m = x.shape[0]
    n = w.shape[1]
    heads_per_block = tn // ATT_HEAD_DIM
    n_head_blocks = D_MODEL // tn
    return pl.pallas_call(
        functools.partial(_qkv_kernel, n_head_blocks=n_head_blocks, heads_per_block=heads_per_block),
        grid=(m // tm, n // tn),
        in_specs=[
            pl.BlockSpec((tm, D_MODEL), lambda i, j: (i, 0)),
            pl.BlockSpec((1, D_MODEL), lambda i, j: (0, 0)),
            pl.BlockSpec((D_MODEL, tn), lambda i, j: (0, j)),
            pl.BlockSpec((1, ATT_HEAD_DIM), lambda i, j: (0, 0)),
            pl.BlockSpec((1, ATT_HEAD_DIM), lambda i, j: (0, 0)),
        ],
        out_specs=pl.BlockSpec((tm, tn), lambda i, j: (i, j)),
        out_shape=jax.ShapeDtypeStruct((m, n), F32),
        scratch_shapes=[pltpu.VMEM((tm, D_MODEL), BF16)],
        compiler_params=_params(("parallel", "arbitrary")),
        name="attn_qkv",
    )(x, g, w, qg, kg)


def _attn_kernel(q_ref, k_ref, v_ref, o_ref, og_ref, lse_ref, *, seq):
    blk = ATT_BLOCK
    row = lax.broadcasted_iota(jnp.int32, (blk, blk), 0)
    col = lax.broadcasted_iota(jnp.int32, (blk, blk), 1)
    cur_ok = col <= row
    prev_ok = col >= row

    for gi, (window, dil) in enumerate(DILATED_PATTERNS):
        assert window // dil == blk
        n_blocks = seq // (dil * blk)

        def rows(r, n, dil=dil):
            start = r + n * (blk * dil)
            if dil == 1:
                return pl.ds(pl.multiple_of(start, blk), blk)
            return pl.ds(start, blk, stride=dil)

        def block(idx, carry, gi=gi, n_blocks=n_blocks, rows=rows):
            r = idx // n_blocks
            n = idx % n_blocks
            cur = rows(r, n)
            prev = rows(r, jnp.maximum(n - 1, 0))
            has_prev = n > 0
            q = q_ref[cur, :].astype(BF16)
            s_c = _dot_nt(q, k_ref[cur, :].astype(BF16))
            s_p = _dot_nt(q, k_ref[prev, :].astype(BF16))
            s_c = jnp.where(cur_ok, s_c, -jnp.inf)
            s_p = jnp.where(jnp.logical_and(prev_ok, has_prev), s_p, -jnp.inf)
            mx = jnp.maximum(jnp.max(s_c, axis=-1, keepdims=True), jnp.max(s_p, axis=-1, keepdims=True))
            p_c = jnp.exp(s_c - mx)
            p_p = jnp.exp(s_p - mx)
            den = jnp.sum(p_c, axis=-1, keepdims=True) + jnp.sum(p_p, axis=-1, keepdims=True)
            pv = _dot(p_c.astype(BF16), v_ref[cur, :].astype(BF16)) + _dot(p_p.astype(BF16), v_ref[prev, :].astype(BF16))
            og_ref[gi, cur, :] = pv / den
            lse_ref[gi, cur, :] = mx + jnp.log(den)
            return carry

        lax.fori_loop(0, dil * n_blocks, block, 0)

    def combine(c, carry):
        sl = pl.ds(pl.multiple_of(c * blk, blk), blk)
        l0 = lse_ref[0, sl, :]
        l1 = lse_ref[1, sl, :]
        l2 = lse_ref[2, sl, :]
        mx = jnp.maximum(jnp.maximum(l0, l1), l2)
        e0 = jnp.exp(l0 - mx)
        e1 = jnp.exp(l1 - mx)
        e2 = jnp.exp(l2 - mx)
        den = e0 + e1 + e2
        o = (e0 / den) * og_ref[0, sl, :] + (e1 / den) * og_ref[1, sl, :] + (e2 / den) * og_ref[2, sl, :]
        o_ref[sl, :] = o.astype(o_ref.dtype)
        return carry

    lax.fori_loop(0, seq // blk, combine, 0)


def _attn_call(qkv, batch, seq):
    qkv = qkv.reshape(batch, seq, 3 * D_MODEL)
    hd = ATT_HEAD_DIM
    n_groups = len(DILATED_PATTERNS)
    return pl.pallas_call(
        functools.partial(_attn_kernel, seq=seq),
        grid=(batch, ATT_HEADS),
        in_specs=[
            pl.BlockSpec((None, seq, hd), lambda b, h: (b, 0, h)),
            pl.BlockSpec((None, seq, hd), lambda b, h: (b, 0, ATT_HEADS + h)),
            pl.BlockSpec((None, seq, hd), lambda b, h: (b, 0, 2 * ATT_HEADS + h)),
        ],
        out_specs=pl.BlockSpec((None, seq, hd), lambda b, h: (b, 0, h)),
        out_shape=jax.ShapeDtypeStruct((batch, seq, D_MODEL), BF16),
        scratch_shapes=[
            pltpu.VMEM((n_groups, seq, hd), F32),
            pltpu.VMEM((n_groups, seq, 1), F32),
        ],
        compiler_params=_params(("parallel", "parallel")),
        name="dilated_attn",
    )(qkv, qkv, qkv)


def _proj_kernel(a_ref, w_ref, x_ref, o_ref):
    o_ref[...] = x_ref[...] + _dot(a_ref[...], w_ref[...])


def _proj_call(a, w, x, *, tm=512, tn=1024):
    m, k = a.shape
    n = w.shape[1]
    return pl.pallas_call(
        _proj_kernel,
        grid=(m // tm, n // tn),
        in_specs=[
            pl.BlockSpec((tm, k), lambda i, j: (i, 0)),
            pl.BlockSpec((k, tn), lambda i, j: (0, j)),
            pl.BlockSpec((tm, tn), lambda i, j: (i, j)),
        ],
        out_specs=pl.BlockSpec((tm, tn), lambda i, j: (i, j)),
        out_shape=jax.ShapeDtypeStruct((m, n), F32),
        compiler_params=_params(("parallel", "parallel")),
        name="proj_residual",
    )(a, w, x)


def _fill_normed(h_ref, x_ref, xh_ref, g_ref, seq_start):
    g = g_ref[...]
    halo = jnp.where(seq_start, 0.0, _rms(xh_ref[...], g))
    h_ref[0:HALO, :] = halo.astype(BF16)
    h_ref[HALO:, :] = _rms(x_ref[...], g).astype(BF16)


def _causal_conv(u, w_ref, b_ref, taps, tm):
    w = w_ref[...]
    y = b_ref[...]
    for j in range(taps):
        off = HALO - (taps - 1) + j
        y = y + w[j:j + 1, :] * u[off:off + tm, :]
    return y


def _ffn_kernel(x_ref, xh_ref, g_ref, wg_ref, wu_ref, cwg_ref, cwu_ref, cbg_ref, cbu_ref, wd_ref,
                o_ref, h_ref, acc_ref, *, tm, tiles_per_seq):
    i = pl.program_id(0)
    j = pl.program_id(1)

    @pl.when(j == 0)
    def _():
        _fill_normed(h_ref, x_ref, xh_ref, g_ref, i % tiles_per_seq == 0)
        acc_ref[...] = jnp.zeros_like(acc_ref)

    h = h_ref[...]
    gate = _causal_conv(_dot(h, wg_ref[...]), cwg_ref, cbg_ref, FFN_CONV, tm)
    up = _causal_conv(_dot(h, wu_ref[...]), cwu_ref, cbu_ref, FFN_CONV, tm)
    act = (gate * jax.nn.sigmoid(gate) * up).astype(BF16)
    acc_ref[...] += _dot(act, wd_ref[...])

    @pl.when(j == pl.num_programs(1) - 1)
    def _():
        o_ref[...] = x_ref[...] + acc_ref[...]


def _ffn_call(x, g, w_up, conv_w, conv_b, w_down, seq, *, tm=512, tf=512):
    m = x.shape[0]
    nf = FFN_DIM // tf
    halo_blocks_per_tile = tm // HALO
    return pl.pallas_call(
        functools.partial(_ffn_kernel, tm=tm, tiles_per_seq=seq // tm),
        grid=(m // tm, nf),
        in_specs=[
            pl.BlockSpec((tm, D_MODEL), lambda i, j: (i, 0)),
            pl.BlockSpec((HALO, D_MODEL), lambda i, j: (jnp.maximum(i * halo_blocks_per_tile - 1, 0), 0)),
            pl.BlockSpec((1, D_MODEL), lambda i, j: (0, 0)),
            pl.BlockSpec((D_MODEL, tf), lambda i, j: (0, j)),
            pl.BlockSpec((D_MODEL, tf), lambda i, j: (0, nf + j)),
            pl.BlockSpec((FFN_CONV, tf), lambda i, j: (0, j)),
            pl.BlockSpec((FFN_CONV, tf), lambda i, j: (0, nf + j)),
            pl.BlockSpec((1, tf), lambda i, j: (0, j)),
            pl.BlockSpec((1, tf), lambda i, j: (0, nf + j)),
            pl.BlockSpec((tf, D_MODEL), lambda i, j: (j, 0)),
        ],
        out_specs=pl.BlockSpec((tm, D_MODEL), lambda i, j: (i, 0)),
        out_shape=jax.ShapeDtypeStruct((m, D_MODEL), F32),
        scratch_shapes=[
            pltpu.VMEM((tm + HALO, D_MODEL), BF16),
            pltpu.VMEM((tm, D_MODEL), F32),
        ],
        compiler_params=_params(("parallel", "arbitrary")),
        name="conv_ffn",
    )(x, x, g, w_up, w_up, conv_w, conv_w, conv_b, conv_b, w_down)


def _lstm_in_kernel(x_ref, xh_ref, g_ref, w_ref, wgate_ref, bgate_ref, cw_ref, cb_ref,
                    z_ref, gates_ref, h_ref, *, tm, tiles_per_seq, n_conv_blocks):
    i = pl.program_id(0)
    j = pl.program_id(1)

    @pl.when(j == 0)
    def _():
        _fill_normed(h_ref, x_ref, xh_ref, g_ref, i % tiles_per_seq == 0)
        gates_ref[...] = _dot(h_ref[HALO:, :], wgate_ref[...]) + bgate_ref[...]

    z = _dot(h_ref[...], w_ref[...])

    @pl.when(j < n_conv_blocks)
    def _():
        y = _causal_conv(z, cw_ref, cb_ref, LSTM_CONV, tm)
        scale = jnp.where(j < n_conv_blocks // 2, 1.0, LSTM_QK_DIM ** -0.5)
        z_ref[...] = y * jax.nn.sigmoid(y) * scale

    @pl.when(j >= n_conv_blocks)
    def _():
        z_ref[...] = z[HALO:, :]


def _lstm_in_call(x, g, w_main, w_gate, b_gate, conv_w, conv_b, seq, *, tm=512, tn=512):
    m = x.shape[0]
    n = w_main.shape[1]
    n_conv_blocks = 2 * LSTM_QK_WIDTH // tn
    halo_blocks_per_tile = tm // HALO
    return pl.pallas_call(
        functools.partial(_lstm_in_kernel, tm=tm, tiles_per_seq=seq // tm, n_conv_blocks=n_conv_blocks),
        grid=(m // tm, n // tn),
        in_specs=[
            pl.BlockSpec((tm, D_MODEL), lambda i, j: (i, 0)),
            pl.BlockSpec((HALO, D_MODEL), lambda i, j: (jnp.maximum(i * halo_blocks_per_tile - 1, 0), 0)),
            pl.BlockSpec((1, D_MODEL), lambda i, j: (0, 0)),
            pl.BlockSpec((D_MODEL, tn), lambda i, j: (0, j)),
            pl.BlockSpec((D_MODEL, LANES), lambda i, j: (0, 0)),
            pl.BlockSpec((1, LANES), lambda i, j: (0, 0)),
            pl.BlockSpec((LSTM_CONV, tn), lambda i, j: (0, jnp.minimum(j, n_conv_blocks - 1))),
            pl.BlockSpec((1, tn), lambda i, j: (0, jnp.minimum(j, n_conv_blocks - 1))),
        ],
        out_specs=[
            pl.BlockSpec((tm, tn), lambda i, j: (i, j)),
            pl.BlockSpec((tm, LANES), lambda i, j: (i, 0)),
        ],
        out_shape=[
            jax.ShapeDtypeStruct((m, n), F32),
            jax.ShapeDtypeStruct((m, LANES), F32),
        ],
        scratch_shapes=[pltpu.VMEM((tm + HALO, D_MODEL), BF16)],
        compiler_params=_params(("parallel", "arbitrary")),
        name="lstm_in",
    )(x, x, g, w_main, w_gate, b_gate, conv_w, conv_b)


def _log_sigmoid(x):
    return -(jnp.maximum(-x, 0.0) + jnp.log1p(jnp.exp(-jnp.abs(x))))


def _cumsum_rows(x):
    n = x.shape[0]
    row = lax.broadcasted_iota(jnp.int32, x.shape, 0)
    shift = 1
    while shift < n:
        x = x + jnp.where(row >= shift, pltpu.roll(x, shift, axis=0), 0.0)
        shift *= 2
    return x


def _mlstm_kernel(q_ref, k_ref, v_ref, og_ref, gates_ref, hg_ref, o_ref, c_ref, n_ref, m_ref):
    chunk = q_ref.shape[0]
    dk, dv, heads = LSTM_QK_DIM, LSTM_V_DIM, LSTM_HEADS

    @pl.when(pl.program_id(1) == 0)
    def _():
        c_ref[...] = jnp.zeros_like(c_ref)
        n_ref[...] = jnp.zeros_like(n_ref)
        m_ref[...] = jnp.zeros_like(m_ref)

    gates = gates_ref[...]
    cum_f = _cumsum_rows(_log_sigmoid(gates))
    gates_t = gates.T
    cum_f_t = cum_f.T
    row = lax.broadcasted_iota(jnp.int32, (chunk, chunk), 0)
    col = lax.broadcasted_iota(jnp.int32, (chunk, chunk), 1)
    causal = col <= row

    for hd in range(heads):
        qs = slice(hd * dk, (hd + 1) * dk)
        vs = slice(hd * dv, (hd + 1) * dv)
        b_col = cum_f[:, heads + hd:heads + hd + 1]
        b_row = cum_f_t[heads + hd:heads + hd + 1, :]
        li_col = gates[:, hd:hd + 1]
        li_row = gates_t[hd:hd + 1, :]
        m_prev = m_ref[hd]
        c_prev = c_ref[hd]
        n_prev = n_ref[hd]
        q = q_ref[:, qs]
        k = k_ref[:, qs]
        v = v_ref[:, vs]
        qb = q.astype(BF16)
        kb = k.astype(BF16)

        dmat = jnp.where(causal, b_col - b_row + li_row, -jnp.inf)
        g = b_col + m_prev
        m_t = jnp.maximum(g, jnp.max(dmat, axis=-1, keepdims=True))
        p = jnp.exp(dmat - m_t)
        inter = jnp.exp(g - m_t)
        w = p * _dot_nt(qb, kb)
        num = inter * _dot(qb, c_prev.astype(BF16)) + _dot(w.astype(BF16), v.astype(BF16))
        den = inter * jnp.sum(q * n_prev, axis=-1, keepdims=True) + jnp.sum(w, axis=-1, keepdims=True)
        h = num / jnp.maximum(jnp.abs(den), jnp.exp(-m_t))

        b_last = b_col[chunk - 1:chunk, :]
        a = b_last - b_col + li_col
        m_new = jnp.maximum(b_last + m_prev, jnp.max(a, axis=0, keepdims=True))
        decay = jnp.exp(b_last + m_prev - m_new)
        wts = jnp.exp(a - m_new)
        c_ref[hd] = decay * c_prev + _dot_tn(kb, (wts * v).astype(BF16))
        n_ref[hd] = decay * n_prev + jnp.sum(wts * k, axis=0, keepdims=True)
        m_ref[hd] = m_new

        hs = _rms(h, hg_ref[:, vs]) * jax.nn.sigmoid(og_ref[:, vs])
        o_ref[:, vs] = hs.astype(o_ref.dtype)


def _mlstm_call(z, gates, head_gain, batch, seq):
    chunk = LSTM_CHUNK
    z = z.reshape(batch, seq, LSTM_MAIN_WIDTH)
    gates = gates.reshape(batch, seq, LANES)
    qkw = LSTM_QK_WIDTH
    return pl.pallas_call(
        _mlstm_kernel,
        grid=(batch, seq // chunk),
        in_specs=[
            pl.BlockSpec((None, chunk, qkw), lambda b, c: (b, c, 0)),
            pl.BlockSpec((None, chunk, qkw), lambda b, c: (b, c, 1)),
            pl.BlockSpec((None, chunk, D_MODEL), lambda b, c: (b, c, 2 * qkw // D_MODEL)),
            pl.BlockSpec((None, chunk, D_MODEL), lambda b, c: (b, c, 2 * qkw // D_MODEL + 1)),
            pl.BlockSpec((None, chunk, LANES), lambda b, c: (b, c, 0)),
            pl.BlockSpec((1, D_MODEL), lambda b, c: (0, 0)),
        ],
        out_specs=pl.BlockSpec((None, chunk, D_MODEL), lambda b, c: (b, c, 0)),
        out_shape=jax.ShapeDtypeStruct((batch, seq, D_MODEL), BF16),
        scratch_shapes=[
            pltpu.VMEM((LSTM_HEADS, LSTM_QK_DIM, LSTM_V_DIM), F32),
            pltpu.VMEM((LSTM_HEADS, 1, LSTM_QK_DIM), F32),
            pltpu.VMEM((LSTM_HEADS, 1, 1), F32),
        ],
        compiler_params=_params(("parallel", "arbitrary")),
        name="mlstm",
    )(z, z, z, z, gates, head_gain)


def kernel(x, attn_norm, attn_w_qkv, attn_q_gain, attn_k_gain, attn_w_o, lstm_norm, lstm_w_in, lstm_gate_bias,
           lstm_conv_w, lstm_conv_b, lstm_head_gain, lstm_w_out, ffn_norm, ffn_w_up, ffn_conv_w, ffn_conv_b,
           ffn_w_down):
    batch, seq, d = x.shape
    assert d == D_MODEL and ffn_norm.shape[0] == 2 and attn_norm.shape[0] == 1 and lstm_norm.shape[0] == 1
    m = batch * seq
    xf = x.reshape(m, d)

    def ffn(xin, layer):
        return _ffn_call(xin, ffn_norm[layer][None], ffn_w_up[layer].astype(BF16), ffn_conv_w[layer],
                         ffn_conv_b[layer][None], ffn_w_down[layer].astype(BF16), seq)

    qkv = _qkv_call(xf, attn_norm[0][None], attn_w_qkv[0].astype(BF16), attn_q_gain[0][None], attn_k_gain[0][None])
    att = _attn_call(qkv, batch, seq)
    xf = _proj_call(att.reshape(m, d), attn_w_o[0].astype(BF16), xf)
    xf = ffn(xf, 0)

    n_gates = 2 * LSTM_HEADS
    w_in = lstm_w_in[0]
    w_gate = jnp.pad(w_in[:, LSTM_MAIN_WIDTH:], ((0, 0), (0, LANES - n_gates))).astype(BF16)
    b_gate = jnp.pad(lstm_gate_bias[0], (0, LANES - n_gates))[None]
    z, gates = _lstm_in_call(xf, lstm_norm[0][None], w_in[:, :LSTM_MAIN_WIDTH].astype(BF16), w_gate, b_gate,
                             lstm_conv_w[0], lstm_conv_b[0][None], seq)
    hs = _mlstm_call(z, gates, lstm_head_gain[0][None], batch, seq)
    xf = _proj_call(hs.reshape(m, d), lstm_w_out[0].astype(BF16), xf)
    xf = ffn(xf, 1)
    return xf.reshape(batch, seq, d)
```

```python
import functools

import jax
import jax.numpy as jnp
from jax import lax
from jax.experimental import pallas as pl
from jax.experimental.pallas import tpu as pltpu

F32 = jnp.float32
BF16 = jnp.bfloat16

D_MODEL = 2048
ATT_HEADS = 16
ATT_HEAD_DIM = D_MODEL // ATT_HEADS
DILATED_PATTERNS = ((128, 1), (512, 4), (2048, 16))
ATT_BLOCK = 128
ATT_GROUP = 4
LSTM_HEADS = 4
LSTM_V_DIM = D_MODEL // LSTM_HEADS
LSTM_QK_DIM = LSTM_V_DIM // 2
LSTM_QK_WIDTH = LSTM_HEADS * LSTM_QK_DIM
LSTM_MAIN_WIDTH = 2 * LSTM_QK_WIDTH + 2 * D_MODEL
LSTM_CONV = 4
FFN_DIM = ((8 * D_MODEL // 3 + 255) // 256) * 256
FFN_CONV = 3
NORM_EPS = 1e-6

LANES = 128
SUBLANES = 8
HALO = SUBLANES
VMEM_LIMIT = 56 * 1024 * 1024

LSTM_CHUNK = 256


def _rms(x, g):
    ms = jnp.mean(x * x, axis=-1, keepdims=True)
    return x * lax.rsqrt(ms + NORM_EPS) * g


def _dot(a, b):
    return jnp.dot(a, b, preferred_element_type=F32)


def _dot_nt(a, b):
    return lax.dot_general(a, b, (((1,), (1,)), ((), ())), preferred_element_type=F32)


def _dot_tn(a, b):
    return lax.dot_general(a, b, (((0,), (0,)), ((), ())), preferred_element_type=F32)


def _params(sem):
    return pltpu.CompilerParams(dimension_semantics=sem, vmem_limit_bytes=VMEM_LIMIT)


def _qkv_kernel(x_ref, g_ref, w_ref, qg_ref, kg_ref, o_ref, h_ref, *, n_head_blocks, heads_per_block):
    j = pl.program_id(1)

    @pl.when(j == 0)
    def _():
        h_ref[...] = _rms(x_ref[...], g_ref[...]).astype(BF16)

    z = _dot(h_ref[...], w_ref[...])

    @pl.when(j < 2 * n_head_blocks)
    def _():
        gain = jnp.where(j < n_head_blocks, qg_ref[...] * (ATT_HEAD_DIM ** -0.5), kg_ref[...])
        for hh in range(heads_per_block):
            sl = slice(hh * ATT_HEAD_DIM, (hh + 1) * ATT_HEAD_DIM)
            o_ref[:, sl] = _rms(z[:, sl], gain)

    @pl.when(j >= 2 * n_head_blocks)
    def _():
        o_ref[...] = z


def _qkv_call(x, g, w, qg, kg, *, tm=512, tn=512):
    m = x.shape[0]
    n = w.shape[1]
    heads_per_block = tn // ATT_HEAD_DIM
    n_head_blocks = D_MODEL // tn
    return pl.pallas_call(
        functools.partial(_qkv_kernel, n_head_blocks=n_head_blocks, heads_per_block=heads_per_block),
        grid=(m // tm, n // tn),
        in_specs=[
            pl.BlockSpec((tm, D_MODEL), lambda i, j: (i, 0)),
            pl.BlockSpec((1, D_MODEL), lambda i, j: (0, 0)),
            pl.BlockSpec((D_MODEL, tn), lambda i, j: (0, j)),
            pl.BlockSpec((1, ATT_HEAD_DIM), lambda i, j: (0, 0)),
            pl.BlockSpec((1, ATT_HEAD_DIM), lambda i, j: (0, 0)),
        ],
        out_specs=pl.BlockSpec((tm, tn), lambda i, j: (i, j)),
        out_shape=jax.ShapeDtypeStruct((m, n), F32),
        scratch_shapes=[pltpu.VMEM((tm, D_MODEL), BF16)],
        compiler_params=_params(("parallel", "arbitrary")),
        name="attn_qkv",
    )(x, g, w, qg, kg)


def _attn_kernel(q_ref, k_ref, v_ref, o_ref, acc_ref, den_ref, mx_ref, bias_ref, *, seq):
    blk, hd, grp = ATT_BLOCK, ATT_HEAD_DIM, ATT_GROUP
    row = lax.broadcasted_iota(jnp.int32, (blk, 2 * blk), 0)
    col = lax.broadcasted_iota(jnp.int32, (blk, 2 * blk), 1)
    dist = blk + row - col
    band = jnp.logical_and(dist >= 0, dist <= blk)
    bias_ref[1] = jnp.where(band, 0.0, -jnp.inf)
    bias_ref[0] = jnp.where(jnp.logical_and(band, col >= blk), 0.0, -jnp.inf)
    ones = jnp.ones((grp, 2 * blk, hd), BF16)

    for gi, (window, dil) in enumerate(DILATED_PATTERNS):
        assert window // dil == blk
        n_blocks = seq // (dil * blk)
        assert n_blocks & (n_blocks - 1) == 0 and (dil * n_blocks) % grp == 0
        log_nb = n_blocks.bit_length() - 1

        def rows(r, n, dil=dil):
            start = r + n * (blk * dil)
            if dil == 1:
                return pl.ds(pl.multiple_of(start, blk), blk)
            return pl.ds(start, blk, stride=dil)

        def group(it, carry, gi=gi, n_blocks=n_blocks, log_nb=log_nb, rows=rows):
            cur, qs, ks, vs, bias = [], [], [], [], []
            for gg in range(grp):
                idx = it * grp + gg
                r = lax.shift_right_logical(idx, log_nb)
                n = jnp.bitwise_and(idx, n_blocks - 1)
                c = rows(r, n)
                p = rows(r, jnp.maximum(n - 1, 0))
                cur.append(c)
                qs.append(q_ref[c, :])
                ks.append(jnp.concatenate([k_ref[p, :], k_ref[c, :]], axis=0))
                vs.append(jnp.concatenate([v_ref[p, :], v_ref[c, :]], axis=0))
                bias.append(bias_ref[jnp.minimum(n, 1)])
            q = jnp.stack(qs).astype(BF16)
            k = jnp.stack(ks).astype(BF16)
            v = jnp.concatenate([jnp.stack(vs).astype(BF16), ones], axis=-1)
            s = jnp.einsum("gqd,gkd->gqk", q, k, preferred_element_type=F32) + jnp.stack(bias)
            mx = jnp.max(s, axis=-1, keepdims=True)
            p = jnp.exp(s - mx).astype(BF16)
            pv = jnp.einsum("gqk,gkd->gqd", p, v, preferred_element_type=F32)
            for gg in range(grp):
                acc_ref[gi, cur[gg], :] = pv[gg, :, :hd]
                den_ref[gi, cur[gg], :] = pv[gg, :, hd:]
                mx_ref[gi, cur[gg], :] = jnp.broadcast_to(mx[gg], (blk, hd))
            return carry

        lax.fori_loop(0, dil * n_blocks // grp, group, 0)

    def combine(c, carry):
        sl = pl.ds(pl.multiple_of(c * blk, blk), blk)
        m0 = mx_ref[0, sl, :]
        m1 = mx_ref[1, sl, :]
        m2 = mx_ref[2, sl, :]
        mx = jnp.maximum(jnp.maximum(m0, m1), m2)
        w0 = jnp.exp(m0 - mx)
        w1 = jnp.exp(m1 - mx)
        w2 = jnp.exp(m2 - mx)
        num = w0 * acc_ref[0, sl, :] + w1 * acc_ref[1, sl, :] + w2 * acc_ref[2, sl, :]
        den = w0 * den_ref[0, sl, :] + w1 * den_ref[1, sl, :] + w2 * den_ref[2, sl, :]
        o_ref[sl, :] = (num / den).astype(o_ref.dtype)
        return carry

    lax.fori_loop(0, seq // blk, combine, 0)


def _attn_call(qkv, batch, seq):
    qkv = qkv.reshape(batch, seq, 3 * D_MODEL)
    hd = ATT_HEAD_DIM
    n_groups = len(DILATED_PATTERNS)
    return pl.pallas_call(
        functools.partial(_attn_kernel, seq=seq),
        grid=(batch, ATT_HEADS),
        in_specs=[
            pl.BlockSpec((None, seq, hd), lambda b, h: (b, 0, h)),
            pl.BlockSpec((None, seq, hd), lambda b, h: (b, 0, ATT_HEADS + h)),
            pl.BlockSpec((None, seq, hd), lambda b, h: (b, 0, 2 * ATT_HEADS + h)),
        ],
        out_specs=pl.BlockSpec((None, seq, hd), lambda b, h: (b, 0, h)),
        out_shape=jax.ShapeDtypeStruct((batch, seq, D_MODEL), BF16),
        scratch_shapes=[
            pltpu.VMEM((n_groups, seq, hd), F32),
            pltpu.VMEM((n_groups, seq, hd), F32),
            pltpu.VMEM((n_groups, seq, hd), F32),
            pltpu.VMEM((2, ATT_BLOCK, 2 * ATT_BLOCK), F32),
        ],
        compiler_params=_params(("parallel", "parallel")),
        name="dilated_attn",
    )(qkv, qkv, qkv)


def _proj_kernel(a_ref, w_ref, x_ref, o_ref):
    o_ref[...] = x_ref[...] + _dot(a_ref[...], w_ref[...])


def _proj_call(a, w, x, *, tm=512, tn=1024):
    m, k = a.shape
    n = w.shape[1]
    return pl.pallas_call(
        _proj_kernel,
        grid=(m // tm, n // tn),
        in_specs=[
            pl.BlockSpec((tm, k), lambda i, j: (i, 0)),
            pl.BlockSpec((k, tn), lambda i, j: (0, j)),
            pl.BlockSpec((tm, tn), lambda i, j: (i, j)),
        ],
        out_specs=pl.BlockSpec((tm, tn), lambda i, j: (i, j)),
        out_shape=jax.ShapeDtypeStruct((m, n), F32),
        compiler_params=_params(("parallel", "parallel")),
        name="proj_residual",
    )(a, w, x)


def _fill_normed(h_ref, x_ref, xh_ref, g_ref, seq_start):
    g = g_ref[...]
    halo = jnp.where(seq_start, 0.0, _rms(xh_ref[...], g))
    h_ref[0:HALO, :] = halo.astype(BF16)
    h_ref[HALO:, :] = _rms(x_ref[...], g).astype(BF16)


def _causal_conv(u, w_ref, b_ref, taps, tm):
    w = w_ref[...]
    y = b_ref[...]
    for j in range(taps):
        off = HALO - (taps - 1) + j
        y = y + w[j:j + 1, :] * u[off:off + tm, :]
    return y


def _ffn_kernel(x_ref, xh_ref, g_ref, wg_ref, wu_ref, cwg_ref, cwu_ref, cbg_ref, cbu_ref, wd_ref,
                o_ref, h_ref, acc_ref, *, tm, tiles_per_seq):
    i = pl.program_id(0)
    j = pl.program_id(1)

    @pl.when(j == 0)
    def _():
        _fill_normed(h_ref, x_ref, xh_ref, g_ref, i % tiles_per_seq == 0)
        acc_ref[...] = jnp.zeros_like(acc_ref)

    h = h_ref[...]
    gate = _causal_conv(_dot(h, wg_ref[...]), cwg_ref, cbg_ref, FFN_CONV, tm)
    up = _causal_conv(_dot(h, wu_ref[...]), cwu_ref, cbu_ref, FFN_CONV, tm)
    act = (gate * jax.nn.sigmoid(gate) * up).astype(BF16)
    acc_ref[...] += _dot(act, wd_ref[...])

    @pl.when(j == pl.num_programs(1) - 1)
    def _():
        o_ref[...] = x_ref[...] + acc_ref[...]


def _ffn_call(x, g, w_up, conv_w, conv_b, w_down, seq, *, tm=512, tf=512):
    m = x.shape[0]
    nf = FFN_DIM // tf
    halo_blocks_per_tile = tm // HALO
    return pl.pallas_call(
        functools.partial(_ffn_kernel, tm=tm, tiles_per_seq=seq // tm),
        grid=(m // tm, nf),
        in_specs=[
            pl.BlockSpec((tm, D_MODEL), lambda i, j: (i, 0)),
            pl.BlockSpec((HALO, D_MODEL), lambda i, j: (jnp.maximum(i * halo_blocks_per_tile - 1, 0), 0)),
            pl.BlockSpec((1, D_MODEL), lambda i, j: (0, 0)),
            pl.BlockSpec((D_MODEL, tf), lambda i, j: (0, j)),
            pl.BlockSpec((D_MODEL, tf), lambda i, j: (0, nf + j)),
            pl.BlockSpec((FFN_CONV, tf), lambda i, j: (0, j)),
            pl.BlockSpec((FFN_CONV, tf), lambda i, j: (0, nf + j)),
            pl.BlockSpec((1, tf), lambda i, j: (0, j)),
            pl.BlockSpec((1, tf), lambda i, j: (0, nf + j)),
            pl.BlockSpec((tf, D_MODEL), lambda i, j: (j, 0)),
        ],
        out_specs=pl.BlockSpec((tm, D_MODEL), lambda i, j: (i, 0)),
        out_shape=jax.ShapeDtypeStruct((m, D_MODEL), F32),
        scratch_shapes=[
            pltpu.VMEM((tm + HALO, D_MODEL), BF16),
            pltpu.VMEM((tm, D_MODEL), F32),
        ],
        compiler_params=_params(("parallel", "arbitrary")),
        name="conv_ffn",
    )(x, x, g, w_up, w_up, conv_w, conv_w, conv_b, conv_b, w_down)


def _lstm_in_kernel(x_ref, xh_ref, g_ref, w_ref, wgate_ref, bgate_ref, cw_ref, cb_ref,
                    z_ref, gates_ref, h_ref, *, tm, tiles_per_seq, n_conv_blocks):
    i = pl.program_id(0)
    j = pl.program_id(1)

    @pl.when(j == 0)
    def _():
        _fill_normed(h_ref, x_ref, xh_ref, g_ref, i % tiles_per_seq == 0)
        gates_ref[...] = _dot(h_ref[HALO:, :], wgate_ref[...]) + bgate_ref[...]

    z = _dot(h_ref[...], w_ref[...])

    @pl.when(j < n_conv_blocks)
    def _():
        y = _causal_conv(z, cw_ref, cb_ref, LSTM_CONV, tm)
        scale = jnp.where(j < n_conv_blocks // 2, 1.0, LSTM_QK_DIM ** -0.5)
        z_ref[...] = y * jax.nn.sigmoid(y) * scale

    @pl.when(j >= n_conv_blocks)
    def _():
        z_ref[...] = z[HALO:, :]


def _lstm_in_call(x, g, w_main, w_gate, b_gate, conv_w, conv_b, seq, *, tm=512, tn=512):
    m = x.shape[0]
    n = w_main.shape[1]
    n_conv_blocks = 2 * LSTM_QK_WIDTH // tn
    halo_blocks_per_tile = tm // HALO
    return pl.pallas_call(
        functools.partial(_lstm_in_kernel, tm=tm, tiles_per_seq=seq // tm, n_conv_blocks=n_conv_blocks),
        grid=(m // tm, n // tn),
        in_specs=[
            pl.BlockSpec((tm, D_MODEL), lambda i, j: (i, 0)),
            pl.BlockSpec((HALO, D_MODEL), lambda i, j: (jnp.maximum(i * halo_blocks_per_tile - 1, 0), 0)),
            pl.BlockSpec((1, D_MODEL), lambda i, j: (0, 0)),
            pl.BlockSpec((D_MODEL, tn), lambda i, j: (0, j)),
            pl.BlockSpec((D_MODEL, LANES), lambda i, j: (0, 0)),
            pl.BlockSpec((1, LANES), lambda i, j: (0, 0)),
            pl.BlockSpec((LSTM_CONV, tn), lambda i, j: (0, jnp.minimum(j, n_conv_blocks - 1))),
            pl.BlockSpec((1, tn), lambda i, j: (0, jnp.minimum(j, n_conv_blocks - 1))),
        ],
        out_specs=[
            pl.BlockSpec((tm, tn), lambda i, j: (i, j)),
            pl.BlockSpec((tm, LANES), lambda i, j: (i, 0)),
        ],
        out_shape=[
            jax.ShapeDtypeStruct((m, n), F32),
            jax.ShapeDtypeStruct((m, LANES), F32),
        ],
        scratch_shapes=[pltpu.VMEM((tm + HALO, D_MODEL), BF16)],
        compiler_params=_params(("parallel", "arbitrary")),
        name="lstm_in",
    )(x, x, g, w_main, w_gate, b_gate, conv_w, conv_b)


def _log_sigmoid(x):
    return -(jnp.maximum(-x, 0.0) + jnp.log1p(jnp.exp(-jnp.abs(x))))


def _cumsum_rows(x):
    n = x.shape[0]
    row = lax.broadcasted_iota(jnp.int32, x.shape, 0)
    shift = 1
    while shift < n:
        x = x + jnp.where(row >= shift, pltpu.roll(x, shift, axis=0), 0.0)
        shift *= 2
    return x


def _mlstm_kernel(q_ref, k_ref, v_ref, og_ref, gates_ref, hg_ref, o_ref, c_ref, n_ref, m_ref):
    chunk = q_ref.shape[0]
    dk, dv, heads = LSTM_QK_DIM, LSTM_V_DIM, LSTM_HEADS

    @pl.when(pl.program_id(1) == 0)
    def _():
        c_ref[...] = jnp.zeros_like(c_ref)
        n_ref[...] = jnp.zeros_like(n_ref)
        m_ref[...] = jnp.zeros_like(m_ref)

    gates = gates_ref[...]
    cum_f = _cumsum_rows(_log_sigmoid(gates))
    gates_t = gates.T
    cum_f_t = cum_f.T
    row = lax.broadcasted_iota(jnp.int32, (chunk, chunk), 0)
    col = lax.broadcasted_iota(jnp.int32, (chunk, chunk), 1)
    causal = col <= row

    for hd in range(heads):
        qs = slice(hd * dk, (hd + 1) * dk)
        vs = slice(hd * dv, (hd + 1) * dv)
        b_col = cum_f[:, heads + hd:heads + hd + 1]
        b_row = cum_f_t[heads + hd:heads + hd + 1, :]
        li_col = gates[:, hd:hd + 1]
        li_row = gates_t[hd:hd + 1, :]
        m_prev = m_ref[hd]
        c_prev = c_ref[hd]
        n_prev = n_ref[hd]
        q = q_ref[:, qs]
        k = k_ref[:, qs]
        v = v_ref[:, vs]
        qb = q.astype(BF16)
        kb = k.astype(BF16)

        dmat = jnp.where(causal, b_col - b_row + li_row, -jnp.inf)
        g = b_col + m_prev
        m_t = jnp.maximum(g, jnp.max(dmat, axis=-1, keepdims=True))
        p = jnp.exp(dmat - m_t)
        inter = jnp.exp(g - m_t)
        w = p * _dot_nt(qb, kb)
        num = inter * _dot(qb, c_prev.astype(BF16)) + _dot(w.astype(BF16), v.astype(BF16))
        den = inter * jnp.sum(q * n_prev, axis=-1, keepdims=True) + jnp.sum(w, axis=-1, keepdims=True)
        h = num / jnp.maximum(jnp.abs(den), jnp.exp(-m_t))

        b_last = b_col[chunk - 1:chunk, :]
        a = b_last - b_col + li_col
        m_new = jnp.maximum(b_last + m_prev, jnp.max(a, axis=0, keepdims=True))
        decay = jnp.exp(b_last + m_prev - m_new)
        wts = jnp.exp(a - m_new)
        c_ref[hd] = decay * c_prev + _dot_tn(kb, (wts * v).astype(BF16))
        n_ref[hd] = decay * n_prev + jnp.sum(wts * k, axis=0, keepdims=True)
        m_ref[hd] = m_new

        hs = _rms(h, hg_ref[:, vs]) * jax.nn.sigmoid(og_ref[:, vs])
        o_ref[:, vs] = hs.astype(o_ref.dtype)


def _mlstm_call(z, gates, head_gain, batch, seq):
    chunk = LSTM_CHUNK
    z = z.reshape(batch, seq, LSTM_MAIN_WIDTH)
    gates = gates.reshape(batch, seq, LANES)
    qkw = LSTM_QK_WIDTH
    return pl.pallas_call(
        _mlstm_kernel,
        grid=(batch, seq // chunk),
        in_specs=[
            pl.BlockSpec((None, chunk, qkw), lambda b, c: (b, c, 0)),
            pl.BlockSpec((None, chunk, qkw), lambda b, c: (b, c, 1)),
            pl.BlockSpec((None, chunk, D_MODEL), lambda b, c: (b, c, 2 * qkw // D_MODEL)),
            pl.BlockSpec((None, chunk, D_MODEL), lambda b, c: (b, c, 2 * qkw // D_MODEL + 1)),
            pl.BlockSpec((None, chunk, LANES), lambda b, c: (b, c, 0)),
            pl.BlockSpec((1, D_MODEL), lambda b, c: (0, 0)),
        ],
        out_specs=pl.BlockSpec((None, chunk, D_MODEL), lambda b, c: (b, c, 0)),
        out_shape=jax.ShapeDtypeStruct((batch, seq, D_MODEL), BF16),
        scratch_shapes=[
            pltpu.VMEM((LSTM_HEADS, LSTM_QK_DIM, LSTM_V_DIM), F32),
            pltpu.VMEM((LSTM_HEADS, 1, LSTM_QK_DIM), F32),
            pltpu.VMEM((LSTM_HEADS, 1, 1), F32),
        ],
        compiler_params=_params(("parallel", "arbitrary")),
        name="mlstm",
    )(z, z, z, z, gates, head_gain)


def kernel(x, attn_norm, attn_w_qkv, attn_q_gain, attn_k_gain, attn_w_o, lstm_norm, lstm_w_in, lstm_gate_bias,
           lstm_conv_w, lstm_conv_b, lstm_head_gain, lstm_w_out, ffn_norm, ffn_w_up, ffn_conv_w, ffn_conv_b,
           ffn_w_down):
    batch, seq, d = x.shape
    assert d == D_MODEL and ffn_norm.shape[0] == 2 and attn_norm.shape[0] == 1 and lstm_norm.shape[0] == 1
    m = batch * seq
    xf = x.reshape(m, d)

    def ffn(xin, layer):
        return _ffn_call(xin, ffn_norm[layer][None], ffn_w_up[layer].astype(BF16), ffn_conv_w[layer],
                         ffn_conv_b[layer][None], ffn_w_down[layer].astype(BF16), seq)

    qkv = _qkv_call(xf, attn_norm[0][None], attn_w_qkv[0].astype(BF16), attn_q_gain[0][None], attn_k_gain[0][None])
    att = _attn_call(qkv, batch, seq)
    xf = _proj_call(att.reshape(m, d), attn_w_o[0].astype(BF16), xf)
    xf = ffn(xf, 0)

    n_gates = 2 * LSTM_HEADS
    w_in = lstm_w_in[0]
    w_gate = jnp.pad(w_in[:, LSTM_MAIN_WIDTH:], ((0, 0), (0, LANES - n_gates))).astype(BF16)
    b_gate = jnp.pad(lstm_gate_bias[0], (0, LANES - n_gates))[None]
    z, gates = _lstm_in_call(xf, lstm_norm[0][None], w_in[:, :LSTM_MAIN_WIDTH].astype(BF16), w_gate, b_gate,
                             lstm_conv_w[0], lstm_conv_b[0][None], seq)
    hs = _mlstm_call(z, gates, lstm_head_gain[0][None], batch, seq)
    xf = _proj_call(hs.reshape(m, d), lstm_w_out[0].astype(BF16), xf)
    xf = ffn(xf, 1)
    return xf.reshape(batch, seq, d)
```

```python
import functools

import jax
import jax.numpy as jnp
from jax import lax
from jax.experimental import pallas as pl
from jax.experimental.pallas import tpu as pltpu

F32 = jnp.float32
BF16 = jnp.bfloat16

D_MODEL = 2048
ATT_HEADS = 16
ATT_HEAD_DIM = D_MODEL // ATT_HEADS
DILATED_PATTERNS = ((128, 1), (512, 4), (2048, 16))
ATT_BLOCK = 128
ATT_GROUP = 4
LSTM_HEADS = 4
LSTM_V_DIM = D_MODEL // LSTM_HEADS
LSTM_QK_DIM = LSTM_V_DIM // 2
LSTM_QK_WIDTH = LSTM_HEADS * LSTM_QK_DIM
LSTM_MAIN_WIDTH = 2 * LSTM_QK_WIDTH + 2 * D_MODEL
LSTM_CONV = 4
FFN_DIM = ((8 * D_MODEL // 3 + 255) // 256) * 256
FFN_CONV = 3
NORM_EPS = 1e-6

LANES = 128
SUBLANES = 8
HALO = SUBLANES
VMEM_LIMIT = 56 * 1024 * 1024

LSTM_CHUNK = 256


def _rms(x, g):
    ms = jnp.mean(x * x, axis=-1, keepdims=True)
    return x * lax.rsqrt(ms + NORM_EPS) * g


def _dot(a, b):
    return jnp.dot(a, b, preferred_element_type=F32)


def _dot_nt(a, b):
    return lax.dot_general(a, b, (((1,), (1,)), ((), ())), preferred_element_type=F32)


def _dot_tn(a, b):
    return lax.dot_general(a, b, (((0,), (0,)), ((), ())), preferred_element_type=F32)


def _params(sem):
    return pltpu.CompilerParams(dimension_semantics=sem, vmem_limit_bytes=VMEM_LIMIT)


def _row_chunks(h_ref, w_refs, rows, lead, with_halo, consume):
    n_chunks = (h_ref.shape[0] - lead) // rows
    assert not with_halo or lead == HALO

    def project(c, tails):
        if with_halo and c == 0:
            h = h_ref[0:rows + lead, :]
            return [_dot(h, w[...]) for w in w_refs]
        h = h_ref[c * rows + lead:(c + 1) * rows + lead, :]
        us = [_dot(h, w[...]) for w in w_refs]
        if with_halo:
            us = [jnp.concatenate([t, u], axis=0) for t, u in zip(tails, us)]
        return us

    us = project(0, None)
    for c in range(n_chunks):
        nxt = project(c + 1, [u[rows:, :] for u in us]) if c + 1 < n_chunks else None
        consume(c, us)
        us = nxt


def _qkv_kernel(x_ref, g_ref, w_ref, qg_ref, kg_ref, o_ref, h_ref, *, rows, n_head_blocks, heads_per_block):
    j = pl.program_id(1)

    @pl.when(j == 0)
    def _():
        h_ref[...] = _rms(x_ref[...], g_ref[...]).astype(BF16)

    @pl.when(j < 2 * n_head_blocks)
    def _():
        gain = jnp.where(j < n_head_blocks, qg_ref[...] * (ATT_HEAD_DIM ** -0.5), kg_ref[...])

        def qk_norm(c, us):
            for hh in range(heads_per_block):
                sl = slice(hh * ATT_HEAD_DIM, (hh + 1) * ATT_HEAD_DIM)
                o_ref[c * rows:(c + 1) * rows, sl] = _rms(us[0][:, sl], gain)

        _row_chunks(h_ref, [w_ref], rows, 0, False, qk_norm)

    @pl.when(j >= 2 * n_head_blocks)
    def _():
        def copy(c, us):
            o_ref[c * rows:(c + 1) * rows, :] = us[0]

        _row_chunks(h_ref, [w_ref], rows, 0, False, copy)


def _qkv_call(x, g, w, qg, kg, *, tm=1024, tn=512, rows=256):
    m = x.shape[0]
    n = w.shape[1]
    heads_per_block = tn // ATT_HEAD_DIM
    n_head_blocks = D_MODEL // tn
    return pl.pallas_call(
        functools.partial(_qkv_kernel, rows=rows, n_head_blocks=n_head_blocks, heads_per_block=heads_per_block),
        grid=(m // tm, n // tn),
        in_specs=[
            pl.BlockSpec((tm, D_MODEL), lambda i, j: (i, 0)),
            pl.BlockSpec((1, D_MODEL), lambda i, j: (0, 0)),
            pl.BlockSpec((D_MODEL, tn), lambda i, j: (0, j)),
            pl.BlockSpec((1, ATT_HEAD_DIM), lambda i, j: (0, 0)),
            pl.BlockSpec((1, ATT_HEAD_DIM), lambda i, j: (0, 0)),
        ],
        out_specs=pl.BlockSpec((tm, tn), lambda i, j: (i, j)),
        out_shape=jax.ShapeDtypeStruct((m, n), F32),
        scratch_shapes=[pltpu.VMEM((tm, D_MODEL), BF16)],
        compiler_params=_params(("parallel", "arbitrary")),
        name="attn_qkv",
    )(x, g, w, qg, kg)


def _attn_kernel(q_ref, k_ref, v_ref, o_ref, acc_ref, den_ref, mx_ref, bias_ref, *, seq):
    blk, hd, grp = ATT_BLOCK, ATT_HEAD_DIM, ATT_GROUP
    row = lax.broadcasted_iota(jnp.int32, (blk, 2 * blk), 0)
    col = lax.broadcasted_iota(jnp.int32, (blk, 2 * blk), 1)
    dist = blk + row - col
    band = jnp.logical_and(dist >= 0, dist <= blk)
    bias_ref[1] = jnp.where(band, 0.0, -jnp.inf)
    bias_ref[0] = jnp.where(jnp.logical_and(band, col >= blk), 0.0, -jnp.inf)
    ones = jnp.ones((grp, 2 * blk, hd), BF16)

    for gi, (window, dil) in enumerate(DILATED_PATTERNS):
        assert window // dil == blk
        n_blocks = seq // (dil * blk)
        assert n_blocks & (n_blocks - 1) == 0 and (dil * n_blocks) % grp == 0
        log_nb = n_blocks.bit_length() - 1

        def rows(r, n, dil=dil):
            start = r + n * (blk * dil)
            if dil == 1:
                return pl.ds(pl.multiple_of(start, blk), blk)
            return pl.ds(start, blk, stride=dil)

        def group(it, carry, gi=gi, n_blocks=n_blocks, log_nb=log_nb, rows=rows):
            cur, qs, ks, vs, bias = [], [], [], [], []
            for gg in range(grp):
                idx = it * grp + gg
                r = lax.shift_right_logical(idx, log_nb)
                n = jnp.bitwise_and(idx, n_blocks - 1)
                c = rows(r, n)
                p = rows(r, jnp.maximum(n - 1, 0))
                cur.append(c)
                qs.append(q_ref[c, :])
                ks.append(jnp.concatenate([k_ref[p, :], k_ref[c, :]], axis=0))
                vs.append(jnp.concatenate([v_ref[p, :], v_ref[c, :]], axis=0))
                bias.append(bias_ref[jnp.minimum(n, 1)])
            q = jnp.stack(qs).astype(BF16)
            k = jnp.stack(ks).astype(BF16)
            v = jnp.concatenate([jnp.stack(vs).astype(BF16), ones], axis=-1)
            s = jnp.einsum("gqd,gkd->gqk", q, k, preferred_element_type=F32) + jnp.stack(bias)
            mx = jnp.max(s, axis=-1, keepdims=True)
            p = jnp.exp(s - mx).astype(BF16)
            pv = jnp.einsum("gqk,gkd->gqd", p, v, preferred_element_type=F32)
            for gg in range(grp):
                acc_ref[gi, cur[gg], :] = pv[gg, :, :hd]
                den_ref[gi, cur[gg], :] = pv[gg, :, hd:]
                mx_ref[gi, cur[gg], :] = jnp.broadcast_to(mx[gg], (blk, hd))
            return carry

        lax.fori_loop(0, dil * n_blocks // grp, group, 0)

    def combine(c, carry):
        sl = pl.ds(pl.multiple_of(c * blk, blk), blk)
        m0 = mx_ref[0, sl, :]
        m1 = mx_ref[1, sl, :]
        m2 = mx_ref[2, sl, :]
        mx = jnp.maximum(jnp.maximum(m0, m1), m2)
        w0 = jnp.exp(m0 - mx)
        w1 = jnp.exp(m1 - mx)
        w2 = jnp.exp(m2 - mx)
        num = w0 * acc_ref[0, sl, :] + w1 * acc_ref[1, sl, :] + w2 * acc_ref[2, sl, :]
        den = w0 * den_ref[0, sl, :] + w1 * den_ref[1, sl, :] + w2 * den_ref[2, sl, :]
        o_ref[sl, :] = (num / den).astype(o_ref.dtype)
        return carry

    lax.fori_loop(0, seq // blk, combine, 0)


def _attn_call(qkv, batch, seq):
    qkv = qkv.reshape(batch, seq, 3 * D_MODEL)
    hd = ATT_HEAD_DIM
    n_groups = len(DILATED_PATTERNS)
    return pl.pallas_call(
        functools.partial(_attn_kernel, seq=seq),
        grid=(batch, ATT_HEADS),
        in_specs=[
            pl.BlockSpec((None, seq, hd), lambda b, h: (b, 0, h)),
            pl.BlockSpec((None, seq, hd), lambda b, h: (b, 0, ATT_HEADS + h)),
            pl.BlockSpec((None, seq, hd), lambda b, h: (b, 0, 2 * ATT_HEADS + h)),
        ],
        out_specs=pl.BlockSpec((None, seq, hd), lambda b, h: (b, 0, h)),
        out_shape=jax.ShapeDtypeStruct((batch, seq, D_MODEL), BF16),
        scratch_shapes=[
            pltpu.VMEM((n_groups, seq, hd), F32),
            pltpu.VMEM((n_groups, seq, hd), F32),
            pltpu.VMEM((n_groups, seq, hd), F32),
            pltpu.VMEM((2, ATT_BLOCK, 2 * ATT_BLOCK), F32),
        ],
        compiler_params=_params(("parallel", "parallel")),
        name="dilated_attn",
    )(qkv, qkv, qkv)


def _proj_kernel(a_ref, w_ref, x_ref, o_ref, *, rows):
    for c in range(o_ref.shape[0] // rows):
        sl = slice(c * rows, (c + 1) * rows)
        o_ref[sl, :] = x_ref[sl, :] + _dot(a_ref[sl, :], w_ref[...])


def _proj_call(a, w, x, *, tm=512, rows=256):
    m, k = a.shape
    n = w.shape[1]
    return pl.pallas_call(
        functools.partial(_proj_kernel, rows=rows),
        grid=(m // tm,),
        in_specs=[
            pl.BlockSpec((tm, k), lambda i: (i, 0)),
            pl.BlockSpec((k, n), lambda i: (0, 0)),
            pl.BlockSpec((tm, n), lambda i: (i, 0)),
        ],
        out_specs=pl.BlockSpec((tm, n), lambda i: (i, 0)),
        out_shape=jax.ShapeDtypeStruct((m, n), F32),
        compiler_params=_params(("parallel",)),
        name="proj_residual",
    )(a, w, x)


def _fill_normed(h_ref, x_ref, xh_ref, g_ref, seq_start):
    g = g_ref[...]
    halo = jnp.where(seq_start, 0.0, _rms(xh_ref[...], g))
    h_ref[0:HALO, :] = halo.astype(BF16)
    h_ref[HALO:, :] = _rms(x_ref[...], g).astype(BF16)


def _causal_conv(u, w_ref, b_ref, taps, rows):
    w = w_ref[...]
    y = b_ref[...]
    for j in range(taps):
        off = HALO - (taps - 1) + j
        y = y + w[j:j + 1, :] * u[off:off + rows, :]
    return y


def _ffn_kernel(x_ref, xh_ref, g_ref, wg_ref, wu_ref, cwg_ref, cwu_ref, cbg_ref, cbu_ref, wd_ref,
                o_ref, h_ref, *, rows, tiles_per_seq):
    i = pl.program_id(0)
    j = pl.program_id(1)

    @pl.when(j == 0)
    def _():
        _fill_normed(h_ref, x_ref, xh_ref, g_ref, i % tiles_per_seq == 0)
        o_ref[...] = x_ref[...]

    def down_proj(c, us):
        gate = _causal_conv(us[0], cwg_ref, cbg_ref, FFN_CONV, rows)
        up = _causal_conv(us[1], cwu_ref, cbu_ref, FFN_CONV, rows)
        act = (gate * jax.nn.sigmoid(gate) * up).astype(BF16)
        o_ref[c * rows:(c + 1) * rows, :] += _dot(act, wd_ref[...])

    _row_chunks(h_ref, [wg_ref, wu_ref], rows, HALO, True, down_proj)


def _ffn_call(x, g, w_up, conv_w, conv_b, w_down, layer, seq, *, tm=1024, tf=512, rows=256):
    m = x.shape[0]
    nf = FFN_DIM // tf
    halo_blocks_per_tile = tm // HALO
    return pl.pallas_call(
        functools.partial(_ffn_kernel, rows=rows, tiles_per_seq=seq // tm),
        grid=(m // tm, nf),
        in_specs=[
            pl.BlockSpec((tm, D_MODEL), lambda i, j: (i, 0), pipeline_mode=pl.Buffered(1)),
            pl.BlockSpec((HALO, D_MODEL), lambda i, j: (jnp.maximum(i * halo_blocks_per_tile - 1, 0), 0)),
            pl.BlockSpec((None, 1, D_MODEL), lambda i, j: (layer, 0, 0)),
            pl.BlockSpec((None, D_MODEL, tf), lambda i, j: (layer, 0, j)),
            pl.BlockSpec((None, D_MODEL, tf), lambda i, j: (layer, 0, nf + j)),
            pl.BlockSpec((None, FFN_CONV, tf), lambda i, j: (layer, 0, j)),
            pl.BlockSpec((None, FFN_CONV, tf), lambda i, j: (layer, 0, nf + j)),
            pl.BlockSpec((None, 1, tf), lambda i, j: (layer, 0, j)),
            pl.BlockSpec((None, 1, tf), lambda i, j: (layer, 0, nf + j)),
            pl.BlockSpec((None, tf, D_MODEL), lambda i, j: (layer, j, 0)),
        ],
        out_specs=pl.BlockSpec((tm, D_MODEL), lambda i, j: (i, 0)),
        out_shape=jax.ShapeDtypeStruct((m, D_MODEL), F32),
        scratch_shapes=[pltpu.VMEM((tm + HALO, D_MODEL), BF16)],
        compiler_params=_params(("parallel", "arbitrary")),
        name="conv_ffn",
    )(x, x, g, w_up, w_up, conv_w, conv_w, conv_b, conv_b, w_down)


def _lstm_in_kernel(x_ref, xh_ref, g_ref, w_ref, wgate_ref, bgate_ref, cw_ref, cb_ref,
                    z_ref, gates_ref, h_ref, *, rows, tiles_per_seq, n_conv_blocks):
    i = pl.program_id(0)
    j = pl.program_id(1)

    @pl.when(j == 0)
    def _():
        _fill_normed(h_ref, x_ref, xh_ref, g_ref, i % tiles_per_seq == 0)
        gates_ref[...] = _dot(h_ref[HALO:, :], wgate_ref[...]) + bgate_ref[...]

    @pl.when(j < n_conv_blocks)
    def _():
        scale = jnp.where(j < n_conv_blocks // 2, 1.0, LSTM_QK_DIM ** -0.5)

        def conv_silu(c, us):
            y = _causal_conv(us[0], cw_ref, cb_ref, LSTM_CONV, rows)
            z_ref[c * rows:(c + 1) * rows, :] = y * jax.nn.sigmoid(y) * scale

        _row_chunks(h_ref, [w_ref], rows, HALO, True, conv_silu)

    @pl.when(j >= n_conv_blocks)
    def _():
        def copy(c, us):
            z_ref[c * rows:(c + 1) * rows, :] = us[0]

        _row_chunks(h_ref, [w_ref], rows, HALO, False, copy)


def _lstm_in_call(x, g, w_main, w_gate, b_gate, conv_w, conv_b, seq, *, tm=1024, tn=512, rows=256):
    m = x.shape[0]
    n = LSTM_MAIN_WIDTH
    n_conv_blocks = 2 * LSTM_QK_WIDTH // tn
    halo_blocks_per_tile = tm // HALO
    return pl.pallas_call(
        functools.partial(_lstm_in_kernel, rows=rows, tiles_per_seq=seq // tm, n_conv_blocks=n_conv_blocks),
        grid=(m // tm, n // tn),
        in_specs=[
            pl.BlockSpec((tm, D_MODEL), lambda i, j: (i, 0)),
            pl.BlockSpec((HALO, D_MODEL), lambda i, j: (jnp.maximum(i * halo_blocks_per_tile - 1, 0), 0)),
            pl.BlockSpec((1, D_MODEL), lambda i, j: (0, 0)),
            pl.BlockSpec((D_MODEL, tn), lambda i, j: (0, j)),
            pl.BlockSpec((D_MODEL, LANES), lambda i, j: (0, 0)),
            pl.BlockSpec((1, LANES), lambda i, j: (0, 0)),
            pl.BlockSpec((LSTM_CONV, tn), lambda i, j: (0, jnp.minimum(j, n_conv_blocks - 1))),
            pl.BlockSpec((1, tn), lambda i, j: (0, jnp.minimum(j, n_conv_blocks - 1))),
        ],
        out_specs=[
            pl.BlockSpec((tm, tn), lambda i, j: (i, j)),
            pl.BlockSpec((tm, LANES), lambda i, j: (i, 0)),
        ],
        out_shape=[
            jax.ShapeDtypeStruct((m, n), F32),
            jax.ShapeDtypeStruct((m, LANES), F32),
        ],
        scratch_shapes=[pltpu.VMEM((tm + HALO, D_MODEL), BF16)],
        compiler_params=_params(("parallel", "arbitrary")),
        name="lstm_in",
    )(x, x, g, w_main, w_gate, b_gate, conv_w, conv_b)


def _log_sigmoid(x):
    return -(jnp.maximum(-x, 0.0) + jnp.log1p(jnp.exp(-jnp.abs(x))))


def _cumsum_rows(x):
    n = x.shape[0]
    row = lax.broadcasted_iota(jnp.int32, x.shape, 0)
    shift = 1
    while shift < n:
        x = x + jnp.where(row >= shift, pltpu.roll(x, shift, axis=0), 0.0)
        shift *= 2
    return x


def _mlstm_kernel(q_ref, k_ref, v_ref, og_ref, gates_ref, hg_ref, o_ref, c_ref, n_ref, m_ref):
    chunk = q_ref.shape[0]
    dk, dv, heads = LSTM_QK_DIM, LSTM_V_DIM, LSTM_HEADS

    @pl.when(pl.program_id(1) == 0)
    def _():
        c_ref[...] = jnp.zeros_like(c_ref)
        n_ref[...] = jnp.zeros_like(n_ref)
        m_ref[...] = jnp.zeros_like(m_ref)

    gates = gates_ref[...]
    cum_f = _cumsum_rows(_log_sigmoid(gates))
    gates_t = gates.T
    cum_f_t = cum_f.T
    row = lax.broadcasted_iota(jnp.int32, (chunk, chunk), 0)
    col = lax.broadcasted_iota(jnp.int32, (chunk, chunk), 1)
    causal = col <= row

    for hd in range(heads):
        qs = slice(hd * dk, (hd + 1) * dk)
        vs = slice(hd * dv, (hd + 1) * dv)
        b_col = cum_f[:, heads + hd:heads + hd + 1]
        b_row = cum_f_t[heads + hd:heads + hd + 1, :]
        li_col = gates[:, hd:hd + 1]
        li_row = gates_t[hd:hd + 1, :]
        m_prev = m_ref[hd]
        c_prev = c_ref[hd]
        n_prev = n_ref[hd]
        q = q_ref[:, qs]
        k = k_ref[:, qs]
        v = v_ref[:, vs]
        qb = q.astype(BF16)
        kb = k.astype(BF16)

        dmat = jnp.where(causal, b_col - b_row + li_row, -jnp.inf)
        g = b_col + m_prev
        m_t = jnp.maximum(g, jnp.max(dmat, axis=-1, keepdims=True))
        p = jnp.exp(dmat - m_t)
        inter = jnp.exp(g - m_t)
        w = p * _dot_nt(qb, kb)
        num = inter * _dot(qb, c_prev.astype(BF16)) + _dot(w.astype(BF16), v.astype(BF16))
        den = inter * jnp.sum(q * n_prev, axis=-1, keepdims=True) + jnp.sum(w, axis=-1, keepdims=True)
        h = num / jnp.maximum(jnp.abs(den), jnp.exp(-m_t))

        b_last = b_col[chunk - 1:chunk, :]
        a = b_last - b_col + li_col
        m_new = jnp.maximum(b_last + m_prev, jnp.max(a, axis=0, keepdims=True))
        decay = jnp.exp(b_last + m_prev - m_new)
        wts = jnp.exp(a - m_new)
        c_ref[hd] = decay * c_prev + _dot_tn(kb, (wts * v).astype(BF16))
        n_ref[hd] = decay * n_prev + jnp.sum(wts * k, axis=0, keepdims=True)
        m_ref[hd] = m_new

        hs = _rms(h, hg_ref[:, vs]) * jax.nn.sigmoid(og_ref[:, vs])
        o_ref[:, vs] = hs.astype(o_ref.dtype)


def _mlstm_call(z, gates, head_gain, batch, seq):
    chunk = LSTM_CHUNK
    z = z.reshape(batch, seq, LSTM_MAIN_WIDTH)
    gates = gates.reshape(batch, seq, LANES)
    qkw = LSTM_QK_WIDTH
    return pl.pallas_call(
        _mlstm_kernel,
        grid=(batch, seq // chunk),
        in_specs=[
            pl.BlockSpec((None, chunk, qkw), lambda b, c: (b, c, 0)),
            pl.BlockSpec((None, chunk, qkw), lambda b, c: (b, c, 1)),
            pl.BlockSpec((None, chunk, D_MODEL), lambda b, c: (b, c, 2 * qkw // D_MODEL)),
            pl.BlockSpec((None, chunk, D_MODEL), lambda b, c: (b, c, 2 * qkw // D_MODEL + 1)),
            pl.BlockSpec((None, chunk, LANES), lambda b, c: (b, c, 0)),
            pl.BlockSpec((1, D_MODEL), lambda b, c: (0, 0)),
        ],
        out_specs=pl.BlockSpec((None, chunk, D_MODEL), lambda b, c: (b, c, 0)),
        out_shape=jax.ShapeDtypeStruct((batch, seq, D_MODEL), BF16),
        scratch_shapes=[
            pltpu.VMEM((LSTM_HEADS, LSTM_QK_DIM, LSTM_V_DIM), F32),
            pltpu.VMEM((LSTM_HEADS, 1, LSTM_QK_DIM), F32),
            pltpu.VMEM((LSTM_HEADS, 1, 1), F32),
        ],
        compiler_params=_params(("parallel", "arbitrary")),
        name="mlstm",
    )(z, z, z, z, gates, head_gain)


def kernel(x, attn_norm, attn_w_qkv, attn_q_gain, attn_k_gain, attn_w_o, lstm_norm, lstm_w_in, lstm_gate_bias,
           lstm_conv_w, lstm_conv_b, lstm_head_gain, lstm_w_out, ffn_norm, ffn_w_up, ffn_conv_w, ffn_conv_b,
           ffn_w_down):
    batch, seq, d = x.shape
    assert d == D_MODEL and ffn_norm.shape[0] == 2 and attn_norm.shape[0] == 1 and lstm_norm.shape[0] == 1
    m = batch * seq
    xf = x.reshape(m, d)

    w_up = ffn_w_up.astype(BF16)
    w_down = ffn_w_down.astype(BF16)

    def ffn(xin, layer):
        return _ffn_call(xin, ffn_norm[:, None, :], w_up, ffn_conv_w, ffn_conv_b[:, None, :], w_down, layer, seq)

    qkv = _qkv_call(xf, attn_norm[0][None], attn_w_qkv[0].astype(BF16), attn_q_gain[0][None], attn_k_gain[0][None])
    att = _attn_call(qkv, batch, seq)
    xf = _proj_call(att.reshape(m, d), attn_w_o[0].astype(BF16), xf)
    xf = ffn(xf, 0)

    n_gates = 2 * LSTM_HEADS
    w_in = lstm_w_in[0]
    w_gate = jnp.pad(w_in[:, LSTM_MAIN_WIDTH:], ((0, 0), (0, LANES - n_gates))).astype(BF16)
    b_gate = jnp.pad(lstm_gate_bias[0], (0, LANES - n_gates))[None]
    z, gates = _lstm_in_call(xf, lstm_norm[0][None], w_in.astype(BF16), w_gate, b_gate,
                             lstm_conv_w[0], lstm_conv_b[0][None], seq)
    hs = _mlstm_call(z, gates, lstm_head_gain[0][None], batch, seq)
    xf = _proj_call(hs.reshape(m, d), lstm_w_out[0].astype(BF16), xf)
    xf = ffn(xf, 1)
    return xf.reshape(batch, seq, d)
```

```python
import functools

import jax
import jax.numpy as jnp
from jax import lax
from jax.experimental import pallas as pl
from jax.experimental.pallas import tpu as pltpu

F32 = jnp.float32
BF16 = jnp.bfloat16

D_MODEL = 2048
ATT_HEADS = 16
ATT_HEAD_DIM = D_MODEL // ATT_HEADS
DILATED_PATTERNS = ((128, 1), (512, 4), (2048, 16))
ATT_BLOCK = 128
ATT_GROUP = 4
LSTM_HEADS = 4
LSTM_V_DIM = D_MODEL // LSTM_HEADS
LSTM_QK_DIM = LSTM_V_DIM // 2
LSTM_QK_WIDTH = LSTM_HEADS * LSTM_QK_DIM
LSTM_MAIN_WIDTH = 2 * LSTM_QK_WIDTH + 2 * D_MODEL
LSTM_CONV = 4
FFN_DIM = ((8 * D_MODEL // 3 + 255) // 256) * 256
FFN_CONV = 3
NORM_EPS = 1e-6

LANES = 128
SUBLANES = 8
BF16_ROWS_PER_VREG = 2 * SUBLANES
HALO = BF16_ROWS_PER_VREG
VMEM_LIMIT = 56 * 1024 * 1024

LSTM_CHUNK = 256

NORM_ROWS = 128


def _rms(x, g):
    ms = jnp.mean(x * x, axis=-1, keepdims=True)
    return x * lax.rsqrt(ms + NORM_EPS) * g


def _norm_rows(h_ref, lead, x_ref, g, copy_ref=None):
    def body(r, carry):
        start = pl.multiple_of(r * NORM_ROWS, NORM_ROWS)
        x = x_ref[pl.ds(start, NORM_ROWS), :]
        h_ref[pl.ds(pl.multiple_of(start + lead, BF16_ROWS_PER_VREG), NORM_ROWS), :] = _rms(x, g).astype(BF16)
        if copy_ref is not None:
            copy_ref[pl.ds(start, NORM_ROWS), :] = x
        return carry

    lax.fori_loop(0, x_ref.shape[0] // NORM_ROWS, body, 0)


def _dot(a, b):
    return jnp.dot(a, b, preferred_element_type=F32)


def _dot_nt(a, b):
    return lax.dot_general(a, b, (((1,), (1,)), ((), ())), preferred_element_type=F32)


def _dot_tn(a, b):
    return lax.dot_general(a, b, (((0,), (0,)), ((), ())), preferred_element_type=F32)


def _params(sem):
    return pltpu.CompilerParams(dimension_semantics=sem, vmem_limit_bytes=VMEM_LIMIT)


def _row_chunks(h_ref, w_refs, rows, lead, with_halo, consume):
    n_chunks = (h_ref.shape[0] - lead) // rows
    assert not with_halo or lead == HALO

    def project(c, tails):
        if with_halo and c == 0:
            h = h_ref[0:rows + lead, :]
            return [_dot(h, w[...]) for w in w_refs]
        h = h_ref[c * rows + lead:(c + 1) * rows + lead, :]
        us = [_dot(h, w[...]) for w in w_refs]
        if with_halo:
            us = [jnp.concatenate([t, u], axis=0) for t, u in zip(tails, us)]
        return us

    us = project(0, None)
    for c in range(n_chunks):
        nxt = project(c + 1, [u[rows:, :] for u in us]) if c + 1 < n_chunks else None
        consume(c, us)
        us = nxt


def _row_chunks_conv(h_ref, w_refs, u_refs, rows, consume):
    n_chunks = (h_ref.shape[0] - HALO) // rows

    def project(c):
        lo = 0 if c == 0 else c * rows + HALO
        hi = (c + 1) * rows + HALO
        h = h_ref[lo:hi, :]
        for w, u in zip(w_refs, u_refs):
            u[lo:hi, :] = _dot(h, w[...])

    project(0)
    for c in range(n_chunks):
        if c + 1 < n_chunks:
            project(c + 1)
        consume(c)


def _qkv_kernel(x_ref, g_ref, w_ref, qg_ref, kg_ref, o_ref, h_ref, *, rows, n_head_blocks, heads_per_block):
    j = pl.program_id(1)

    @pl.when(j == 0)
    def _():
        _norm_rows(h_ref, 0, x_ref, g_ref[...])

    @pl.when(j < 2 * n_head_blocks)
    def _():
        gain = jnp.where(j < n_head_blocks, qg_ref[...] * (ATT_HEAD_DIM ** -0.5), kg_ref[...])

        def qk_norm(c, us):
            for hh in range(heads_per_block):
                sl = slice(hh * ATT_HEAD_DIM, (hh + 1) * ATT_HEAD_DIM)
                o_ref[c * rows:(c + 1) * rows, sl] = _rms(us[0][:, sl], gain)

        _row_chunks(h_ref, [w_ref], rows, 0, False, qk_norm)

    @pl.when(j >= 2 * n_head_blocks)
    def _():
        def copy(c, us):
            o_ref[c * rows:(c + 1) * rows, :] = us[0]

        _row_chunks(h_ref, [w_ref], rows, 0, False, copy)


def _qkv_call(x, g, w, qg, kg, *, tm=1024, tn=512, rows=256):
    m = x.shape[0]
    n = w.shape[1]
    heads_per_block = tn // ATT_HEAD_DIM
    n_head_blocks = D_MODEL // tn
    return pl.pallas_call(
        functools.partial(_qkv_kernel, rows=rows, n_head_blocks=n_head_blocks, heads_per_block=heads_per_block),
        grid=(m // tm, n // tn),
        in_specs=[
            pl.BlockSpec((tm, D_MODEL), lambda i, j: (i, 0)),
            pl.BlockSpec((1, D_MODEL), lambda i, j: (0, 0)),
            pl.BlockSpec((D_MODEL, tn), lambda i, j: (0, j)),
            pl.BlockSpec((1, ATT_HEAD_DIM), lambda i, j: (0, 0)),
            pl.BlockSpec((1, ATT_HEAD_DIM), lambda i, j: (0, 0)),
        ],
        out_specs=pl.BlockSpec((tm, tn), lambda i, j: (i, j)),
        out_shape=jax.ShapeDtypeStruct((m, n), F32),
        scratch_shapes=[pltpu.VMEM((tm, D_MODEL), BF16)],
        compiler_params=_params(("parallel", "arbitrary")),
        name="attn_qkv",
    )(x, g, w, qg, kg)


def _attn_kernel(q_ref, k_ref, v_ref, o_ref, acc_ref, den_ref, mx_ref, bias_ref, *, seq):
    blk, hd, grp = ATT_BLOCK, ATT_HEAD_DIM, ATT_GROUP
    row = lax.broadcasted_iota(jnp.int32, (blk, 2 * blk), 0)
    col = lax.broadcasted_iota(jnp.int32, (blk, 2 * blk), 1)
    dist = blk + row - col
    band = jnp.logical_and(dist >= 0, dist <= blk)
    bias_ref[1] = jnp.where(band, 0.0, -jnp.inf)
    bias_ref[0] = jnp.where(jnp.logical_and(band, col >= blk), 0.0, -jnp.inf)
    ones = jnp.ones((grp, 2 * blk, hd), BF16)

    for gi, (window, dil) in enumerate(DILATED_PATTERNS):
        assert window // dil == blk
        n_blocks = seq // (dil * blk)
        assert n_blocks & (n_blocks - 1) == 0 and (dil * n_blocks) % grp == 0
        log_nb = n_blocks.bit_length() - 1

        def rows(r, n, dil=dil):
            start = r + n * (blk * dil)
            if dil == 1:
                return pl.ds(pl.multiple_of(start, blk), blk)
            return pl.ds(start, blk, stride=dil)

        def group(it, carry, gi=gi, n_blocks=n_blocks, log_nb=log_nb, rows=rows):
            cur, qs, ks, vs, bias = [], [], [], [], []
            for gg in range(grp):
                idx = it * grp + gg
                r = lax.shift_right_logical(idx, log_nb)
                n = jnp.bitwise_and(idx, n_blocks - 1)
                c = rows(r, n)
                p = rows(r, jnp.maximum(n - 1, 0))
                cur.append(c)
                qs.append(q_ref[c, :])
                ks.append(jnp.concatenate([k_ref[p, :], k_ref[c, :]], axis=0))
                vs.append(jnp.concatenate([v_ref[p, :], v_ref[c, :]], axis=0))
                bias.append(bias_ref[jnp.minimum(n, 1)])
            q = jnp.stack(qs).astype(BF16)
            k = jnp.stack(ks).astype(BF16)
            v = jnp.concatenate([jnp.stack(vs).astype(BF16), ones], axis=-1)
            s = jnp.einsum("gqd,gkd->gqk", q, k, preferred_element_type=F32) + jnp.stack(bias)
            mx = jnp.max(s, axis=-1, keepdims=True)
            p = jnp.exp(s - mx).astype(BF16)
            pv = jnp.einsum("gqk,gkd->gqd", p, v, preferred_element_type=F32)
            for gg in range(grp):
                acc_ref[gi, cur[gg], :] = pv[gg, :, :hd]
                den_ref[gi, cur[gg], :] = pv[gg, :, hd:]
                mx_ref[gi, cur[gg], :] = jnp.broadcast_to(mx[gg], (blk, hd))
            return carry

        lax.fori_loop(0, dil * n_blocks // grp, group, 0)

    def combine(c, carry):
        sl = pl.ds(pl.multiple_of(c * blk, blk), blk)
        m0 = mx_ref[0, sl, :]
        m1 = mx_ref[1, sl, :]
        m2 = mx_ref[2, sl, :]
        mx = jnp.maximum(jnp.maximum(m0, m1), m2)
        w0 = jnp.exp(m0 - mx)
        w1 = jnp.exp(m1 - mx)
        w2 = jnp.exp(m2 - mx)
        num = w0 * acc_ref[0, sl, :] + w1 * acc_ref[1, sl, :] + w2 * acc_ref[2, sl, :]
        den = w0 * den_ref[0, sl, :] + w1 * den_ref[1, sl, :] + w2 * den_ref[2, sl, :]
        o_ref[sl, :] = (num / den).astype(o_ref.dtype)
        return carry

    lax.fori_loop(0, seq // blk, combine, 0)


def _attn_call(qkv, batch, seq):
    qkv = qkv.reshape(batch, seq, 3 * D_MODEL)
    hd = ATT_HEAD_DIM
    n_groups = len(DILATED_PATTERNS)
    return pl.pallas_call(
        functools.partial(_attn_kernel, seq=seq),
        grid=(batch, ATT_HEADS),
        in_specs=[
            pl.BlockSpec((None, seq, hd), lambda b, h: (b, 0, h)),
            pl.BlockSpec((None, seq, hd), lambda b, h: (b, 0, ATT_HEADS + h)),
            pl.BlockSpec((None, seq, hd), lambda b, h: (b, 0, 2 * ATT_HEADS + h)),
        ],
        out_specs=pl.BlockSpec((None, seq, hd), lambda b, h: (b, 0, h)),
        out_shape=jax.ShapeDtypeStruct((batch, seq, D_MODEL), BF16),
        scratch_shapes=[
            pltpu.VMEM((n_groups, seq, hd), F32),
            pltpu.VMEM((n_groups, seq, hd), F32),
            pltpu.VMEM((n_groups, seq, hd), F32),
            pltpu.VMEM((2, ATT_BLOCK, 2 * ATT_BLOCK), F32),
        ],
        compiler_params=_params(("parallel", "parallel")),
        name="dilated_attn",
    )(qkv, qkv, qkv)


def _proj_kernel(a_ref, w_ref, x_ref, o_ref, *, rows):
    for c in range(o_ref.shape[0] // rows):
        sl = slice(c * rows, (c + 1) * rows)
        o_ref[sl, :] = x_ref[sl, :] + _dot(a_ref[sl, :], w_ref[...])


def _proj_call(a, w, x, *, tm=512, rows=256):
    m, k = a.shape
    n = w.shape[1]
    return pl.pallas_call(
        functools.partial(_proj_kernel, rows=rows),
        grid=(m // tm,),
        in_specs=[
            pl.BlockSpec((tm, k), lambda i: (i, 0)),
            pl.BlockSpec((k, n), lambda i: (0, 0)),
            pl.BlockSpec((tm, n), lambda i: (i, 0)),
        ],
        out_specs=pl.BlockSpec((tm, n), lambda i: (i, 0)),
        out_shape=jax.ShapeDtypeStruct((m, n), F32),
        compiler_params=_params(("parallel",)),
        name="proj_residual",
    )(a, w, x)


def _fill_normed(h_ref, x_ref, xh_ref, g_ref, seq_start, copy_ref=None):
    g = g_ref[...]
    halo = jnp.where(seq_start, 0.0, _rms(xh_ref[...], g))
    h_ref[0:HALO, :] = halo.astype(BF16)
    _norm_rows(h_ref, HALO, x_ref, g, copy_ref)


def _causal_conv(u_ref, c, w_ref, b_ref, taps, rows):
    w = w_ref[...]
    y = b_ref[...]
    for j in range(taps):
        off = c * rows + HALO - (taps - 1) + j
        y = y + w[j:j + 1, :] * u_ref[off:off + rows, :]
    return y


def _ffn_kernel(x_ref, xh_ref, g_ref, wg_ref, wu_ref, cwg_ref, cwu_ref, cbg_ref, cbu_ref, wd_ref,
                o_ref, h_ref, ug_ref, uu_ref, *, rows, tiles_per_seq):
    i = pl.program_id(0)
    j = pl.program_id(1)

    @pl.when(j == 0)
    def _():
        _fill_normed(h_ref, x_ref, xh_ref, g_ref, i % tiles_per_seq == 0, copy_ref=o_ref)

    def down_proj(c):
        gate = _causal_conv(ug_ref, c, cwg_ref, cbg_ref, FFN_CONV, rows)
        up = _causal_conv(uu_ref, c, cwu_ref, cbu_ref, FFN_CONV, rows)
        act = (gate * up / (1.0 + jnp.exp(-gate))).astype(BF16)
        o_ref[c * rows:(c + 1) * rows, :] += _dot(act, wd_ref[...])

    _row_chunks_conv(h_ref, [wg_ref, wu_ref], [ug_ref, uu_ref], rows, down_proj)


def _ffn_call(x, g, w_up, conv_w, conv_b, w_down, layer, seq, *, tm=1024, tf=512, rows=256):
    m = x.shape[0]
    nf = FFN_DIM // tf
    halo_blocks_per_tile = tm // HALO
    return pl.pallas_call(
        functools.partial(_ffn_kernel, rows=rows, tiles_per_seq=seq // tm),
        grid=(m // tm, nf),
        in_specs=[
            pl.BlockSpec((tm, D_MODEL), lambda i, j: (i, 0)),
            pl.BlockSpec((HALO, D_MODEL), lambda i, j: (jnp.maximum(i * halo_blocks_per_tile - 1, 0), 0)),
            pl.BlockSpec((None, 1, D_MODEL), lambda i, j: (layer, 0, 0)),
            pl.BlockSpec((None, D_MODEL, tf), lambda i, j: (layer, 0, j)),
            pl.BlockSpec((None, D_MODEL, tf), lambda i, j: (layer, 0, nf + j)),
            pl.BlockSpec((None, FFN_CONV, tf), lambda i, j: (layer, 0, j)),
            pl.BlockSpec((None, FFN_CONV, tf), lambda i, j: (layer, 0, nf + j)),
            pl.BlockSpec((None, 1, tf), lambda i, j: (layer, 0, j)),
            pl.BlockSpec((None, 1, tf), lambda i, j: (layer, 0, nf + j)),
            pl.BlockSpec((None, tf, D_MODEL), lambda i, j: (layer, j, 0)),
        ],
        out_specs=pl.BlockSpec((tm, D_MODEL), lambda i, j: (i, 0)),
        out_shape=jax.ShapeDtypeStruct((m, D_MODEL), F32),
        scratch_shapes=[
            pltpu.VMEM((tm + HALO, D_MODEL), BF16),
            pltpu.VMEM((tm + HALO, tf), F32),
            pltpu.VMEM((tm + HALO, tf), F32),
        ],
        compiler_params=_params(("parallel", "arbitrary")),
        name="conv_ffn",
    )(x, x, g, w_up, w_up, conv_w, conv_w, conv_b, conv_b, w_down)


def _lstm_in_kernel(x_ref, xh_ref, g_ref, w_ref, wgate_ref, bgate_ref, cw_ref, cb_ref,
                    z_ref, gates_ref, h_ref, u_ref, *, rows, tiles_per_seq, n_conv_blocks):
    i = pl.program_id(0)
    j = pl.program_id(1)

    @pl.when(j == 0)
    def _():
        _fill_normed(h_ref, x_ref, xh_ref, g_ref, i % tiles_per_seq == 0)
        gates_ref[...] = _dot(h_ref[HALO:, :], wgate_ref[...]) + bgate_ref[...]

    @pl.when(j < n_conv_blocks)
    def _():
        scale = jnp.where(j < n_conv_blocks // 2, 1.0, LSTM_QK_DIM ** -0.5)

        def conv_silu(c):
            y = _causal_conv(u_ref, c, cw_ref, cb_ref, LSTM_CONV, rows)
            z_ref[c * rows:(c + 1) * rows, :] = y * scale / (1.0 + jnp.exp(-y))

        _row_chunks_conv(h_ref, [w_ref], [u_ref], rows, conv_silu)

    @pl.when(j >= n_conv_blocks)
    def _():
        def copy(c, us):
            z_ref[c * rows:(c + 1) * rows, :] = us[0]

        _row_chunks(h_ref, [w_ref], rows, HALO, False, copy)


def _lstm_in_call(x, g, w_main, w_gate, b_gate, conv_w, conv_b, seq, *, tm=1024, tn=512, rows=256):
    m = x.shape[0]
    n = LSTM_MAIN_WIDTH
    n_conv_blocks = 2 * LSTM_QK_WIDTH // tn
    halo_blocks_per_tile = tm // HALO
    return pl.pallas_call(
        functools.partial(_lstm_in_kernel, rows=rows, tiles_per_seq=seq // tm, n_conv_blocks=n_conv_blocks),
        grid=(m // tm, n // tn),
        in_specs=[
            pl.BlockSpec((tm, D_MODEL), lambda i, j: (i, 0)),
            pl.BlockSpec((HALO, D_MODEL), lambda i, j: (jnp.maximum(i * halo_blocks_per_tile - 1, 0), 0)),
            pl.BlockSpec((1, D_MODEL), lambda i, j: (0, 0)),
            pl.BlockSpec((D_MODEL, tn), lambda i, j: (0, j)),
            pl.BlockSpec((D_MODEL, LANES), lambda i, j: (0, 0)),
            pl.BlockSpec((1, LANES), lambda i, j: (0, 0)),
            pl.BlockSpec((LSTM_CONV, tn), lambda i, j: (0, jnp.minimum(j, n_conv_blocks - 1))),
            pl.BlockSpec((1, tn), lambda i, j: (0, jnp.minimum(j, n_conv_blocks - 1))),
        ],
        out_specs=[
            pl.BlockSpec((tm, tn), lambda i, j: (i, j)),
            pl.BlockSpec((tm, LANES), lambda i, j: (i, 0)),
        ],
        out_shape=[
            jax.ShapeDtypeStruct((m, n), F32),
            jax.ShapeDtypeStruct((m, LANES), F32),
        ],
        scratch_shapes=[
            pltpu.VMEM((tm + HALO, D_MODEL), BF16),
            pltpu.VMEM((tm + HALO, tn), F32),
        ],
        compiler_params=_params(("parallel", "arbitrary")),
        name="lstm_in",
    )(x, x, g, w_main, w_gate, b_gate, conv_w, conv_b)


def _log_sigmoid(x):
    return -(jnp.maximum(-x, 0.0) + jnp.log1p(jnp.exp(-jnp.abs(x))))


def _cumsum_rows(x):
    n = x.shape[0]
    row = lax.broadcasted_iota(jnp.int32, x.shape, 0)
    shift = 1
    while shift < n:
        x = x + jnp.where(row >= shift, pltpu.roll(x, shift, axis=0), 0.0)
        shift *= 2
    return x


def _mlstm_kernel(q_ref, k_ref, v_ref, og_ref, gates_ref, hg_ref, o_ref, c_ref, n_ref, m_ref):
    chunk = q_ref.shape[0]
    dk, dv, heads = LSTM_QK_DIM, LSTM_V_DIM, LSTM_HEADS

    @pl.when(pl.program_id(1) == 0)
    def _():
        c_ref[...] = jnp.zeros_like(c_ref)
        n_ref[...] = jnp.zeros_like(n_ref)
        m_ref[...] = jnp.zeros_like(m_ref)

    gates = gates_ref[...]
    cum_f = _cumsum_rows(_log_sigmoid(gates))
    gates_t = gates.T
    cum_f_t = cum_f.T
    row = lax.broadcasted_iota(jnp.int32, (chunk, chunk), 0)
    col = lax.broadcasted_iota(jnp.int32, (chunk, chunk), 1)
    causal = col <= row

    for hd in range(heads):
        qs = slice(hd * dk, (hd + 1) * dk)
        vs = slice(hd * dv, (hd + 1) * dv)
        b_col = cum_f[:, heads + hd:heads + hd + 1]
        b_row = cum_f_t[heads + hd:heads + hd + 1, :]
        li_col = gates[:, hd:hd + 1]
        li_row = gates_t[hd:hd + 1, :]
        m_prev = m_ref[hd]
        c_prev = c_ref[hd]
        n_prev = n_ref[hd]
        q = q_ref[:, qs]
        k = k_ref[:, qs]
        v = v_ref[:, vs]
        qb = q.astype(BF16)
        kb = k.astype(BF16)

        dmat = jnp.where(causal, b_col - b_row + li_row, -jnp.inf)
        g = b_col + m_prev
        m_t = jnp.maximum(g, jnp.max(dmat, axis=-1, keepdims=True))
        p = jnp.exp(dmat - m_t)
        inter = jnp.exp(g - m_t)
        w = p * _dot_nt(qb, kb)
        num = inter * _dot(qb, c_prev.astype(BF16)) + _dot(w.astype(BF16), v.astype(BF16))
        den = inter * jnp.sum(q * n_prev, axis=-1, keepdims=True) + jnp.sum(w, axis=-1, keepdims=True)
        h = num / jnp.maximum(jnp.abs(den), jnp.exp(-m_t))

        b_last = b_col[chunk - 1:chunk, :]
        a = b_last - b_col + li_col
        m_new = jnp.maximum(b_last + m_prev, jnp.max(a, axis=0, keepdims=True))
        decay = jnp.exp(b_last + m_prev - m_new)
        wts = jnp.exp(a - m_new)
        c_ref[hd] = decay * c_prev + _dot_tn(kb, (wts * v).astype(BF16))
        n_ref[hd] = decay * n_prev + jnp.sum(wts * k, axis=0, keepdims=True)
        m_ref[hd] = m_new

        hs = _rms(h, hg_ref[:, vs]) * jax.nn.sigmoid(og_ref[:, vs])
        o_ref[:, vs] = hs.astype(o_ref.dtype)


def _mlstm_call(z, gates, head_gain, batch, seq):
    chunk = LSTM_CHUNK
    z = z.reshape(batch, seq, LSTM_MAIN_WIDTH)
    gates = gates.reshape(batch, seq, LANES)
    qkw = LSTM_QK_WIDTH
    return pl.pallas_call(
        _mlstm_kernel,
        grid=(batch, seq // chunk),
        in_specs=[
            pl.BlockSpec((None, chunk, qkw), lambda b, c: (b, c, 0)),
            pl.BlockSpec((None, chunk, qkw), lambda b, c: (b, c, 1)),
            pl.BlockSpec((None, chunk, D_MODEL), lambda b, c: (b, c, 2 * qkw // D_MODEL)),
            pl.BlockSpec((None, chunk, D_MODEL), lambda b, c: (b, c, 2 * qkw // D_MODEL + 1)),
            pl.BlockSpec((None, chunk, LANES), lambda b, c: (b, c, 0)),
            pl.BlockSpec((1, D_MODEL), lambda b, c: (0, 0)),
        ],
        out_specs=pl.BlockSpec((None, chunk, D_MODEL), lambda b, c: (b, c, 0)),
        out_shape=jax.ShapeDtypeStruct((batch, seq, D_MODEL), BF16),
        scratch_shapes=[
            pltpu.VMEM((LSTM_HEADS, LSTM_QK_DIM, LSTM_V_DIM), F32),
            pltpu.VMEM((LSTM_HEADS, 1, LSTM_QK_DIM), F32),
            pltpu.VMEM((LSTM_HEADS, 1, 1), F32),
        ],
        compiler_params=_params(("parallel", "arbitrary")),
        name="mlstm",
    )(z, z, z, z, gates, head_gain)


def kernel(x, attn_norm, attn_w_qkv, attn_q_gain, attn_k_gain, attn_w_o, lstm_norm, lstm_w_in, lstm_gate_bias,
           lstm_conv_w, lstm_conv_b, lstm_head_gain, lstm_w_out, ffn_norm, ffn_w_up, ffn_conv_w, ffn_conv_b,
           ffn_w_down):
    batch, seq, d = x.shape
    assert d == D_MODEL and ffn_norm.shape[0] == 2 and attn_norm.shape[0] == 1 and lstm_norm.shape[0] == 1
    m = batch * seq
    xf = x.reshape(m, d)

    w_up = ffn_w_up.astype(BF16)
    w_down = ffn_w_down.astype(BF16)

    def ffn(xin, layer):
        return _ffn_call(xin, ffn_norm[:, None, :], w_up, ffn_conv_w, ffn_conv_b[:, None, :], w_down, layer, seq)

    qkv = _qkv_call(xf, attn_norm[0][None], attn_w_qkv[0].astype(BF16), attn_q_gain[0][None], attn_k_gain[0][None])
    att = _attn_call(qkv, batch, seq)
    xf = _proj_call(att.reshape(m, d), attn_w_o[0].astype(BF16), xf)
    xf = ffn(xf, 0)

    n_gates = 2 * LSTM_HEADS
    w_in = lstm_w_in[0]
    w_gate = jnp.pad(w_in[:, LSTM_MAIN_WIDTH:], ((0, 0), (0, LANES - n_gates))).astype(BF16)
    b_gate = jnp.pad(lstm_gate_bias[0], (0, LANES - n_gates))[None]
    z, gates = _lstm_in_call(xf, lstm_norm[0][None], w_in.astype(BF16), w_gate, b_gate,
                             lstm_conv_w[0], lstm_conv_b[0][None], seq)
    hs = _mlstm_call(z, gates, lstm_head_gain[0][None], batch, seq)
    xf = _proj_call(hs.reshape(m, d), lstm_w_out[0].astype(BF16), xf)
    xf = ffn(xf, 1)
    return xf.reshape(batch, seq, d)
```

```python
import functools

import jax
import jax.numpy as jnp
from jax import lax
from jax.experimental import pallas as pl
from jax.experimental.pallas import tpu as pltpu

F32 = jnp.float32
BF16 = jnp.bfloat16

D_MODEL = 2048
ATT_HEADS = 16
ATT_HEAD_DIM = D_MODEL // ATT_HEADS
DILATED_PATTERNS = ((128, 1), (512, 4), (2048, 16))
ATT_BLOCK = 128
ATT_GROUP = 4
LSTM_HEADS = 4
LSTM_V_DIM = D_MODEL // LSTM_HEADS
LSTM_QK_DIM = LSTM_V_DIM // 2
LSTM_QK_WIDTH = LSTM_HEADS * LSTM_QK_DIM
LSTM_MAIN_WIDTH = 2 * LSTM_QK_WIDTH + 2 * D_MODEL
LSTM_CONV = 4
FFN_DIM = ((8 * D_MODEL // 3 + 255) // 256) * 256
FFN_CONV = 3
NORM_EPS = 1e-6

LANES = 128
SUBLANES = 8
BF16_ROWS_PER_VREG = 2 * SUBLANES
HALO = BF16_ROWS_PER_VREG
VMEM_LIMIT = 56 * 1024 * 1024

LSTM_CHUNK = 256

NORM_ROWS = 128


def _rms(x, g):
    ms = jnp.mean(x * x, axis=-1, keepdims=True)
    return x * lax.rsqrt(ms + NORM_EPS) * g


def _norm_rows(h_ref, lead, x_ref, g, copy_ref=None):
    def body(r, carry):
        start = pl.multiple_of(r * NORM_ROWS, NORM_ROWS)
        x = x_ref[pl.ds(start, NORM_ROWS), :]
        h_ref[pl.ds(pl.multiple_of(start + lead, BF16_ROWS_PER_VREG), NORM_ROWS), :] = _rms(x, g).astype(BF16)
        if copy_ref is not None:
            copy_ref[pl.ds(start, NORM_ROWS), :] = x
        return carry

    lax.fori_loop(0, x_ref.shape[0] // NORM_ROWS, body, 0)


def _dot(a, b):
    return jnp.dot(a, b, preferred_element_type=F32)


def _dot_nt(a, b):
    return lax.dot_general(a, b, (((1,), (1,)), ((), ())), preferred_element_type=F32)


def _dot_tn(a, b):
    return lax.dot_general(a, b, (((0,), (0,)), ((), ())), preferred_element_type=F32)


def _params(sem):
    return pltpu.CompilerParams(dimension_semantics=sem, vmem_limit_bytes=VMEM_LIMIT)


def _row_chunks(h_ref, w_refs, rows, lead, with_halo, consume):
    n_chunks = (h_ref.shape[0] - lead) // rows
    assert not with_halo or lead == HALO

    def project(c, tails):
        if with_halo and c == 0:
            h = h_ref[0:rows + lead, :]
            return [_dot(h, w[...]) for w in w_refs]
        h = h_ref[c * rows + lead:(c + 1) * rows + lead, :]
        us = [_dot(h, w[...]) for w in w_refs]
        if with_halo:
            us = [jnp.concatenate([t, u], axis=0) for t, u in zip(tails, us)]
        return us

    us = project(0, None)
    for c in range(n_chunks):
        nxt = project(c + 1, [u[rows:, :] for u in us]) if c + 1 < n_chunks else None
        consume(c, us)
        us = nxt


def _row_chunks_conv(h_ref, w_refs, u_refs, rows, consume):
    n_chunks = (h_ref.shape[0] - HALO) // rows

    def project(c):
        lo = 0 if c == 0 else c * rows + HALO
        hi = (c + 1) * rows + HALO
        h = h_ref[lo:hi, :]
        for w, u in zip(w_refs, u_refs):
            u[lo:hi, :] = _dot(h, w[...])

    project(0)
    for c in range(n_chunks):
        if c + 1 < n_chunks:
            project(c + 1)
        consume(c)


RESIDUES = max(d for _, d in DILATED_PATTERNS)
QKV_RESIDUES_PER_TILE = 4
PROJ_RESIDUES_PER_TILE = 2


def _qkv_kernel(x_ref, g_ref, w_ref, qg_ref, kg_ref, o_ref, h_ref, *, rows, n_head_blocks, heads_per_block):
    j = pl.program_id(1)

    @pl.when(j == 0)
    def _():
        sub = x_ref.shape[0]
        for r in range(x_ref.shape[1] // D_MODEL):
            _norm_rows(h_ref, r * sub, x_ref.at[:, pl.ds(r * D_MODEL, D_MODEL)], g_ref[...])

    @pl.when(j < 2 * n_head_blocks)
    def _():
        gain = jnp.where(j < n_head_blocks, qg_ref[...] * (ATT_HEAD_DIM ** -0.5), kg_ref[...])

        def qk_norm(c, us):
            for hh in range(heads_per_block):
                sl = slice(hh * ATT_HEAD_DIM, (hh + 1) * ATT_HEAD_DIM)
                o_ref[c * rows:(c + 1) * rows, sl] = _rms(us[0][:, sl], gain)

        _row_chunks(h_ref, [w_ref], rows, 0, False, qk_norm)

    @pl.when(j >= 2 * n_head_blocks)
    def _():
        def copy(c, us):
            o_ref[c * rows:(c + 1) * rows, :] = us[0]

        _row_chunks(h_ref, [w_ref], rows, 0, False, copy)


def _qkv_call(x, g, w, qg, kg, *, tn=512, rows=256):
    batch, seq, _ = x.shape
    sub = seq // RESIDUES
    tiles_per_seq = RESIDUES // QKV_RESIDUES_PER_TILE
    tm = QKV_RESIDUES_PER_TILE * sub
    n = w.shape[1]
    heads_per_block = tn // ATT_HEAD_DIM
    n_head_blocks = D_MODEL // tn
    return pl.pallas_call(
        functools.partial(_qkv_kernel, rows=rows, n_head_blocks=n_head_blocks, heads_per_block=heads_per_block),
        grid=(batch * tiles_per_seq, n // tn),
        in_specs=[
            pl.BlockSpec((None, sub, QKV_RESIDUES_PER_TILE * D_MODEL),
                         lambda i, j: (i // tiles_per_seq, 0, i % tiles_per_seq)),
            pl.BlockSpec((1, D_MODEL), lambda i, j: (0, 0)),
            pl.BlockSpec((D_MODEL, tn), lambda i, j: (0, j)),
            pl.BlockSpec((1, ATT_HEAD_DIM), lambda i, j: (0, 0)),
            pl.BlockSpec((1, ATT_HEAD_DIM), lambda i, j: (0, 0)),
        ],
        out_specs=pl.BlockSpec((tm, tn), lambda i, j: (i, j)),
        out_shape=jax.ShapeDtypeStruct((batch * seq, n), F32),
        scratch_shapes=[pltpu.VMEM((tm, D_MODEL), BF16)],
        compiler_params=_params(("parallel", "arbitrary")),
        name="attn_qkv",
    )(x.reshape(batch, sub, RESIDUES * D_MODEL), g, w, qg, kg)


def _attn_kernel(q_ref, k_ref, v_ref, o_ref, acc_ref, den_ref, mx_ref, bias_ref, *, seq):
    blk, hd, grp = ATT_BLOCK, ATT_HEAD_DIM, ATT_GROUP
    sub = seq // RESIDUES
    row = lax.broadcasted_iota(jnp.int32, (blk, 2 * blk), 0)
    col = lax.broadcasted_iota(jnp.int32, (blk, 2 * blk), 1)
    ones = jnp.ones((grp, 2 * blk, hd), BF16)

    for gi, (window, dil) in enumerate(DILATED_PATTERNS):
        assert window // dil == blk and RESIDUES % dil == 0
        pieces = RESIDUES // dil
        plen = blk // pieces
        n_blocks = seq // (dil * blk)
        assert n_blocks & (n_blocks - 1) == 0 and (dil * n_blocks) % grp == 0 and plen % SUBLANES == 0
        log_nb = n_blocks.bit_length() - 1
        log_plen = plen.bit_length() - 1

        def step_in_block(i, pieces=pieces, plen=plen, log_plen=log_plen):
            return pieces * jnp.bitwise_and(i, plen - 1) + lax.shift_right_logical(i, log_plen)

        dist = step_in_block(row) - step_in_block(jnp.bitwise_and(col, blk - 1)) + jnp.where(col < blk, blk, 0)
        band = jnp.logical_and(dist >= 0, dist <= blk)
        bias_ref[2 * gi + 1] = jnp.where(band, 0.0, -jnp.inf)
        bias_ref[2 * gi] = jnp.where(jnp.logical_and(band, col >= blk), 0.0, -jnp.inf)

        def block_rows(a, n, dil=dil, pieces=pieces, plen=plen):
            return [pl.ds(pl.multiple_of((p * dil + a) * sub + n * plen, SUBLANES), plen) for p in range(pieces)]

        def load(ref, parts):
            return jnp.concatenate([ref[d, :] for d in parts], axis=0)

        def store(ref, gi, parts, val, plen=plen):
            for p, d in enumerate(parts):
                ref[gi, d, :] = val[p * plen:(p + 1) * plen, :]

        def group(it, carry, gi=gi, n_blocks=n_blocks, log_nb=log_nb, block_rows=block_rows, load=load, store=store):
            cur, qs, ks, vs, bias = [], [], [], [], []
            for gg in range(grp):
                idx = it * grp + gg
                a = lax.shift_right_logical(idx, log_nb)
                n = jnp.bitwise_and(idx, n_blocks - 1)
                c = block_rows(a, n)
                p = block_rows(a, jnp.maximum(n - 1, 0))
                cur.append(c)
                qs.append(load(q_ref, c))
                ks.append(jnp.concatenate([load(k_ref, p), load(k_ref, c)], axis=0))
                vs.append(jnp.concatenate([load(v_ref, p), load(v_ref, c)], axis=0))
                bias.append(bias_ref[2 * gi + jnp.minimum(n, 1)])
            q = jnp.stack(qs).astype(BF16)
            k = jnp.stack(ks).astype(BF16)
            v = jnp.concatenate([jnp.stack(vs).astype(BF16), ones], axis=-1)
            s = jnp.einsum("gqd,gkd->gqk", q, k, preferred_element_type=F32) + jnp.stack(bias)
            mx = jnp.max(s, axis=-1, keepdims=True)
            p = jnp.exp(s - mx).astype(BF16)
            pv = jnp.einsum("gqk,gkd->gqd", p, v, preferred_element_type=F32)
            for gg in range(grp):
                store(acc_ref, gi, cur[gg], pv[gg, :, :hd])
                store(den_ref, gi, cur[gg], pv[gg, :, hd:])
                store(mx_ref, gi, cur[gg], jnp.broadcast_to(mx[gg], (blk, hd)))
            return carry

        lax.fori_loop(0, dil * n_blocks // grp, group, 0)

    def combine(c, carry):
        sl = pl.ds(pl.multiple_of(c * blk, blk), blk)
        m0 = mx_ref[0, sl, :]
        m1 = mx_ref[1, sl, :]
        m2 = mx_ref[2, sl, :]
        mx = jnp.maximum(jnp.maximum(m0, m1), m2)
        w0 = jnp.exp(m0 - mx)
        w1 = jnp.exp(m1 - mx)
        w2 = jnp.exp(m2 - mx)
        num = w0 * acc_ref[0, sl, :] + w1 * acc_ref[1, sl, :] + w2 * acc_ref[2, sl, :]
        den = w0 * den_ref[0, sl, :] + w1 * den_ref[1, sl, :] + w2 * den_ref[2, sl, :]
        o_ref[sl, :] = (num / den).astype(o_ref.dtype)
        return carry

    lax.fori_loop(0, seq // blk, combine, 0)


def _attn_call(qkv, batch, seq):
    qkv = qkv.reshape(batch, seq, 3 * D_MODEL)
    hd = ATT_HEAD_DIM
    n_groups = len(DILATED_PATTERNS)
    out = pl.pallas_call(
        functools.partial(_attn_kernel, seq=seq),
        grid=(batch, ATT_HEADS),
        in_specs=[
            pl.BlockSpec((None, seq, hd), lambda b, h: (b, 0, h)),
            pl.BlockSpec((None, seq, hd), lambda b, h: (b, 0, ATT_HEADS + h)),
            pl.BlockSpec((None, seq, hd), lambda b, h: (b, 0, 2 * ATT_HEADS + h)),
        ],
        out_specs=pl.BlockSpec((None, seq, hd), lambda b, h: (b, 0, h)),
        out_shape=jax.ShapeDtypeStruct((batch, seq, D_MODEL), BF16),
        scratch_shapes=[
            pltpu.VMEM((n_groups, seq, hd), F32),
            pltpu.VMEM((n_groups, seq, hd), F32),
            pltpu.VMEM((n_groups, seq, hd), F32),
            pltpu.VMEM((2 * n_groups, ATT_BLOCK, 2 * ATT_BLOCK), F32),
        ],
        compiler_params=_params(("parallel", "parallel")),
        name="dilated_attn",
    )(qkv, qkv, qkv)
    return out.reshape(batch * seq, D_MODEL)


def _proj_kernel(a_ref, w_ref, x_ref, o_ref, *, rows):
    n = w_ref.shape[1]
    part_rows = o_ref.shape[0]
    for p in range(o_ref.shape[1] // n):
        cols = slice(p * n, (p + 1) * n)
        for c in range(part_rows // rows):
            sl = slice(c * rows, (c + 1) * rows)
            a = a_ref[p * part_rows + c * rows:p * part_rows + (c + 1) * rows, :]
            o_ref[sl, cols] = x_ref[sl, cols] + _dot(a, w_ref[...])


def _proj_from_residue_major_call(a, w, x, *, rows=256):
    batch, seq, n = x.shape
    k = a.shape[1]
    sub = seq // RESIDUES
    parts = PROJ_RESIDUES_PER_TILE
    tiles_per_seq = RESIDUES // parts
    view = (batch, sub, RESIDUES * n)
    out = pl.pallas_call(
        functools.partial(_proj_kernel, rows=rows),
        grid=(batch * tiles_per_seq,),
        in_specs=[
            pl.BlockSpec((parts * sub, k), lambda i: (i, 0)),
            pl.BlockSpec((k, n), lambda i: (0, 0)),
            pl.BlockSpec((None, sub, parts * n), lambda i: (i // tiles_per_seq, 0, i % tiles_per_seq)),
        ],
        out_specs=pl.BlockSpec((None, sub, parts * n), lambda i: (i // tiles_per_seq, 0, i % tiles_per_seq)),
        out_shape=jax.ShapeDtypeStruct(view, F32),
        compiler_params=_params(("parallel",)),
        name="proj_residual_attn",
    )(a, w, x.reshape(view))
    return out.reshape(batch * seq, n)


def _proj_call(a, w, x, *, tm=512, rows=256):
    m, k = a.shape
    n = w.shape[1]
    return pl.pallas_call(
        functools.partial(_proj_kernel, rows=rows),
        grid=(m // tm,),
        in_specs=[
            pl.BlockSpec((tm, k), lambda i: (i, 0)),
            pl.BlockSpec((k, n), lambda i: (0, 0)),
            pl.BlockSpec((tm, n), lambda i: (i, 0)),
        ],
        out_specs=pl.BlockSpec((tm, n), lambda i: (i, 0)),
        out_shape=jax.ShapeDtypeStruct((m, n), F32),
        compiler_params=_params(("parallel",)),
        name="proj_residual",
    )(a, w, x)


def _fill_normed(h_ref, x_ref, xh_ref, g_ref, seq_start, copy_ref=None):
    g = g_ref[...]
    halo = jnp.where(seq_start, 0.0, _rms(xh_ref[...], g))
    h_ref[0:HALO, :] = halo.astype(BF16)
    _norm_rows(h_ref, HALO, x_ref, g, copy_ref)


def _causal_conv(u_ref, c, w_ref, b_ref, taps, rows):
    w = w_ref[...]
    y = b_ref[...]
    for j in range(taps):
        off = c * rows + HALO - (taps - 1) + j
        y = y + w[j:j + 1, :] * u_ref[off:off + rows, :]
    return y


def _ffn_kernel(x_ref, xh_ref, g_ref, wg_ref, wu_ref, cwg_ref, cwu_ref, cbg_ref, cbu_ref, wd_ref,
                o_ref, h_ref, ug_ref, uu_ref, *, rows, tiles_per_seq):
    i = pl.program_id(0)
    j = pl.program_id(1)

    @pl.when(j == 0)
    def _():
        _fill_normed(h_ref, x_ref, xh_ref, g_ref, i % tiles_per_seq == 0, copy_ref=o_ref)

    def down_proj(c):
        gate = _causal_conv(ug_ref, c, cwg_ref, cbg_ref, FFN_CONV, rows)
        up = _causal_conv(uu_ref, c, cwu_ref, cbu_ref, FFN_CONV, rows)
        act = (gate * up / (1.0 + jnp.exp(-gate))).astype(BF16)
        o_ref[c * rows:(c + 1) * rows, :] += _dot(act, wd_ref[...])

    _row_chunks_conv(h_ref, [wg_ref, wu_ref], [ug_ref, uu_ref], rows, down_proj)


def _ffn_call(x, g, w_up, conv_w, conv_b, w_down, layer, seq, *, tm=1024, tf=512, rows=256):
    m = x.shape[0]
    nf = FFN_DIM // tf
    halo_blocks_per_tile = tm // HALO
    return pl.pallas_call(
        functools.partial(_ffn_kernel, rows=rows, tiles_per_seq=seq // tm),
        grid=(m // tm, nf),
        in_specs=[
            pl.BlockSpec((tm, D_MODEL), lambda i, j: (i, 0)),
            pl.BlockSpec((HALO, D_MODEL), lambda i, j: (jnp.maximum(i * halo_blocks_per_tile - 1, 0), 0)),
            pl.BlockSpec((None, 1, D_MODEL), lambda i, j: (layer, 0, 0)),
            pl.BlockSpec((None, D_MODEL, tf), lambda i, j: (layer, 0, j)),
            pl.BlockSpec((None, D_MODEL, tf), lambda i, j: (layer, 0, nf + j)),
            pl.BlockSpec((None, FFN_CONV, tf), lambda i, j: (layer, 0, j)),
            pl.BlockSpec((None, FFN_CONV, tf), lambda i, j: (layer, 0, nf + j)),
            pl.BlockSpec((None, 1, tf), lambda i, j: (layer, 0, j)),
            pl.BlockSpec((None, 1, tf), lambda i, j: (layer, 0, nf + j)),
            pl.BlockSpec((None, tf, D_MODEL), lambda i, j: (layer, j, 0)),
        ],
        out_specs=pl.BlockSpec((tm, D_MODEL), lambda i, j: (i, 0)),
        out_shape=jax.ShapeDtypeStruct((m, D_MODEL), F32),
        scratch_shapes=[
            pltpu.VMEM((tm + HALO, D_MODEL), BF16),
            pltpu.VMEM((tm + HALO, tf), F32),
            pltpu.VMEM((tm + HALO, tf), F32),
        ],
        compiler_params=_params(("parallel", "arbitrary")),
        name="conv_ffn",
    )(x, x, g, w_up, w_up, conv_w, conv_w, conv_b, conv_b, w_down)


def _lstm_in_kernel(x_ref, xh_ref, g_ref, w_ref, wgate_ref, bgate_ref, cw_ref, cb_ref,
                    z_ref, gates_ref, h_ref, u_ref, *, rows, tiles_per_seq, n_conv_blocks):
    i = pl.program_id(0)
    j = pl.program_id(1)

    @pl.when(j == 0)
    def _():
        _fill_normed(h_ref, x_ref, xh_ref, g_ref, i % tiles_per_seq == 0)
        gates_ref[...] = _dot(h_ref[HALO:, :], wgate_ref[...]) + bgate_ref[...]

    @pl.when(j < n_conv_blocks)
    def _():
        scale = jnp.where(j < n_conv_blocks // 2, 1.0, LSTM_QK_DIM ** -0.5)

        def conv_silu(c):
            y = _causal_conv(u_ref, c, cw_ref, cb_ref, LSTM_CONV, rows)
            z_ref[c * rows:(c + 1) * rows, :] = y * scale / (1.0 + jnp.exp(-y))

        _row_chunks_conv(h_ref, [w_ref], [u_ref], rows, conv_silu)

    @pl.when(j >= n_conv_blocks)
    def _():
        def copy(c, us):
            z_ref[c * rows:(c + 1) * rows, :] = us[0]

        _row_chunks(h_ref, [w_ref], rows, HALO, False, copy)


def _lstm_in_call(x, g, w_main, w_gate, b_gate, conv_w, conv_b, seq, *, tm=1024, tn=512, rows=256):
    m = x.shape[0]
    n = LSTM_MAIN_WIDTH
    n_conv_blocks = 2 * LSTM_QK_WIDTH // tn
    halo_blocks_per_tile = tm // HALO
    return pl.pallas_call(
        functools.partial(_lstm_in_kernel, rows=rows, tiles_per_seq=seq // tm, n_conv_blocks=n_conv_blocks),
        grid=(m // tm, n // tn),
        in_specs=[
            pl.BlockSpec((tm, D_MODEL), lambda i, j: (i, 0)),
            pl.BlockSpec((HALO, D_MODEL), lambda i, j: (jnp.maximum(i * halo_blocks_per_tile - 1, 0), 0)),
            pl.BlockSpec((1, D_MODEL), lambda i, j: (0, 0)),
            pl.BlockSpec((D_MODEL, tn), lambda i, j: (0, j)),
            pl.BlockSpec((D_MODEL, LANES), lambda i, j: (0, 0)),
            pl.BlockSpec((1, LANES), lambda i, j: (0, 0)),
            pl.BlockSpec((LSTM_CONV, tn), lambda i, j: (0, jnp.minimum(j, n_conv_blocks - 1))),
            pl.BlockSpec((1, tn), lambda i, j: (0, jnp.minimum(j, n_conv_blocks - 1))),
        ],
        out_specs=[
            pl.BlockSpec((tm, tn), lambda i, j: (i, j)),
            pl.BlockSpec((tm, LANES), lambda i, j: (i, 0)),
        ],
        out_shape=[
            jax.ShapeDtypeStruct((m, n), F32),
            jax.ShapeDtypeStruct((m, LANES), F32),
        ],
        scratch_shapes=[
            pltpu.VMEM((tm + HALO, D_MODEL), BF16),
            pltpu.VMEM((tm + HALO, tn), F32),
        ],
        compiler_params=_params(("parallel", "arbitrary")),
        name="lstm_in",
    )(x, x, g, w_main, w_gate, b_gate, conv_w, conv_b)


def _log_sigmoid(x):
    return -(jnp.maximum(-x, 0.0) + jnp.log1p(jnp.exp(-jnp.abs(x))))


def _cumsum_rows(x):
    n = x.shape[0]
    row = lax.broadcasted_iota(jnp.int32, x.shape, 0)
    shift = 1
    while shift < n:
        x = x + jnp.where(row >= shift, pltpu.roll(x, shift, axis=0), 0.0)
        shift *= 2
    return x


def _mlstm_kernel(q_ref, k_ref, v_ref, og_ref, gates_ref, hg_ref, o_ref, c_ref, n_ref, m_ref):
    chunk = q_ref.shape[0]
    dk, dv, heads = LSTM_QK_DIM, LSTM_V_DIM, LSTM_HEADS

    @pl.when(pl.program_id(1) == 0)
    def _():
        c_ref[...] = jnp.zeros_like(c_ref)
        n_ref[...] = jnp.zeros_like(n_ref)
        m_ref[...] = jnp.zeros_like(m_ref)

    gates = gates_ref[...]
    cum_f = _cumsum_rows(_log_sigmoid(gates))
    gates_t = gates.T
    cum_f_t = cum_f.T
    row = lax.broadcasted_iota(jnp.int32, (chunk, chunk), 0)
    col = lax.broadcasted_iota(jnp.int32, (chunk, chunk), 1)
    causal = col <= row

    for hd in range(heads):
        qs = slice(hd * dk, (hd + 1) * dk)
        vs = slice(hd * dv, (hd + 1) * dv)
        b_col = cum_f[:, heads + hd:heads + hd + 1]
        b_row = cum_f_t[heads + hd:heads + hd + 1, :]
        li_col = gates[:, hd:hd + 1]
        li_row = gates_t[hd:hd + 1, :]
        m_prev = m_ref[hd]
        c_prev = c_ref[hd]
        n_prev = n_ref[hd]
        q = q_ref[:, qs]
        k = k_ref[:, qs]
        v = v_ref[:, vs]
        qb = q.astype(BF16)
        kb = k.astype(BF16)

        dmat = jnp.where(causal, b_col - b_row + li_row, -jnp.inf)
        g = b_col + m_prev
        m_t = jnp.maximum(g, jnp.max(dmat, axis=-1, keepdims=True))
        p = jnp.exp(dmat - m_t)
        inter = jnp.exp(g - m_t)
        w = p * _dot_nt(qb, kb)
        num = inter * _dot(qb, c_prev.astype(BF16)) + _dot(w.astype(BF16), v.astype(BF16))
        den = inter * jnp.sum(q * n_prev, axis=-1, keepdims=True) + jnp.sum(w, axis=-1, keepdims=True)
        h = num / jnp.maximum(jnp.abs(den), jnp.exp(-m_t))

        b_last = b_col[chunk - 1:chunk, :]
        a = b_last - b_col + li_col
        m_new = jnp.maximum(b_last + m_prev, jnp.max(a, axis=0, keepdims=True))
        decay = jnp.exp(b_last + m_prev - m_new)
        wts = jnp.exp(a - m_new)
        c_ref[hd] = decay * c_prev + _dot_tn(kb, (wts * v).astype(BF16))
        n_ref[hd] = decay * n_prev + jnp.sum(wts * k, axis=0, keepdims=True)
        m_ref[hd] = m_new

        hs = _rms(h, hg_ref[:, vs]) * jax.nn.sigmoid(og_ref[:, vs])
        o_ref[:, vs] = hs.astype(o_ref.dtype)


def _mlstm_call(z, gates, head_gain, batch, seq):
    chunk = LSTM_CHUNK
    z = z.reshape(batch, seq, LSTM_MAIN_WIDTH)
    gates = gates.reshape(batch, seq, LANES)
    qkw = LSTM_QK_WIDTH
    return pl.pallas_call(
        _mlstm_kernel,
        grid=(batch, seq // chunk),
        in_specs=[
            pl.BlockSpec((None, chunk, qkw), lambda b, c: (b, c, 0)),
            pl.BlockSpec((None, chunk, qkw), lambda b, c: (b, c, 1)),
            pl.BlockSpec((None, chunk, D_MODEL), lambda b, c: (b, c, 2 * qkw // D_MODEL)),
            pl.BlockSpec((None, chunk, D_MODEL), lambda b, c: (b, c, 2 * qkw // D_MODEL + 1)),
            pl.BlockSpec((None, chunk, LANES), lambda b, c: (b, c, 0)),
            pl.BlockSpec((1, D_MODEL), lambda b, c: (0, 0)),
        ],
        out_specs=pl.BlockSpec((None, chunk, D_MODEL), lambda b, c: (b, c, 0)),
        out_shape=jax.ShapeDtypeStruct((batch, seq, D_MODEL), BF16),
        scratch_shapes=[
            pltpu.VMEM((LSTM_HEADS, LSTM_QK_DIM, LSTM_V_DIM), F32),
            pltpu.VMEM((LSTM_HEADS, 1, LSTM_QK_DIM), F32),
            pltpu.VMEM((LSTM_HEADS, 1, 1), F32),
        ],
        compiler_params=_params(("parallel", "arbitrary")),
        name="mlstm",
    )(z, z, z, z, gates, head_gain)


def kernel(x, attn_norm, attn_w_qkv, attn_q_gain, attn_k_gain, attn_w_o, lstm_norm, lstm_w_in, lstm_gate_bias,
           lstm_conv_w, lstm_conv_b, lstm_head_gain, lstm_w_out, ffn_norm, ffn_w_up, ffn_conv_w, ffn_conv_b,
           ffn_w_down):
    batch, seq, d = x.shape
    assert d == D_MODEL and ffn_norm.shape[0] == 2 and attn_norm.shape[0] == 1 and lstm_norm.shape[0] == 1
    m = batch * seq

    w_up = ffn_w_up.astype(BF16)
    w_down = ffn_w_down.astype(BF16)

    def ffn(xin, layer):
        return _ffn_call(xin, ffn_norm[:, None, :], w_up, ffn_conv_w, ffn_conv_b[:, None, :], w_down, layer, seq)

    qkv = _qkv_call(x, attn_norm[0][None], attn_w_qkv[0].astype(BF16), attn_q_gain[0][None], attn_k_gain[0][None])
    att = _attn_call(qkv, batch, seq)
    xf = _proj_from_residue_major_call(att, attn_w_o[0].astype(BF16), x)
    xf = ffn(xf, 0)

    n_gates = 2 * LSTM_HEADS
    w_in = lstm_w_in[0]
    w_gate = jnp.pad(w_in[:, LSTM_MAIN_WIDTH:], ((0, 0), (0, LANES - n_gates))).astype(BF16)
    b_gate = jnp.pad(lstm_gate_bias[0], (0, LANES - n_gates))[None]
    z, gates = _lstm_in_call(xf, lstm_norm[0][None], w_in.astype(BF16), w_gate, b_gate,
                             lstm_conv_w[0], lstm_conv_b[0][None], seq)
    hs = _mlstm_call(z, gates, lstm_head_gain[0][None], batch, seq)
    xf = _proj_call(hs.reshape(m, d), lstm_w_out[0].astype(BF16), xf)
    xf = ffn(xf, 1)
    return xf.reshape(batch, seq, d)
```

```python
import functools

import jax
import jax.numpy as jnp
from jax import lax
from jax.experimental import pallas as pl
from jax.experimental.pallas import tpu as pltpu

F32 = jnp.float32
BF16 = jnp.bfloat16

D_MODEL = 2048
ATT_HEADS = 16
ATT_HEAD_DIM = D_MODEL // ATT_HEADS
DILATED_PATTERNS = ((128, 1), (512, 4), (2048, 16))
ATT_BLOCK = 128
ATT_GROUP = 8
LSTM_HEADS = 4
LSTM_V_DIM = D_MODEL // LSTM_HEADS
LSTM_QK_DIM = LSTM_V_DIM // 2
LSTM_QK_WIDTH = LSTM_HEADS * LSTM_QK_DIM
LSTM_MAIN_WIDTH = 2 * LSTM_QK_WIDTH + 2 * D_MODEL
LSTM_CONV = 4
FFN_DIM = ((8 * D_MODEL // 3 + 255) // 256) * 256
FFN_CONV = 3
NORM_EPS = 1e-6

LANES = 128
SUBLANES = 8
BF16_ROWS_PER_VREG = 2 * SUBLANES
HALO = BF16_ROWS_PER_VREG
VMEM_LIMIT = 56 * 1024 * 1024

LSTM_CHUNK = 256

NORM_ROWS = 128


def _rms(x, g):
    ms = jnp.mean(x * x, axis=-1, keepdims=True)
    return x * lax.rsqrt(ms + NORM_EPS) * g


def _norm_rows(h_ref, lead, x_ref, g, copy_ref=None):
    def body(r, carry):
        start = pl.multiple_of(r * NORM_ROWS, NORM_ROWS)
        x = x_ref[pl.ds(start, NORM_ROWS), :]
        h_ref[pl.ds(pl.multiple_of(start + lead, BF16_ROWS_PER_VREG), NORM_ROWS), :] = _rms(x, g).astype(BF16)
        if copy_ref is not None:
            copy_ref[pl.ds(start, NORM_ROWS), :] = x
        return carry

    lax.fori_loop(0, x_ref.shape[0] // NORM_ROWS, body, 0)


def _dot(a, b):
    return jnp.dot(a, b, preferred_element_type=F32)


def _dot_nt(a, b):
    return lax.dot_general(a, b, (((1,), (1,)), ((), ())), preferred_element_type=F32)


def _dot_tn(a, b):
    return lax.dot_general(a, b, (((0,), (0,)), ((), ())), preferred_element_type=F32)


def _params(sem):
    return pltpu.CompilerParams(dimension_semantics=sem, vmem_limit_bytes=VMEM_LIMIT)


def _row_chunks(h_ref, w_refs, rows, lead, with_halo, consume):
    n_chunks = (h_ref.shape[0] - lead) // rows
    assert not with_halo or lead == HALO

    def project(c, tails):
        if with_halo and c == 0:
            h = h_ref[0:rows + lead, :]
            return [_dot(h, w[...]) for w in w_refs]
        h = h_ref[c * rows + lead:(c + 1) * rows + lead, :]
        us = [_dot(h, w[...]) for w in w_refs]
        if with_halo:
            us = [jnp.concatenate([t, u], axis=0) for t, u in zip(tails, us)]
        return us

    us = project(0, None)
    for c in range(n_chunks):
        nxt = project(c + 1, [u[rows:, :] for u in us]) if c + 1 < n_chunks else None
        consume(c, us)
        us = nxt


def _row_chunks_conv(h_ref, w_refs, u_refs, rows, consume):
    n_chunks = (h_ref.shape[0] - HALO) // rows

    def project(c):
        lo = 0 if c == 0 else c * rows + HALO
        hi = (c + 1) * rows + HALO
        h = h_ref[lo:hi, :]
        for w, u in zip(w_refs, u_refs):
            u[lo:hi, :] = _dot(h, w[...])

    project(0)
    for c in range(n_chunks):
        if c + 1 < n_chunks:
            project(c + 1)
        consume(c)


def _qkv_kernel(x_ref, g_ref, w_ref, qg_ref, kg_ref, o_ref, h_ref, *, rows, n_head_blocks, heads_per_block):
    j = pl.program_id(1)

    @pl.when(j == 0)
    def _():
        _norm_rows(h_ref, 0, x_ref, g_ref[...])

    @pl.when(j < 2 * n_head_blocks)
    def _():
        gain = jnp.where(j < n_head_blocks, qg_ref[...] * (ATT_HEAD_DIM ** -0.5), kg_ref[...])

        def qk_norm(c, us):
            for hh in range(heads_per_block):
                sl = slice(hh * ATT_HEAD_DIM, (hh + 1) * ATT_HEAD_DIM)
                o_ref[c * rows:(c + 1) * rows, sl] = _rms(us[0][:, sl], gain)

        _row_chunks(h_ref, [w_ref], rows, 0, False, qk_norm)

    @pl.when(j >= 2 * n_head_blocks)
    def _():
        def copy(c, us):
            o_ref[c * rows:(c + 1) * rows, :] = us[0]

        _row_chunks(h_ref, [w_ref], rows, 0, False, copy)


def _qkv_call(x, g, w, qg, kg, *, tm=1024, tn=512, rows=256):
    m = x.shape[0]
    n = w.shape[1]
    heads_per_block = tn // ATT_HEAD_DIM
    n_head_blocks = D_MODEL // tn
    return pl.pallas_call(
        functools.partial(_qkv_kernel, rows=rows, n_head_blocks=n_head_blocks, heads_per_block=heads_per_block),
        grid=(m // tm, n // tn),
        in_specs=[
            pl.BlockSpec((tm, D_MODEL), lambda i, j: (i, 0)),
            pl.BlockSpec((1, D_MODEL), lambda i, j: (0, 0)),
            pl.BlockSpec((D_MODEL, tn), lambda i, j: (0, j)),
            pl.BlockSpec((1, ATT_HEAD_DIM), lambda i, j: (0, 0)),
            pl.BlockSpec((1, ATT_HEAD_DIM), lambda i, j: (0, 0)),
        ],
        out_specs=pl.BlockSpec((tm, tn), lambda i, j: (i, j)),
        out_shape=jax.ShapeDtypeStruct((m, n), F32),
        scratch_shapes=[pltpu.VMEM((tm, D_MODEL), BF16)],
        compiler_params=_params(("parallel", "arbitrary")),
        name="attn_qkv",
    )(x, g, w, qg, kg)


ROW_INTERLEAVE = 4


def _attn_kernel(q_ref, k_ref, v_ref, o_ref, z_ref, acc_ref, den_ref, mx_ref, bias_ref, nat_ref, *, seq):
    blk, hd, grp, il = ATT_BLOCK, ATT_HEAD_DIM, ATT_GROUP, ROW_INTERLEAVE
    part = seq // il
    n_chunks = seq // blk
    assert part % blk == 0

    def chunk(c):
        return pl.ds(pl.multiple_of(c * blk, blk), blk)

    def natural_rows(c):
        lo = c // (part // blk)
        b0 = (c % (part // blk)) * blk
        return pl.ds(lo + il * b0, blk, stride=il)

    for ti, src_ref in enumerate((q_ref, k_ref, v_ref)):
        def interleave(c, carry, ti=ti, src_ref=src_ref):
            z_ref[ti, chunk(c), :] = src_ref[natural_rows(c), :]
            return carry

        lax.fori_loop(0, n_chunks, interleave, 0, unroll=2)

    qz_ref, kz_ref, vz_ref = z_ref.at[0], z_ref.at[1], z_ref.at[2]
    row = lax.broadcasted_iota(jnp.int32, (blk, 2 * blk), 0)
    col = lax.broadcasted_iota(jnp.int32, (blk, 2 * blk), 1)
    ones = jnp.ones((grp, 2 * blk, hd), BF16)

    for gi, (window, dil) in enumerate(DILATED_PATTERNS):
        assert window // dil == blk and (dil % il == 0 or il % dil == 0)
        pieces = max(il // dil, 1)
        plen = blk // pieces
        stride = max(dil // il, 1)
        n_blocks = seq // (dil * blk)
        assert n_blocks & (n_blocks - 1) == 0 and (dil * n_blocks) % grp == 0 and plen % SUBLANES == 0
        log_nb = n_blocks.bit_length() - 1
        log_plen = plen.bit_length() - 1

        def step_in_block(i, pieces=pieces, plen=plen, log_plen=log_plen):
            return pieces * jnp.bitwise_and(i, plen - 1) + lax.shift_right_logical(i, log_plen)

        dist = step_in_block(row) - step_in_block(jnp.bitwise_and(col, blk - 1)) + jnp.where(col < blk, blk, 0)
        band = jnp.logical_and(dist >= 0, dist <= blk)
        bias_ref[2 * gi + 1] = jnp.where(band, 0.0, -jnp.inf)
        bias_ref[2 * gi] = jnp.where(jnp.logical_and(band, col >= blk), 0.0, -jnp.inf)

        def block_rows(a, n, dil=dil, pieces=pieces, plen=plen, stride=stride):
            if pieces > 1:
                return [pl.ds(pl.multiple_of((p * dil + a) * part + n * plen, SUBLANES), plen) for p in range(pieces)]
            lo = jnp.bitwise_and(a, il - 1)
            hi = lax.shift_right_logical(a, il.bit_length() - 1)
            start = lo * part + hi + stride * blk * n
            if stride == 1:
                return [pl.ds(pl.multiple_of(start, blk), blk)]
            return [pl.ds(start, blk, stride=stride)]

        def load(ref, parts):
            return jnp.concatenate([ref[d, :] for d in parts], axis=0)

        def store(ref, gi, parts, val, plen=plen):
            for p, d in enumerate(parts):
                ref[gi, d, :] = val[p * plen:(p + 1) * plen, :]

        def group(it, carry, gi=gi, n_blocks=n_blocks, log_nb=log_nb, block_rows=block_rows, load=load, store=store):
            cur, qs, ks, vs, bias = [], [], [], [], []
            for gg in range(grp):
                idx = it * grp + gg
                a = lax.shift_right_logical(idx, log_nb)
                n = jnp.bitwise_and(idx, n_blocks - 1)
                c = block_rows(a, n)
                p = block_rows(a, jnp.maximum(n - 1, 0))
                cur.append(c)
                qs.append(load(qz_ref, c))
                ks.append(jnp.concatenate([load(kz_ref, p), load(kz_ref, c)], axis=0))
                vs.append(jnp.concatenate([load(vz_ref, p), load(vz_ref, c)], axis=0))
                bias.append(bias_ref[2 * gi + jnp.minimum(n, 1)])
            q = jnp.stack(qs).astype(BF16)
            k = jnp.stack(ks).astype(BF16)
            v = jnp.concatenate([jnp.stack(vs).astype(BF16), ones], axis=-1)
            s = jnp.einsum("gqd,gkd->gqk", q, k, preferred_element_type=F32) + jnp.stack(bias)
            mx = jnp.max(s, axis=-1, keepdims=True)
            p = jnp.exp(s - mx).astype(BF16)
            pv = jnp.einsum("gqk,gkd->gqd", p, v, preferred_element_type=F32)
            for gg in range(grp):
                store(acc_ref, gi, cur[gg], pv[gg, :, :hd])
                store(den_ref, gi, cur[gg], pv[gg, :, hd:])
                store(mx_ref, gi, cur[gg], jnp.broadcast_to(mx[gg], (blk, hd)))
            return carry

        lax.fori_loop(0, dil * n_blocks // grp, group, 0)

    def combine(c, carry):
        sl = chunk(c)
        m0 = mx_ref[0, sl, :]
        m1 = mx_ref[1, sl, :]
        m2 = mx_ref[2, sl, :]
        mx = jnp.maximum(jnp.maximum(m0, m1), m2)
        w0 = jnp.exp(m0 - mx)
        w1 = jnp.exp(m1 - mx)
        w2 = jnp.exp(m2 - mx)
        num = w0 * acc_ref[0, sl, :] + w1 * acc_ref[1, sl, :] + w2 * acc_ref[2, sl, :]
        den = w0 * den_ref[0, sl, :] + w1 * den_ref[1, sl, :] + w2 * den_ref[2, sl, :]
        nat_ref[natural_rows(c), :] = num / den
        return carry

    lax.fori_loop(0, n_chunks, combine, 0)

    def emit(c, carry):
        o_ref[chunk(c), :] = nat_ref[chunk(c), :].astype(o_ref.dtype)
        return carry

    lax.fori_loop(0, n_chunks, emit, 0, unroll=2)


def _attn_call(qkv, batch, seq):
    qkv = qkv.reshape(batch, seq, 3 * D_MODEL)
    hd = ATT_HEAD_DIM
    n_groups = len(DILATED_PATTERNS)
    out = pl.pallas_call(
        functools.partial(_attn_kernel, seq=seq),
        grid=(batch, ATT_HEADS),
        in_specs=[
            pl.BlockSpec((None, seq, hd), lambda b, h: (b, 0, h)),
            pl.BlockSpec((None, seq, hd), lambda b, h: (b, 0, ATT_HEADS + h)),
            pl.BlockSpec((None, seq, hd), lambda b, h: (b, 0, 2 * ATT_HEADS + h)),
        ],
        out_specs=pl.BlockSpec((None, seq, hd), lambda b, h: (b, 0, h)),
        out_shape=jax.ShapeDtypeStruct((batch, seq, D_MODEL), BF16),
        scratch_shapes=[
            pltpu.VMEM((3, seq, hd), F32),
            pltpu.VMEM((n_groups, seq, hd), F32),
            pltpu.VMEM((n_groups, seq, hd), F32),
            pltpu.VMEM((n_groups, seq, hd), F32),
            pltpu.VMEM((2 * n_groups, ATT_BLOCK, 2 * ATT_BLOCK), F32),
            pltpu.VMEM((seq, hd), F32),
        ],
        compiler_params=_params(("parallel", "parallel")),
        name="dilated_attn",
    )(qkv, qkv, qkv)
    return out.reshape(batch * seq, D_MODEL)


def _proj_kernel(a_ref, w_ref, x_ref, o_ref, *, rows):
    for c in range(o_ref.shape[0] // rows):
        sl = slice(c * rows, (c + 1) * rows)
        o_ref[sl, :] = x_ref[sl, :] + _dot(a_ref[sl, :], w_ref[...])


def _proj_call(a, w, x, *, tm=512, rows=256):
    m, k = a.shape
    n = w.shape[1]
    return pl.pallas_call(
        functools.partial(_proj_kernel, rows=rows),
        grid=(m // tm,),
        in_specs=[
            pl.BlockSpec((tm, k), lambda i: (i, 0)),
            pl.BlockSpec((k, n), lambda i: (0, 0)),
            pl.BlockSpec((tm, n), lambda i: (i, 0)),
        ],
        out_specs=pl.BlockSpec((tm, n), lambda i: (i, 0)),
        out_shape=jax.ShapeDtypeStruct((m, n), F32),
        compiler_params=_params(("parallel",)),
        name="proj_residual",
    )(a, w, x)


def _fill_normed(h_ref, x_ref, xh_ref, g_ref, seq_start, copy_ref=None):
    g = g_ref[...]
    halo = jnp.where(seq_start, 0.0, _rms(xh_ref[...], g))
    h_ref[0:HALO, :] = halo.astype(BF16)
    _norm_rows(h_ref, HALO, x_ref, g, copy_ref)


def _causal_conv(u_ref, c, w_ref, b_ref, taps, rows):
    w = w_ref[...]
    y = b_ref[...]
    for j in range(taps):
        off = c * rows + HALO - (taps - 1) + j
        y = y + w[j:j + 1, :] * u_ref[off:off + rows, :]
    return y


def _ffn_kernel(x_ref, xh_ref, g_ref, wg_ref, wu_ref, cwg_ref, cwu_ref, cbg_ref, cbu_ref, wd_ref,
                o_ref, h_ref, ug_ref, uu_ref, *, rows, tiles_per_seq):
    i = pl.program_id(0)
    j = pl.program_id(1)

    @pl.when(j == 0)
    def _():
        _fill_normed(h_ref, x_ref, xh_ref, g_ref, i % tiles_per_seq == 0, copy_ref=o_ref)

    def down_proj(c):
        gate = _causal_conv(ug_ref, c, cwg_ref, cbg_ref, FFN_CONV, rows)
        up = _causal_conv(uu_ref, c, cwu_ref, cbu_ref, FFN_CONV, rows)
        act = (gate * up / (1.0 + jnp.exp(-gate))).astype(BF16)
        o_ref[c * rows:(c + 1) * rows, :] += _dot(act, wd_ref[...])

    _row_chunks_conv(h_ref, [wg_ref, wu_ref], [ug_ref, uu_ref], rows, down_proj)


def _ffn_call(x, g, w_up, conv_w, conv_b, w_down, layer, seq, *, tm=1024, tf=512, rows=256):
    m = x.shape[0]
    nf = FFN_DIM // tf
    halo_blocks_per_tile = tm // HALO
    return pl.pallas_call(
        functools.partial(_ffn_kernel, rows=rows, tiles_per_seq=seq // tm),
        grid=(m // tm, nf),
        in_specs=[
            pl.BlockSpec((tm, D_MODEL), lambda i, j: (i, 0)),
            pl.BlockSpec((HALO, D_MODEL), lambda i, j: (jnp.maximum(i * halo_blocks_per_tile - 1, 0), 0)),
            pl.BlockSpec((None, 1, D_MODEL), lambda i, j: (layer, 0, 0)),
            pl.BlockSpec((None, D_MODEL, tf), lambda i, j: (layer, 0, j)),
            pl.BlockSpec((None, D_MODEL, tf), lambda i, j: (layer, 0, nf + j)),
            pl.BlockSpec((None, FFN_CONV, tf), lambda i, j: (layer, 0, j)),
            pl.BlockSpec((None, FFN_CONV, tf), lambda i, j: (layer, 0, nf + j)),
            pl.BlockSpec((None, 1, tf), lambda i, j: (layer, 0, j)),
            pl.BlockSpec((None, 1, tf), lambda i, j: (layer, 0, nf + j)),
            pl.BlockSpec((None, tf, D_MODEL), lambda i, j: (layer, j, 0)),
        ],
        out_specs=pl.BlockSpec((tm, D_MODEL), lambda i, j: (i, 0)),
        out_shape=jax.ShapeDtypeStruct((m, D_MODEL), F32),
        scratch_shapes=[
            pltpu.VMEM((tm + HALO, D_MODEL), BF16),
            pltpu.VMEM((tm + HALO, tf), F32),
            pltpu.VMEM((tm + HALO, tf), F32),
        ],
        compiler_params=_params(("parallel", "arbitrary")),
        name="conv_ffn",
    )(x, x, g, w_up, w_up, conv_w, conv_w, conv_b, conv_b, w_down)


def _lstm_in_kernel(x_ref, xh_ref, g_ref, w_ref, wgate_ref, bgate_ref, cw_ref, cb_ref,
                    z_ref, gates_ref, h_ref, u_ref, *, rows, tiles_per_seq, n_conv_blocks):
    i = pl.program_id(0)
    j = pl.program_id(1)

    @pl.when(j == 0)
    def _():
        _fill_normed(h_ref, x_ref, xh_ref, g_ref, i % tiles_per_seq == 0)
        gates_ref[...] = _dot(h_ref[HALO:, :], wgate_ref[...]) + bgate_ref[...]

    @pl.when(j < n_conv_blocks)
    def _():
        scale = jnp.where(j < n_conv_blocks // 2, 1.0, LSTM_QK_DIM ** -0.5)

        def conv_silu(c):
            y = _causal_conv(u_ref, c, cw_ref, cb_ref, LSTM_CONV, rows)
            z_ref[c * rows:(c + 1) * rows, :] = y * scale / (1.0 + jnp.exp(-y))

        _row_chunks_conv(h_ref, [w_ref], [u_ref], rows, conv_silu)

    @pl.when(j >= n_conv_blocks)
    def _():
        def copy(c, us):
            z_ref[c * rows:(c + 1) * rows, :] = us[0]

        _row_chunks(h_ref, [w_ref], rows, HALO, False, copy)


def _lstm_in_call(x, g, w_main, w_gate, b_gate, conv_w, conv_b, seq, *, tm=1024, tn=512, rows=256):
    m = x.shape[0]
    n = LSTM_MAIN_WIDTH
    n_conv_blocks = 2 * LSTM_QK_WIDTH // tn
    halo_blocks_per_tile = tm // HALO
    return pl.pallas_call(
        functools.partial(_lstm_in_kernel, rows=rows, tiles_per_seq=seq // tm, n_conv_blocks=n_conv_blocks),
        grid=(m // tm, n // tn),
        in_specs=[
            pl.BlockSpec((tm, D_MODEL), lambda i, j: (i, 0)),
            pl.BlockSpec((HALO, D_MODEL), lambda i, j: (jnp.maximum(i * halo_blocks_per_tile - 1, 0), 0)),
            pl.BlockSpec((1, D_MODEL), lambda i, j: (0, 0)),
            pl.BlockSpec((D_MODEL, tn), lambda i, j: (0, j)),
            pl.BlockSpec((D_MODEL, LANES), lambda i, j: (0, 0)),
            pl.BlockSpec((1, LANES), lambda i, j: (0, 0)),
            pl.BlockSpec((LSTM_CONV, tn), lambda i, j: (0, jnp.minimum(j, n_conv_blocks - 1))),
            pl.BlockSpec((1, tn), lambda i, j: (0, jnp.minimum(j, n_conv_blocks - 1))),
        ],
        out_specs=[
            pl.BlockSpec((tm, tn), lambda i, j: (i, j)),
            pl.BlockSpec((tm, LANES), lambda i, j: (i, 0)),
        ],
        out_shape=[
            jax.ShapeDtypeStruct((m, n), F32),
            jax.ShapeDtypeStruct((m, LANES), F32),
        ],
        scratch_shapes=[
            pltpu.VMEM((tm + HALO, D_MODEL), BF16),
            pltpu.VMEM((tm + HALO, tn), F32),
        ],
        compiler_params=_params(("parallel", "arbitrary")),
        name="lstm_in",
    )(x, x, g, w_main, w_gate, b_gate, conv_w, conv_b)


def _log_sigmoid(x):
    return -(jnp.maximum(-x, 0.0) + jnp.log1p(jnp.exp(-jnp.abs(x))))


def _cumsum_rows(x):
    n = x.shape[0]
    row = lax.broadcasted_iota(jnp.int32, x.shape, 0)
    shift = 1
    while shift < n:
        x = x + jnp.where(row >= shift, pltpu.roll(x, shift, axis=0), 0.0)
        shift *= 2
    return x


def _mlstm_kernel(q_ref, k_ref, v_ref, og_ref, gates_ref, hg_ref, o_ref, c_ref, n_ref, m_ref):
    chunk = q_ref.shape[0]
    dk, dv, heads = LSTM_QK_DIM, LSTM_V_DIM, LSTM_HEADS

    @pl.when(pl.program_id(1) == 0)
    def _():
        c_ref[...] = jnp.zeros_like(c_ref)
        n_ref[...] = jnp.zeros_like(n_ref)
        m_ref[...] = jnp.zeros_like(m_ref)

    gates = gates_ref[...]
    cum_f = _cumsum_rows(_log_sigmoid(gates))
    gates_t = gates.T
    cum_f_t = cum_f.T
    row = lax.broadcasted_iota(jnp.int32, (chunk, chunk), 0)
    col = lax.broadcasted_iota(jnp.int32, (chunk, chunk), 1)
    causal = col <= row

    for hd in range(heads):
        qs = slice(hd * dk, (hd + 1) * dk)
        vs = slice(hd * dv, (hd + 1) * dv)
        b_col = cum_f[:, heads + hd:heads + hd + 1]
        b_row = cum_f_t[heads + hd:heads + hd + 1, :]
        li_col = gates[:, hd:hd + 1]
        li_row = gates_t[hd:hd + 1, :]
        m_prev = m_ref[hd]
        c_prev = c_ref[hd]
        n_prev = n_ref[hd]
        q = q_ref[:, qs]
        k = k_ref[:, qs]
        v = v_ref[:, vs]
        qb = q.astype(BF16)
        kb = k.astype(BF16)

        dmat = jnp.where(causal, b_col - b_row + li_row, -jnp.inf)
        g = b_col + m_prev
        m_t = jnp.maximum(g, jnp.max(dmat, axis=-1, keepdims=True))
        p = jnp.exp(dmat - m_t)
        inter = jnp.exp(g - m_t)
        w = p * _dot_nt(qb, kb)
        num = inter * _dot(qb, c_prev.astype(BF16)) + _dot(w.astype(BF16), v.astype(BF16))
        den = inter * jnp.sum(q * n_prev, axis=-1, keepdims=True) + jnp.sum(w, axis=-1, keepdims=True)
        h = num / jnp.maximum(jnp.abs(den), jnp.exp(-m_t))

        b_last = b_col[chunk - 1:chunk, :]
        a = b_last - b_col + li_col
        m_new = jnp.maximum(b_last + m_prev, jnp.max(a, axis=0, keepdims=True))
        decay = jnp.exp(b_last + m_prev - m_new)
        wts = jnp.exp(a - m_new)
        c_ref[hd] = decay * c_prev + _dot_tn(kb, (wts * v).astype(BF16))
        n_ref[hd] = decay * n_prev + jnp.sum(wts * k, axis=0, keepdims=True)
        m_ref[hd] = m_new

        hs = _rms(h, hg_ref[:, vs]) * jax.nn.sigmoid(og_ref[:, vs])
        o_ref[:, vs] = hs.astype(o_ref.dtype)


def _mlstm_call(z, gates, head_gain, batch, seq):
    chunk = LSTM_CHUNK
    z = z.reshape(batch, seq, LSTM_MAIN_WIDTH)
    gates = gates.reshape(batch, seq, LANES)
    qkw = LSTM_QK_WIDTH
    return pl.pallas_call(
        _mlstm_kernel,
        grid=(batch, seq // chunk),
        in_specs=[
            pl.BlockSpec((None, chunk, qkw), lambda b, c: (b, c, 0)),
            pl.BlockSpec((None, chunk, qkw), lambda b, c: (b, c, 1)),
            pl.BlockSpec((None, chunk, D_MODEL), lambda b, c: (b, c, 2 * qkw // D_MODEL)),
            pl.BlockSpec((None, chunk, D_MODEL), lambda b, c: (b, c, 2 * qkw // D_MODEL + 1)),
            pl.BlockSpec((None, chunk, LANES), lambda b, c: (b, c, 0)),
            pl.BlockSpec((1, D_MODEL), lambda b, c: (0, 0)),
        ],
        out_specs=pl.BlockSpec((None, chunk, D_MODEL), lambda b, c: (b, c, 0)),
        out_shape=jax.ShapeDtypeStruct((batch, seq, D_MODEL), BF16),
        scratch_shapes=[
            pltpu.VMEM((LSTM_HEADS, LSTM_QK_DIM, LSTM_V_DIM), F32),
            pltpu.VMEM((LSTM_HEADS, 1, LSTM_QK_DIM), F32),
            pltpu.VMEM((LSTM_HEADS, 1, 1), F32),
        ],
        compiler_params=_params(("parallel", "arbitrary")),
        name="mlstm",
    )(z, z, z, z, gates, head_gain)


def kernel(x, attn_norm, attn_w_qkv, attn_q_gain, attn_k_gain, attn_w_o, lstm_norm, lstm_w_in, lstm_gate_bias,
           lstm_conv_w, lstm_conv_b, lstm_head_gain, lstm_w_out, ffn_norm, ffn_w_up, ffn_conv_w, ffn_conv_b,
           ffn_w_down):
    batch, seq, d = x.shape
    assert d == D_MODEL and ffn_norm.shape[0] == 2 and attn_norm.shape[0] == 1 and lstm_norm.shape[0] == 1
    m = batch * seq
    xf = x.reshape(m, d)

    w_up = ffn_w_up.astype(BF16)
    w_down = ffn_w_down.astype(BF16)

    def ffn(xin, layer):
        return _ffn_call(xin, ffn_norm[:, None, :], w_up, ffn_conv_w, ffn_conv_b[:, None, :], w_down, layer, seq)

    qkv = _qkv_call(xf, attn_norm[0][None], attn_w_qkv[0].astype(BF16), attn_q_gain[0][None], attn_k_gain[0][None])
    att = _attn_call(qkv, batch, seq)
    xf = _proj_call(att, attn_w_o[0].astype(BF16), xf)
    xf = ffn(xf, 0)

    n_gates = 2 * LSTM_HEADS
    w_in = lstm_w_in[0]
    w_gate = jnp.pad(w_in[:, LSTM_MAIN_WIDTH:], ((0, 0), (0, LANES - n_gates))).astype(BF16)
    b_gate = jnp.pad(lstm_gate_bias[0], (0, LANES - n_gates))[None]
    z, gates = _lstm_in_call(xf, lstm_norm[0][None], w_in.astype(BF16), w_gate, b_gate,
                             lstm_conv_w[0], lstm_conv_b[0][None], seq)
    hs = _mlstm_call(z, gates, lstm_head_gain[0][None], batch, seq)
    xf = _proj_call(hs.reshape(m, d), lstm_w_out[0].astype(BF16), xf)
    xf = ffn(xf, 1)
    return xf.reshape(batch, seq, d)
```

```python
import functools
from typing import NamedTuple, Optional

import jax
import jax.numpy as jnp
from jax import lax
from jax.experimental import pallas as pl
from jax.experimental.pallas import tpu as pltpu

F32 = jnp.float32
BF16 = jnp.bfloat16

D_MODEL = 2048
ATT_HEADS = 16
ATT_HEAD_DIM = D_MODEL // ATT_HEADS
DILATED_PATTERNS = ((128, 1), (512, 4), (2048, 16))
ATT_BLOCK = 128
ATT_GROUP = 8
LSTM_HEADS = 4
LSTM_V_DIM = D_MODEL // LSTM_HEADS
LSTM_QK_DIM = LSTM_V_DIM // 2
LSTM_QK_WIDTH = LSTM_HEADS * LSTM_QK_DIM
LSTM_MAIN_WIDTH = 2 * LSTM_QK_WIDTH + 2 * D_MODEL
LSTM_CONV = 4
FFN_DIM = ((8 * D_MODEL // 3 + 255) // 256) * 256
FFN_CONV = 3
NORM_EPS = 1e-6

LANES = 128
SUBLANES = 8
BF16_ROWS_PER_VREG = 2 * SUBLANES
HALO = BF16_ROWS_PER_VREG
VMEM_LIMIT = 56 * 1024 * 1024

LSTM_CHUNK = 256

NORM_ROWS = 128


def _rms(x, g):
    ms = jnp.mean(x * x, axis=-1, keepdims=True)
    return x * lax.rsqrt(ms + NORM_EPS) * g


def _norm_rows(h_ref, lead, x_ref, g, copy_ref=None):
    def body(r, carry):
        start = pl.multiple_of(r * NORM_ROWS, NORM_ROWS)
        x = x_ref[pl.ds(start, NORM_ROWS), :]
        h_ref[pl.ds(pl.multiple_of(start + lead, BF16_ROWS_PER_VREG), NORM_ROWS), :] = _rms(x, g).astype(BF16)
        if copy_ref is not None:
            copy_ref[pl.ds(start, NORM_ROWS), :] = x
        return carry

    lax.fori_loop(0, x_ref.shape[0] // NORM_ROWS, body, 0)


def _dot(a, b):
    return jnp.dot(a, b, preferred_element_type=F32)


def _dot_nt(a, b):
    return lax.dot_general(a, b, (((1,), (1,)), ((), ())), preferred_element_type=F32)


def _dot_tn(a, b):
    return lax.dot_general(a, b, (((0,), (0,)), ((), ())), preferred_element_type=F32)


def _params(sem):
    return pltpu.CompilerParams(dimension_semantics=sem, vmem_limit_bytes=VMEM_LIMIT)


class _Cast(NamedTuple):
    src: jax.Array
    layer: Optional[int]
    rows: int


def _cast_plan(casts, step_of):
    in_specs, out_specs, out_shapes, n_blocks = [], [], [], []
    for c in casts:
        r, cols = c.src.shape[-2:]
        assert r % c.rows == 0 and c.rows % BF16_ROWS_PER_VREG == 0
        nb = r // c.rows

        def block(*ids, nb=nb):
            return jnp.minimum(step_of(*ids), nb - 1)

        if c.layer is None:
            in_specs.append(pl.BlockSpec((c.rows, cols), lambda *ids, block=block: (block(*ids), 0)))
        else:
            in_specs.append(pl.BlockSpec((None, c.rows, cols),
                                         lambda *ids, block=block, layer=c.layer: (layer, block(*ids), 0)))
        out_specs.append(pl.BlockSpec((c.rows, cols), lambda *ids, block=block: (block(*ids), 0)))
        out_shapes.append(jax.ShapeDtypeStruct((r, cols), BF16))
        n_blocks.append(nb)
    return in_specs, out_specs, out_shapes, tuple(n_blocks)


def _cast_step(step, src_refs, dst_refs, n_blocks):
    for src, dst, nb in zip(src_refs, dst_refs, n_blocks):
        @pl.when(step < nb)
        def _(src=src, dst=dst):
            for r0 in range(0, src.shape[0], BF16_ROWS_PER_VREG):
                sl = slice(r0, r0 + BF16_ROWS_PER_VREG)
                dst[sl, :] = src[sl, :].astype(BF16)


def _row_chunks(h_ref, w_refs, rows, lead, with_halo, consume):
    n_chunks = (h_ref.shape[0] - lead) // rows
    assert not with_halo or lead == HALO

    def project(c, tails):
        if with_halo and c == 0:
            h = h_ref[0:rows + lead, :]
            return [_dot(h, w[...]) for w in w_refs]
        h = h_ref[c * rows + lead:(c + 1) * rows + lead, :]
        us = [_dot(h, w[...]) for w in w_refs]
        if with_halo:
            us = [jnp.concatenate([t, u], axis=0) for t, u in zip(tails, us)]
        return us

    us = project(0, None)
    for c in range(n_chunks):
        nxt = project(c + 1, [u[rows:, :] for u in us]) if c + 1 < n_chunks else None
        consume(c, us)
        us = nxt


def _row_chunks_conv(h_ref, w_refs, u_refs, rows, consume):
    n_chunks = (h_ref.shape[0] - HALO) // rows

    def project(c):
        lo = 0 if c == 0 else c * rows + HALO
        hi = (c + 1) * rows + HALO
        h = h_ref[lo:hi, :]
        for w, u in zip(w_refs, u_refs):
            u[lo:hi, :] = _dot(h, w[...])

    project(0)
    for c in range(n_chunks):
        if c + 1 < n_chunks:
            project(c + 1)
        consume(c)


def _qkv_kernel(*refs, rows, n_head_blocks, heads_per_block, cast_blocks):
    nc = len(cast_blocks)
    x_ref, g_ref, w_ref, qg_ref, kg_ref = refs[:5]
    cast_src, o_ref, cast_dst, h_ref = refs[5:5 + nc], refs[5 + nc], refs[6 + nc:6 + 2 * nc], refs[6 + 2 * nc]
    j = pl.program_id(1)
    _cast_step(pl.program_id(0) * pl.num_programs(1) + j, cast_src, cast_dst, cast_blocks)

    @pl.when(j == 0)
    def _():
        _norm_rows(h_ref, 0, x_ref, g_ref[...])

    @pl.when(j < 2 * n_head_blocks)
    def _():
        gain = jnp.where(j < n_head_blocks, qg_ref[...] * (ATT_HEAD_DIM ** -0.5), kg_ref[...])

        def qk_norm(c, us):
            for hh in range(heads_per_block):
                sl = slice(hh * ATT_HEAD_DIM, (hh + 1) * ATT_HEAD_DIM)
                o_ref[c * rows:(c + 1) * rows, sl] = _rms(us[0][:, sl], gain)

        _row_chunks(h_ref, [w_ref], rows, 0, False, qk_norm)

    @pl.when(j >= 2 * n_head_blocks)
    def _():
        def copy(c, us):
            o_ref[c * rows:(c + 1) * rows, :] = us[0]

        _row_chunks(h_ref, [w_ref], rows, 0, False, copy)


def _qkv_call(x, g, w, qg, kg, casts, *, tm=1024, tn=512, rows=256):
    m = x.shape[0]
    n = w.shape[1]
    heads_per_block = tn // ATT_HEAD_DIM
    n_head_blocks = D_MODEL // tn
    n_col_blocks = n // tn
    cast_in, cast_out, cast_shapes, cast_blocks = _cast_plan(casts, lambda i, j: i * n_col_blocks + j)
    qkv, *cast = pl.pallas_call(
        functools.partial(_qkv_kernel, rows=rows, n_head_blocks=n_head_blocks, heads_per_block=heads_per_block,
                          cast_blocks=cast_blocks),
        grid=(m // tm, n_col_blocks),
        in_specs=[
            pl.BlockSpec((tm, D_MODEL), lambda i, j: (i, 0)),
            pl.BlockSpec((1, D_MODEL), lambda i, j: (0, 0)),
            pl.BlockSpec((D_MODEL, tn), lambda i, j: (0, j)),
            pl.BlockSpec((1, ATT_HEAD_DIM), lambda i, j: (0, 0)),
            pl.BlockSpec((1, ATT_HEAD_DIM), lambda i, j: (0, 0)),
        ] + cast_in,
        out_specs=[pl.BlockSpec((tm, tn), lambda i, j: (i, j))] + cast_out,
        out_shape=[jax.ShapeDtypeStruct((m, n), F32)] + cast_shapes,
        scratch_shapes=[pltpu.VMEM((tm, D_MODEL), BF16)],
        compiler_params=_params(("arbitrary", "arbitrary")),
        name="attn_qkv",
    )(x, g, w, qg, kg, *[c.src for c in casts])
    return qkv, cast


ROW_INTERLEAVE = 4


def _attn_kernel(*refs, seq, cast_blocks):
    nc = len(cast_blocks)
    q_ref, k_ref, v_ref = refs[:3]
    cast_src, o_ref, cast_dst = refs[3:3 + nc], refs[3 + nc], refs[4 + nc:4 + 2 * nc]
    z_ref, acc_ref, den_ref, mx_ref, bias_ref, nat_ref = refs[4 + 2 * nc:]
    _cast_step(pl.program_id(0) * pl.num_programs(1) + pl.program_id(1), cast_src, cast_dst, cast_blocks)
    _attn_body(q_ref, k_ref, v_ref, o_ref, z_ref, acc_ref, den_ref, mx_ref, bias_ref, nat_ref, seq=seq)


def _attn_body(q_ref, k_ref, v_ref, o_ref, z_ref, acc_ref, den_ref, mx_ref, bias_ref, nat_ref, *, seq):
    blk, hd, grp, il = ATT_BLOCK, ATT_HEAD_DIM, ATT_GROUP, ROW_INTERLEAVE
    part = seq // il
    n_chunks = seq // blk
    assert part % blk == 0

    def chunk(c):
        return pl.ds(pl.multiple_of(c * blk, blk), blk)

    def natural_rows(c):
        lo = c // (part // blk)
        b0 = (c % (part // blk)) * blk
        return pl.ds(lo + il * b0, blk, stride=il)

    for ti, src_ref in enumerate((q_ref, k_ref, v_ref)):
        def interleave(c, carry, ti=ti, src_ref=src_ref):
            z_ref[ti, chunk(c), :] = src_ref[natural_rows(c), :]
            return carry

        lax.fori_loop(0, n_chunks, interleave, 0, unroll=2)

    qz_ref, kz_ref, vz_ref = z_ref.at[0], z_ref.at[1], z_ref.at[2]
    row = lax.broadcasted_iota(jnp.int32, (blk, 2 * blk), 0)
    col = lax.broadcasted_iota(jnp.int32, (blk, 2 * blk), 1)
    ones = jnp.ones((grp, 2 * blk, hd), BF16)

    for gi, (window, dil) in enumerate(DILATED_PATTERNS):
        assert window // dil == blk and (dil % il == 0 or il % dil == 0)
        pieces = max(il // dil, 1)
        plen = blk // pieces
        stride = max(dil // il, 1)
        n_blocks = seq // (dil * blk)
        assert n_blocks & (n_blocks - 1) == 0 and (dil * n_blocks) % grp == 0 and plen % SUBLANES == 0
        log_nb = n_blocks.bit_length() - 1
        log_plen = plen.bit_length() - 1

        def step_in_block(i, pieces=pieces, plen=plen, log_plen=log_plen):
            return pieces * jnp.bitwise_and(i, plen - 1) + lax.shift_right_logical(i, log_plen)

        dist = step_in_block(row) - step_in_block(jnp.bitwise_and(col, blk - 1)) + jnp.where(col < blk, blk, 0)
        band = jnp.logical_and(dist >= 0, dist <= blk)
        bias_ref[2 * gi + 1] = jnp.where(band, 0.0, -jnp.inf)
        bias_ref[2 * gi] = jnp.where(jnp.logical_and(band, col >= blk), 0.0, -jnp.inf)

        def block_rows(a, n, dil=dil, pieces=pieces, plen=plen, stride=stride):
            if pieces > 1:
                return [pl.ds(pl.multiple_of((p * dil + a) * part + n * plen, SUBLANES), plen) for p in range(pieces)]
            lo = jnp.bitwise_and(a, il - 1)
            hi = lax.shift_right_logical(a, il.bit_length() - 1)
            start = lo * part + hi + stride * blk * n
            if stride == 1:
                return [pl.ds(pl.multiple_of(start, blk), blk)]
            return [pl.ds(start, blk, stride=stride)]

        def load(ref, parts):
            return jnp.concatenate([ref[d, :] for d in parts], axis=0)

        def store(ref, gi, parts, val, plen=plen):
            for p, d in enumerate(parts):
                ref[gi, d, :] = val[p * plen:(p + 1) * plen, :]

        def group(it, carry, gi=gi, n_blocks=n_blocks, log_nb=log_nb, block_rows=block_rows, load=load, store=store):
            cur, qs, ks, vs, bias = [], [], [], [], []
            for gg in range(grp):
                idx = it * grp + gg
                a = lax.shift_right_logical(idx, log_nb)
                n = jnp.bitwise_and(idx, n_blocks - 1)
                c = block_rows(a, n)
                p = block_rows(a, jnp.maximum(n - 1, 0))
                cur.append(c)
                qs.append(load(qz_ref, c))
                ks.append(jnp.concatenate([load(kz_ref, p), load(kz_ref, c)], axis=0))
                vs.append(jnp.concatenate([load(vz_ref, p), load(vz_ref, c)], axis=0))
                bias.append(bias_ref[2 * gi + jnp.minimum(n, 1)])
            q = jnp.stack(qs).astype(BF16)
            k = jnp.stack(ks).astype(BF16)
            v = jnp.concatenate([jnp.stack(vs).astype(BF16), ones], axis=-1)
            s = jnp.einsum("gqd,gkd->gqk", q, k, preferred_element_type=F32) + jnp.stack(bias)
            mx = jnp.max(s, axis=-1, keepdims=True)
            p = jnp.exp(s - mx).astype(BF16)
            pv = jnp.einsum("gqk,gkd->gqd", p, v, preferred_element_type=F32)
            for gg in range(grp):
                store(acc_ref, gi, cur[gg], pv[gg, :, :hd])
                store(den_ref, gi, cur[gg], pv[gg, :, hd:])
                store(mx_ref, gi, cur[gg], jnp.broadcast_to(mx[gg], (blk, hd)))
            return carry

        lax.fori_loop(0, dil * n_blocks // grp, group, 0)

    def combine(c, carry):
        sl = chunk(c)
        m0 = mx_ref[0, sl, :]
        m1 = mx_ref[1, sl, :]
        m2 = mx_ref[2, sl, :]
        mx = jnp.maximum(jnp.maximum(m0, m1), m2)
        w0 = jnp.exp(m0 - mx)
        w1 = jnp.exp(m1 - mx)
        w2 = jnp.exp(m2 - mx)
        num = w0 * acc_ref[0, sl, :] + w1 * acc_ref[1, sl, :] + w2 * acc_ref[2, sl, :]
        den = w0 * den_ref[0, sl, :] + w1 * den_ref[1, sl, :] + w2 * den_ref[2, sl, :]
        nat_ref[natural_rows(c), :] = num / den
        return carry

    lax.fori_loop(0, n_chunks, combine, 0)

    def emit(c, carry):
        o_ref[chunk(c), :] = nat_ref[chunk(c), :].astype(o_ref.dtype)
        return carry

    lax.fori_loop(0, n_chunks, emit, 0, unroll=2)


def _attn_call(qkv, batch, seq, casts):
    qkv = qkv.reshape(batch, seq, 3 * D_MODEL)
    hd = ATT_HEAD_DIM
    n_groups = len(DILATED_PATTERNS)
    cast_in, cast_out, cast_shapes, cast_blocks = _cast_plan(casts, lambda b, h: b * ATT_HEADS + h)
    out, *cast = pl.pallas_call(
        functools.partial(_attn_kernel, seq=seq, cast_blocks=cast_blocks),
        grid=(batch, ATT_HEADS),
        in_specs=[
            pl.BlockSpec((None, seq, hd), lambda b, h: (b, 0, h)),
            pl.BlockSpec((None, seq, hd), lambda b, h: (b, 0, ATT_HEADS + h)),
            pl.BlockSpec((None, seq, hd), lambda b, h: (b, 0, 2 * ATT_HEADS + h)),
        ] + cast_in,
        out_specs=[pl.BlockSpec((None, seq, hd), lambda b, h: (b, 0, h))] + cast_out,
        out_shape=[jax.ShapeDtypeStruct((batch, seq, D_MODEL), BF16)] + cast_shapes,
        scratch_shapes=[
            pltpu.VMEM((3, seq, hd), F32),
            pltpu.VMEM((n_groups, seq, hd), F32),
            pltpu.VMEM((n_groups, seq, hd), F32),
            pltpu.VMEM((n_groups, seq, hd), F32),
            pltpu.VMEM((2 * n_groups, ATT_BLOCK, 2 * ATT_BLOCK), F32),
            pltpu.VMEM((seq, hd), F32),
        ],
        compiler_params=_params(("arbitrary", "arbitrary")),
        name="dilated_attn",
    )(qkv, qkv, qkv, *[c.src for c in casts])
    return out.reshape(batch * seq, D_MODEL), cast


def _proj_kernel(a_ref, w_ref, x_ref, o_ref, *, rows):
    for c in range(o_ref.shape[0] // rows):
        sl = slice(c * rows, (c + 1) * rows)
        o_ref[sl, :] = x_ref[sl, :] + _dot(a_ref[sl, :], w_ref[...])


def _proj_call(a, w, x, *, tm=512, rows=256):
    m, k = a.shape
    n = w.shape[1]
    return pl.pallas_call(
        functools.partial(_proj_kernel, rows=rows),
        grid=(m // tm,),
        in_specs=[
            pl.BlockSpec((tm, k), lambda i: (i, 0)),
            pl.BlockSpec((k, n), lambda i: (0, 0)),
            pl.BlockSpec((tm, n), lambda i: (i, 0)),
        ],
        out_specs=pl.BlockSpec((tm, n), lambda i: (i, 0)),
        out_shape=jax.ShapeDtypeStruct((m, n), F32),
        compiler_params=_params(("parallel",)),
        name="proj_residual",
    )(a, w, x)


def _fill_normed(h_ref, x_ref, xh_ref, g_ref, seq_start, copy_ref=None):
    g = g_ref[...]
    halo = jnp.where(seq_start, 0.0, _rms(xh_ref[...], g))
    h_ref[0:HALO, :] = halo.astype(BF16)
    _norm_rows(h_ref, HALO, x_ref, g, copy_ref)


def _causal_conv(u_ref, c, w_ref, b_ref, taps, rows):
    w = w_ref[...]
    y = b_ref[...]
    for j in range(taps):
        off = c * rows + HALO - (taps - 1) + j
        y = y + w[j:j + 1, :] * u_ref[off:off + rows, :]
    return y


def _ffn_kernel(x_ref, xh_ref, g_ref, wg_ref, wu_ref, cwg_ref, cwu_ref, cbg_ref, cbu_ref, wd_ref,
                o_ref, h_ref, ug_ref, uu_ref, *, rows, tiles_per_seq):
    i = pl.program_id(0)
    j = pl.program_id(1)

    @pl.when(j == 0)
    def _():
        _fill_normed(h_ref, x_ref, xh_ref, g_ref, i % tiles_per_seq == 0, copy_ref=o_ref)

    def down_proj(c):
        gate = _causal_conv(ug_ref, c, cwg_ref, cbg_ref, FFN_CONV, rows)
        up = _causal_conv(uu_ref, c, cwu_ref, cbu_ref, FFN_CONV, rows)
        act = (gate * up / (1.0 + jnp.exp(-gate))).astype(BF16)
        o_ref[c * rows:(c + 1) * rows, :] += _dot(act, wd_ref[...])

    _row_chunks_conv(h_ref, [wg_ref, wu_ref], [ug_ref, uu_ref], rows, down_proj)


def _ffn_call(x, g, w_up, conv_w, conv_b, w_down, layer, seq, *, tm=1024, tf=512, rows=256):
    m = x.shape[0]
    nf = FFN_DIM // tf
    halo_blocks_per_tile = tm // HALO
    return pl.pallas_call(
        functools.partial(_ffn_kernel, rows=rows, tiles_per_seq=seq // tm),
        grid=(m // tm, nf),
        in_specs=[
            pl.BlockSpec((tm, D_MODEL), lambda i, j: (i, 0)),
            pl.BlockSpec((HALO, D_MODEL), lambda i, j: (jnp.maximum(i * halo_blocks_per_tile - 1, 0), 0)),
            pl.BlockSpec((None, 1, D_MODEL), lambda i, j: (layer, 0, 0)),
            pl.BlockSpec((D_MODEL, tf), lambda i, j: (0, j)),
            pl.BlockSpec((D_MODEL, tf), lambda i, j: (0, nf + j)),
            pl.BlockSpec((None, FFN_CONV, tf), lambda i, j: (layer, 0, j)),
            pl.BlockSpec((None, FFN_CONV, tf), lambda i, j: (layer, 0, nf + j)),
            pl.BlockSpec((None, 1, tf), lambda i, j: (layer, 0, j)),
            pl.BlockSpec((None, 1, tf), lambda i, j: (layer, 0, nf + j)),
            pl.BlockSpec((tf, D_MODEL), lambda i, j: (j, 0)),
        ],
        out_specs=pl.BlockSpec((tm, D_MODEL), lambda i, j: (i, 0)),
        out_shape=jax.ShapeDtypeStruct((m, D_MODEL), F32),
        scratch_shapes=[
            pltpu.VMEM((tm + HALO, D_MODEL), BF16),
            pltpu.VMEM((tm + HALO, tf), F32),
            pltpu.VMEM((tm + HALO, tf), F32),
        ],
        compiler_params=_params(("parallel", "arbitrary")),
        name="conv_ffn",
    )(x, x, g, w_up, w_up, conv_w, conv_w, conv_b, conv_b, w_down)


def _lstm_in_kernel(x_ref, xh_ref, g_ref, w_ref, wgate_ref, bgate_ref, cw_ref, cb_ref,
                    z_ref, gates_ref, h_ref, u_ref, *, rows, tiles_per_seq, n_conv_blocks):
    i = pl.program_id(0)
    j = pl.program_id(1)

    @pl.when(j == 0)
    def _():
        _fill_normed(h_ref, x_ref, xh_ref, g_ref, i % tiles_per_seq == 0)
        gates_ref[...] = _dot(h_ref[HALO:, :], wgate_ref[...]) + bgate_ref[...]

    @pl.when(j < n_conv_blocks)
    def _():
        scale = jnp.where(j < n_conv_blocks // 2, 1.0, LSTM_QK_DIM ** -0.5)

        def conv_silu(c):
            y = _causal_conv(u_ref, c, cw_ref, cb_ref, LSTM_CONV, rows)
            z_ref[c * rows:(c + 1) * rows, :] = y * scale / (1.0 + jnp.exp(-y))

        _row_chunks_conv(h_ref, [w_ref], [u_ref], rows, conv_silu)

    @pl.when(j >= n_conv_blocks)
    def _():
        def copy(c, us):
            z_ref[c * rows:(c + 1) * rows, :] = us[0]

        _row_chunks(h_ref, [w_ref], rows, HALO, False, copy)


def _lstm_in_call(x, g, w_main, w_gate, b_gate, conv_w, conv_b, seq, *, tm=1024, tn=512, rows=256):
    m = x.shape[0]
    n = LSTM_MAIN_WIDTH
    n_conv_blocks = 2 * LSTM_QK_WIDTH // tn
    halo_blocks_per_tile = tm // HALO
    return pl.pallas_call(
        functools.partial(_lstm_in_kernel, rows=rows, tiles_per_seq=seq // tm, n_conv_blocks=n_conv_blocks),
        grid=(m // tm, n // tn),
        in_specs=[
            pl.BlockSpec((tm, D_MODEL), lambda i, j: (i, 0)),
            pl.BlockSpec((HALO, D_MODEL), lambda i, j: (jnp.maximum(i * halo_blocks_per_tile - 1, 0), 0)),
            pl.BlockSpec((1, D_MODEL), lambda i, j: (0, 0)),
            pl.BlockSpec((D_MODEL, tn), lambda i, j: (0, j)),
            pl.BlockSpec((D_MODEL, LANES), lambda i, j: (0, 0)),
            pl.BlockSpec((1, LANES), lambda i, j: (0, 0)),
            pl.BlockSpec((LSTM_CONV, tn), lambda i, j: (0, jnp.minimum(j, n_conv_blocks - 1))),
            pl.BlockSpec((1, tn), lambda i, j: (0, jnp.minimum(j, n_conv_blocks - 1))),
        ],
        out_specs=[
            pl.BlockSpec((tm, tn), lambda i, j: (i, j)),
            pl.BlockSpec((tm, LANES), lambda i, j: (i, 0)),
        ],
        out_shape=[
            jax.ShapeDtypeStruct((m, n), F32),
            jax.ShapeDtypeStruct((m, LANES), F32),
        ],
        scratch_shapes=[
            pltpu.VMEM((tm + HALO, D_MODEL), BF16),
            pltpu.VMEM((tm + HALO, tn), F32),
        ],
        compiler_params=_params(("parallel", "arbitrary")),
        name="lstm_in",
    )(x, x, g, w_main, w_gate, b_gate, conv_w, conv_b)


def _log_sigmoid(x):
    return -(jnp.maximum(-x, 0.0) + jnp.log1p(jnp.exp(-jnp.abs(x))))


def _cumsum_rows(x):
    n = x.shape[0]
    row = lax.broadcasted_iota(jnp.int32, x.shape, 0)
    shift = 1
    while shift < n:
        x = x + jnp.where(row >= shift, pltpu.roll(x, shift, axis=0), 0.0)
        shift *= 2
    return x


def _mlstm_kernel(q_ref, k_ref, v_ref, og_ref, gates_ref, hg_ref, o_ref, c_ref, n_ref, m_ref):
    chunk = q_ref.shape[0]
    dk, dv, heads = LSTM_QK_DIM, LSTM_V_DIM, LSTM_HEADS

    @pl.when(pl.program_id(1) == 0)
    def _():
        c_ref[...] = jnp.zeros_like(c_ref)
        n_ref[...] = jnp.zeros_like(n_ref)
        m_ref[...] = jnp.zeros_like(m_ref)

    gates = gates_ref[...]
    cum_f = _cumsum_rows(_log_sigmoid(gates))
    gates_t = gates.T
    cum_f_t = cum_f.T
    row = lax.broadcasted_iota(jnp.int32, (chunk, chunk), 0)
    col = lax.broadcasted_iota(jnp.int32, (chunk, chunk), 1)
    causal = col <= row

    for hd in range(heads):
        qs = slice(hd * dk, (hd + 1) * dk)
        vs = slice(hd * dv, (hd + 1) * dv)
        b_col = cum_f[:, heads + hd:heads + hd + 1]
        b_row = cum_f_t[heads + hd:heads + hd + 1, :]
        li_col = gates[:, hd:hd + 1]
        li_row = gates_t[hd:hd + 1, :]
        m_prev = m_ref[hd]
        c_prev = c_ref[hd]
        n_prev = n_ref[hd]
        q = q_ref[:, qs]
        k = k_ref[:, qs]
        v = v_ref[:, vs]
        qb = q.astype(BF16)
        kb = k.astype(BF16)

        dmat = jnp.where(causal, b_col - b_row + li_row, -jnp.inf)
        g = b_col + m_prev
        m_t = jnp.maximum(g, jnp.max(dmat, axis=-1, keepdims=True))
        p = jnp.exp(dmat - m_t)
        inter = jnp.exp(g - m_t)
        w = p * _dot_nt(qb, kb)
        num = inter * _dot(qb, c_prev.astype(BF16)) + _dot(w.astype(BF16), v.astype(BF16))
        den = inter * jnp.sum(q * n_prev, axis=-1, keepdims=True) + jnp.sum(w, axis=-1, keepdims=True)
        h = num / jnp.maximum(jnp.abs(den), jnp.exp(-m_t))

        b_last = b_col[chunk - 1:chunk, :]
        a = b_last - b_col + li_col
        m_new = jnp.maximum(b_last + m_prev, jnp.max(a, axis=0, keepdims=True))
        decay = jnp.exp(b_last + m_prev - m_new)
        wts = jnp.exp(a - m_new)
        c_ref[hd] = decay * c_prev + _dot_tn(kb, (wts * v).astype(BF16))
        n_ref[hd] = decay * n_prev + jnp.sum(wts * k, axis=0, keepdims=True)
        m_ref[hd] = m_new

        hs = _rms(h, hg_ref[:, vs]) * jax.nn.sigmoid(og_ref[:, vs])
        o_ref[:, vs] = hs.astype(o_ref.dtype)


def _mlstm_call(z, gates, head_gain, batch, seq):
    chunk = LSTM_CHUNK
    z = z.reshape(batch, seq, LSTM_MAIN_WIDTH)
    gates = gates.reshape(batch, seq, LANES)
    qkw = LSTM_QK_WIDTH
    return pl.pallas_call(
        _mlstm_kernel,
        grid=(batch, seq // chunk),
        in_specs=[
            pl.BlockSpec((None, chunk, qkw), lambda b, c: (b, c, 0)),
            pl.BlockSpec((None, chunk, qkw), lambda b, c: (b, c, 1)),
            pl.BlockSpec((None, chunk, D_MODEL), lambda b, c: (b, c, 2 * qkw // D_MODEL)),
            pl.BlockSpec((None, chunk, D_MODEL), lambda b, c: (b, c, 2 * qkw // D_MODEL + 1)),
            pl.BlockSpec((None, chunk, LANES), lambda b, c: (b, c, 0)),
            pl.BlockSpec((1, D_MODEL), lambda b, c: (0, 0)),
        ],
        out_specs=pl.BlockSpec((None, chunk, D_MODEL), lambda b, c: (b, c, 0)),
        out_shape=jax.ShapeDtypeStruct((batch, seq, D_MODEL), BF16),
        scratch_shapes=[
            pltpu.VMEM((LSTM_HEADS, LSTM_QK_DIM, LSTM_V_DIM), F32),
            pltpu.VMEM((LSTM_HEADS, 1, LSTM_QK_DIM), F32),
            pltpu.VMEM((LSTM_HEADS, 1, 1), F32),
        ],
        compiler_params=_params(("parallel", "arbitrary")),
        name="mlstm",
    )(z, z, z, z, gates, head_gain)


def kernel(x, attn_norm, attn_w_qkv, attn_q_gain, attn_k_gain, attn_w_o, lstm_norm, lstm_w_in, lstm_gate_bias,
           lstm_conv_w, lstm_conv_b, lstm_head_gain, lstm_w_out, ffn_norm, ffn_w_up, ffn_conv_w, ffn_conv_b,
           ffn_w_down):
    batch, seq, d = x.shape
    assert d == D_MODEL and ffn_norm.shape[0] == 2 and attn_norm.shape[0] == 1 and lstm_norm.shape[0] == 1
    m = batch * seq
    xf = x.reshape(m, d)

    def ffn(xin, layer, w_up, w_down):
        return _ffn_call(xin, ffn_norm[:, None, :], w_up, ffn_conv_w, ffn_conv_b[:, None, :], w_down, layer, seq)

    qkv, (w_up0, w_down0, w_o, w_in) = _qkv_call(
        xf, attn_norm[0][None], attn_w_qkv[0].astype(BF16), attn_q_gain[0][None], attn_k_gain[0][None],
        [_Cast(ffn_w_up, 0, 32), _Cast(ffn_w_down, 0, 64), _Cast(attn_w_o, 0, 32), _Cast(lstm_w_in, 0, 32)])
    att, (w_up1, w_down1) = _attn_call(qkv, batch, seq, [_Cast(ffn_w_up, 1, 64), _Cast(ffn_w_down, 1, 176)])
    xf = _proj_call(att, w_o, xf)
    xf = ffn(xf, 0, w_up0, w_down0)

    n_gates = 2 * LSTM_HEADS
    w_gate = jnp.pad(w_in[:, LSTM_MAIN_WIDTH:], ((0, 0), (0, LANES - n_gates)))
    b_gate = jnp.pad(lstm_gate_bias[0], (0, LANES - n_gates))[None]
    z, gates = _lstm_in_call(xf, lstm_norm[0][None], w_in, w_gate, b_gate,
                             lstm_conv_w[0], lstm_conv_b[0][None], seq)
    hs = _mlstm_call(z, gates, lstm_head_gain[0][None], batch, seq)
    xf = _proj_call(hs.reshape(m, d), lstm_w_out[0].astype(BF16), xf)
    xf = ffn(xf, 1, w_up1, w_down1)
    return xf.reshape(batch, seq, d)
```

```python
import functools
from typing import NamedTuple, Optional

import jax
import jax.numpy as jnp
from jax import lax
from jax.experimental import pallas as pl
from jax.experimental.pallas import tpu as pltpu

F32 = jnp.float32
BF16 = jnp.bfloat16

D_MODEL = 2048
ATT_HEADS = 16
ATT_HEAD_DIM = D_MODEL // ATT_HEADS
DILATED_PATTERNS = ((128, 1), (512, 4), (2048, 16))
ATT_BLOCK = 128
ATT_GROUP = 8
LSTM_HEADS = 4
LSTM_V_DIM = D_MODEL // LSTM_HEADS
LSTM_QK_DIM = LSTM_V_DIM // 2
LSTM_QK_WIDTH = LSTM_HEADS * LSTM_QK_DIM
LSTM_MAIN_WIDTH = 2 * LSTM_QK_WIDTH + 2 * D_MODEL
LSTM_CONV = 4
FFN_DIM = ((8 * D_MODEL // 3 + 255) // 256) * 256
FFN_CONV = 3
NORM_EPS = 1e-6

LANES = 128
SUBLANES = 8
BF16_ROWS_PER_VREG = 2 * SUBLANES
HALO = BF16_ROWS_PER_VREG
VMEM_LIMIT = 56 * 1024 * 1024

LSTM_CHUNK = 256

NORM_ROWS = 128


def _rms(x, g):
    ms = jnp.mean(x * x, axis=-1, keepdims=True)
    return x * lax.rsqrt(ms + NORM_EPS) * g


def _norm_rows(h_ref, lead, x_ref, g, copy_ref=None):
    def body(r, carry):
        start = pl.multiple_of(r * NORM_ROWS, NORM_ROWS)
        x = x_ref[pl.ds(start, NORM_ROWS), :]
        h_ref[pl.ds(pl.multiple_of(start + lead, BF16_ROWS_PER_VREG), NORM_ROWS), :] = _rms(x, g).astype(BF16)
        if copy_ref is not None:
            copy_ref[pl.ds(start, NORM_ROWS), :] = x
        return carry

    lax.fori_loop(0, x_ref.shape[0] // NORM_ROWS, body, 0)


def _dot(a, b):
    return jnp.dot(a, b, preferred_element_type=F32)


def _dot_nt(a, b):
    return lax.dot_general(a, b, (((1,), (1,)), ((), ())), preferred_element_type=F32)


def _dot_tn(a, b):
    return lax.dot_general(a, b, (((0,), (0,)), ((), ())), preferred_element_type=F32)


def _params(sem):
    return pltpu.CompilerParams(dimension_semantics=sem, vmem_limit_bytes=VMEM_LIMIT)


class _Cast(NamedTuple):
    src: jax.Array
    layer: Optional[int]
    rows: int
    total_rows: Optional[int] = None


def _cast_plan(casts, step_of):
    in_specs, out_specs, out_shapes, n_blocks = [], [], [], []
    for c in casts:
        r, cols = c.src.shape[-2:]
        r = c.total_rows or r
        assert r % c.rows == 0 and c.rows % BF16_ROWS_PER_VREG == 0
        nb = r // c.rows

        def block(*ids, nb=nb):
            return jnp.minimum(step_of(*ids), nb - 1)

        if c.layer is None:
            in_specs.append(pl.BlockSpec((c.rows, cols), lambda *ids, block=block: (block(*ids), 0)))
        else:
            in_specs.append(pl.BlockSpec((None, c.rows, cols),
                                         lambda *ids, block=block, layer=c.layer: (layer, block(*ids), 0)))
        out_specs.append(pl.BlockSpec((c.rows, cols), lambda *ids, block=block: (block(*ids), 0)))
        out_shapes.append(jax.ShapeDtypeStruct((r, cols), BF16))
        n_blocks.append(nb)
    return in_specs, out_specs, out_shapes, tuple(n_blocks)


def _cast_step(step, src_refs, dst_refs, n_blocks):
    for src, dst, nb in zip(src_refs, dst_refs, n_blocks):
        @pl.when(step < nb)
        def _(src=src, dst=dst):
            for r0 in range(0, src.shape[0], BF16_ROWS_PER_VREG):
                sl = slice(r0, r0 + BF16_ROWS_PER_VREG)
                dst[sl, :] = src[sl, :].astype(BF16)


def _row_chunks(h_ref, w_refs, rows, lead, with_halo, consume, dot=_dot):
    n_chunks = (h_ref.shape[0] - lead) // rows
    assert not with_halo or lead == HALO

    def project(c, tails):
        if with_halo and c == 0:
            h = h_ref[0:rows + lead, :]
            return [dot(h, w[...]) for w in w_refs]
        h = h_ref[c * rows + lead:(c + 1) * rows + lead, :]
        us = [dot(h, w[...]) for w in w_refs]
        if with_halo:
            us = [jnp.concatenate([t, u], axis=0) for t, u in zip(tails, us)]
        return us

    us = project(0, None)
    for c in range(n_chunks):
        nxt = project(c + 1, [u[rows:, :] for u in us]) if c + 1 < n_chunks else None
        consume(c, us)
        us = nxt


def _row_chunks_conv(h_ref, w_refs, u_refs, rows, consume, dot=_dot):
    n_chunks = (h_ref.shape[0] - HALO) // rows

    def project(c):
        lo = 0 if c == 0 else c * rows + HALO
        hi = (c + 1) * rows + HALO
        h = h_ref[lo:hi, :]
        for w, u in zip(w_refs, u_refs):
            u[lo:hi, :] = dot(h, w[...])

    project(0)
    for c in range(n_chunks):
        if c + 1 < n_chunks:
            project(c + 1)
        consume(c)


def _qkv_kernel(*refs, rows, n_head_blocks, heads_per_block, cast_blocks):
    nc = len(cast_blocks)
    x_ref, g_ref, w_ref, qg_ref, kg_ref = refs[:5]
    cast_src, o_ref, cast_dst, h_ref = refs[5:5 + nc], refs[5 + nc], refs[6 + nc:6 + 2 * nc], refs[6 + 2 * nc]
    j = pl.program_id(1)
    _cast_step(pl.program_id(0) * pl.num_programs(1) + j, cast_src, cast_dst, cast_blocks)

    @pl.when(j == 0)
    def _():
        _norm_rows(h_ref, 0, x_ref, g_ref[...])

    @pl.when(j < 2 * n_head_blocks)
    def _():
        gain = jnp.where(j < n_head_blocks, qg_ref[...] * (ATT_HEAD_DIM ** -0.5), kg_ref[...])

        def qk_norm(c, us):
            for hh in range(heads_per_block):
                sl = slice(hh * ATT_HEAD_DIM, (hh + 1) * ATT_HEAD_DIM)
                o_ref[c * rows:(c + 1) * rows, sl] = _rms(us[0][:, sl], gain)

        _row_chunks(h_ref, [w_ref], rows, 0, False, qk_norm)

    @pl.when(j >= 2 * n_head_blocks)
    def _():
        def copy(c, us):
            o_ref[c * rows:(c + 1) * rows, :] = us[0]

        _row_chunks(h_ref, [w_ref], rows, 0, False, copy)


def _qkv_call(x, g, w, qg, kg, casts, *, tm=1024, tn=512, rows=256):
    m = x.shape[0]
    n = w.shape[1]
    heads_per_block = tn // ATT_HEAD_DIM
    n_head_blocks = D_MODEL // tn
    n_col_blocks = n // tn
    cast_in, cast_out, cast_shapes, cast_blocks = _cast_plan(casts, lambda i, j: i * n_col_blocks + j)
    qkv, *cast = pl.pallas_call(
        functools.partial(_qkv_kernel, rows=rows, n_head_blocks=n_head_blocks, heads_per_block=heads_per_block,
                          cast_blocks=cast_blocks),
        grid=(m // tm, n_col_blocks),
        in_specs=[
            pl.BlockSpec((tm, D_MODEL), lambda i, j: (i, 0)),
            pl.BlockSpec((1, D_MODEL), lambda i, j: (0, 0)),
            pl.BlockSpec((D_MODEL, tn), lambda i, j: (0, j)),
            pl.BlockSpec((1, ATT_HEAD_DIM), lambda i, j: (0, 0)),
            pl.BlockSpec((1, ATT_HEAD_DIM), lambda i, j: (0, 0)),
        ] + cast_in,
        out_specs=[pl.BlockSpec((tm, tn), lambda i, j: (i, j))] + cast_out,
        out_shape=[jax.ShapeDtypeStruct((m, n), F32)] + cast_shapes,
        scratch_shapes=[pltpu.VMEM((tm, D_MODEL), BF16)],
        compiler_params=_params(("arbitrary", "arbitrary")),
        name="attn_qkv",
    )(x, g, w, qg, kg, *[c.src for c in casts])
    return qkv, cast


ROW_INTERLEAVE = 4


def _attn_kernel(*refs, seq, cast_blocks):
    nc = len(cast_blocks)
    q_ref, k_ref, v_ref = refs[:3]
    cast_src, o_ref, cast_dst = refs[3:3 + nc], refs[3 + nc], refs[4 + nc:4 + 2 * nc]
    z_ref, acc_ref, den_ref, mx_ref, bias_ref, nat_ref = refs[4 + 2 * nc:]
    _cast_step(pl.program_id(0) * pl.num_programs(1) + pl.program_id(1), cast_src, cast_dst, cast_blocks)
    _attn_body(q_ref, k_ref, v_ref, o_ref, z_ref, acc_ref, den_ref, mx_ref, bias_ref, nat_ref, seq=seq)


def _attn_body(q_ref, k_ref, v_ref, o_ref, z_ref, acc_ref, den_ref, mx_ref, bias_ref, nat_ref, *, seq):
    blk, hd, grp, il = ATT_BLOCK, ATT_HEAD_DIM, ATT_GROUP, ROW_INTERLEAVE
    part = seq // il
    n_chunks = seq // blk
    assert part % blk == 0

    def chunk(c):
        return pl.ds(pl.multiple_of(c * blk, blk), blk)

    def natural_rows(c):
        lo = c // (part // blk)
        b0 = (c % (part // blk)) * blk
        return pl.ds(lo + il * b0, blk, stride=il)

    for ti, src_ref in enumerate((q_ref, k_ref, v_ref)):
        def interleave(c, carry, ti=ti, src_ref=src_ref):
            z_ref[ti, chunk(c), :] = src_ref[natural_rows(c), :]
            return carry

        lax.fori_loop(0, n_chunks, interleave, 0, unroll=2)

    qz_ref, kz_ref, vz_ref = z_ref.at[0], z_ref.at[1], z_ref.at[2]
    row = lax.broadcasted_iota(jnp.int32, (blk, 2 * blk), 0)
    col = lax.broadcasted_iota(jnp.int32, (blk, 2 * blk), 1)
    ones = jnp.ones((grp, 2 * blk, hd), BF16)

    for gi, (window, dil) in enumerate(DILATED_PATTERNS):
        assert window // dil == blk and (dil % il == 0 or il % dil == 0)
        pieces = max(il // dil, 1)
        plen = blk // pieces
        stride = max(dil // il, 1)
        n_blocks = seq // (dil * blk)
        assert n_blocks & (n_blocks - 1) == 0 and (dil * n_blocks) % grp == 0 and plen % SUBLANES == 0
        log_nb = n_blocks.bit_length() - 1
        log_plen = plen.bit_length() - 1

        def step_in_block(i, pieces=pieces, plen=plen, log_plen=log_plen):
            return pieces * jnp.bitwise_and(i, plen - 1) + lax.shift_right_logical(i, log_plen)

        dist = step_in_block(row) - step_in_block(jnp.bitwise_and(col, blk - 1)) + jnp.where(col < blk, blk, 0)
        band = jnp.logical_and(dist >= 0, dist <= blk)
        bias_ref[2 * gi + 1] = jnp.where(band, 0.0, -jnp.inf)
        bias_ref[2 * gi] = jnp.where(jnp.logical_and(band, col >= blk), 0.0, -jnp.inf)

        def block_rows(a, n, dil=dil, pieces=pieces, plen=plen, stride=stride):
            if pieces > 1:
                return [pl.ds(pl.multiple_of((p * dil + a) * part + n * plen, SUBLANES), plen) for p in range(pieces)]
            lo = jnp.bitwise_and(a, il - 1)
            hi = lax.shift_right_logical(a, il.bit_length() - 1)
            start = lo * part + hi + stride * blk * n
            if stride == 1:
                return [pl.ds(pl.multiple_of(start, blk), blk)]
            return [pl.ds(start, blk, stride=stride)]

        def load(ref, parts):
            return jnp.concatenate([ref[d, :] for d in parts], axis=0)

        def store(ref, gi, parts, val, plen=plen):
            for p, d in enumerate(parts):
                ref[gi, d, :] = val[p * plen:(p + 1) * plen, :]

        def group(it, carry, gi=gi, n_blocks=n_blocks, log_nb=log_nb, block_rows=block_rows, load=load, store=store):
            cur, qs, ks, vs, bias = [], [], [], [], []
            for gg in range(grp):
                idx = it * grp + gg
                a = lax.shift_right_logical(idx, log_nb)
                n = jnp.bitwise_and(idx, n_blocks - 1)
                c = block_rows(a, n)
                p = block_rows(a, jnp.maximum(n - 1, 0))
                cur.append(c)
                qs.append(load(qz_ref, c))
                ks.append(jnp.concatenate([load(kz_ref, p), load(kz_ref, c)], axis=0))
                vs.append(jnp.concatenate([load(vz_ref, p), load(vz_ref, c)], axis=0))
                bias.append(bias_ref[2 * gi + jnp.minimum(n, 1)])
            q = jnp.stack(qs).astype(BF16)
            k = jnp.stack(ks).astype(BF16)
            v = jnp.concatenate([jnp.stack(vs).astype(BF16), ones], axis=-1)
            s = jnp.einsum("gqd,gkd->gqk", q, k, preferred_element_type=F32) + jnp.stack(bias)
            mx = jnp.max(s, axis=-1, keepdims=True)
            p = jnp.exp(s - mx).astype(BF16)
            pv = jnp.einsum("gqk,gkd->gqd", p, v, preferred_element_type=F32)
            for gg in range(grp):
                store(acc_ref, gi, cur[gg], pv[gg, :, :hd])
                store(den_ref, gi, cur[gg], pv[gg, :, hd:])
                store(mx_ref, gi, cur[gg], jnp.broadcast_to(mx[gg], (blk, hd)))
            return carry

        lax.fori_loop(0, dil * n_blocks // grp, group, 0)

    def combine(c, carry):
        sl = chunk(c)
        m0 = mx_ref[0, sl, :]
        m1 = mx_ref[1, sl, :]
        m2 = mx_ref[2, sl, :]
        mx = jnp.maximum(jnp.maximum(m0, m1), m2)
        w0 = jnp.exp(m0 - mx)
        w1 = jnp.exp(m1 - mx)
        w2 = jnp.exp(m2 - mx)
        num = w0 * acc_ref[0, sl, :] + w1 * acc_ref[1, sl, :] + w2 * acc_ref[2, sl, :]
        den = w0 * den_ref[0, sl, :] + w1 * den_ref[1, sl, :] + w2 * den_ref[2, sl, :]
        nat_ref[natural_rows(c), :] = num / den
        return carry

    lax.fori_loop(0, n_chunks, combine, 0)

    def emit(c, carry):
        o_ref[chunk(c), :] = nat_ref[chunk(c), :].astype(o_ref.dtype)
        return carry

    lax.fori_loop(0, n_chunks, emit, 0, unroll=2)


def _attn_call(qkv, batch, seq, casts):
    qkv = qkv.reshape(batch, seq, 3 * D_MODEL)
    hd = ATT_HEAD_DIM
    n_groups = len(DILATED_PATTERNS)
    cast_in, cast_out, cast_shapes, cast_blocks = _cast_plan(casts, lambda b, h: b * ATT_HEADS + h)
    out, *cast = pl.pallas_call(
        functools.partial(_attn_kernel, seq=seq, cast_blocks=cast_blocks),
        grid=(batch, ATT_HEADS),
        in_specs=[
            pl.BlockSpec((None, seq, hd), lambda b, h: (b, 0, h)),
            pl.BlockSpec((None, seq, hd), lambda b, h: (b, 0, ATT_HEADS + h)),
            pl.BlockSpec((None, seq, hd), lambda b, h: (b, 0, 2 * ATT_HEADS + h)),
        ] + cast_in,
        out_specs=[pl.BlockSpec((None, seq, hd), lambda b, h: (b, 0, h))] + cast_out,
        out_shape=[jax.ShapeDtypeStruct((batch, seq, D_MODEL), BF16)] + cast_shapes,
        scratch_shapes=[
            pltpu.VMEM((3, seq, hd), F32),
            pltpu.VMEM((n_groups, seq, hd), F32),
            pltpu.VMEM((n_groups, seq, hd), F32),
            pltpu.VMEM((n_groups, seq, hd), F32),
            pltpu.VMEM((2 * n_groups, ATT_BLOCK, 2 * ATT_BLOCK), F32),
            pltpu.VMEM((seq, hd), F32),
        ],
        compiler_params=_params(("arbitrary", "arbitrary")),
        name="dilated_attn",
    )(qkv, qkv, qkv, *[c.src for c in casts])
    return out.reshape(batch * seq, D_MODEL), cast


def _proj_kernel(a_ref, w_ref, x_ref, o_ref, *, rows):
    for c in range(o_ref.shape[0] // rows):
        sl = slice(c * rows, (c + 1) * rows)
        o_ref[sl, :] = x_ref[sl, :] + _dot(a_ref[sl, :], w_ref[...])


def _proj_call(a, w, x, *, tm=512, rows=256):
    m, k = a.shape
    n = w.shape[1]
    return pl.pallas_call(
        functools.partial(_proj_kernel, rows=rows),
        grid=(m // tm,),
        in_specs=[
            pl.BlockSpec((tm, k), lambda i: (i, 0)),
            pl.BlockSpec((k, n), lambda i: (0, 0)),
            pl.BlockSpec((tm, n), lambda i: (i, 0)),
        ],
        out_specs=pl.BlockSpec((tm, n), lambda i: (i, 0)),
        out_shape=jax.ShapeDtypeStruct((m, n), F32),
        compiler_params=_params(("parallel",)),
        name="proj_residual",
    )(a, w, x)


def _fill_normed(h_ref, x_ref, xh_ref, g_ref, seq_start, copy_ref=None):
    g = g_ref[...]
    halo = jnp.where(seq_start, 0.0, _rms(xh_ref[...], g))
    h_ref[0:HALO, :] = halo.astype(BF16)
    _norm_rows(h_ref, HALO, x_ref, g, copy_ref)


def _causal_conv(u_ref, c, w_ref, b_ref, taps, rows):
    w = w_ref[...]
    y = b_ref[...]
    for j in range(taps):
        off = c * rows + HALO - (taps - 1) + j
        y = y + w[j:j + 1, :] * u_ref[off:off + rows, :]
    return y


def _ffn_kernel(x_ref, xh_ref, g_ref, wg_ref, wu_ref, cwg_ref, cwu_ref, cbg_ref, cbu_ref, wd_ref,
                o_ref, h_ref, ug_ref, uu_ref, *, rows, tiles_per_seq):
    i = pl.program_id(0)
    j = pl.program_id(1)

    @pl.when(j == 0)
    def _():
        _fill_normed(h_ref, x_ref, xh_ref, g_ref, i % tiles_per_seq == 0, copy_ref=o_ref)

    def down_proj(c):
        gate = _causal_conv(ug_ref, c, cwg_ref, cbg_ref, FFN_CONV, rows)
        up = _causal_conv(uu_ref, c, cwu_ref, cbu_ref, FFN_CONV, rows)
        act = (gate * up / (1.0 + jnp.exp(-gate))).astype(BF16)
        o_ref[c * rows:(c + 1) * rows, :] += _dot(act, wd_ref[...])

    _row_chunks_conv(h_ref, [wg_ref, wu_ref], [ug_ref, uu_ref], rows, down_proj)


def _ffn_call(x, g, w_up, conv_w, conv_b, w_down, layer, seq, *, tm=1024, tf=512, rows=256):
    m = x.shape[0]
    nf = FFN_DIM // tf
    halo_blocks_per_tile = tm // HALO
    return pl.pallas_call(
        functools.partial(_ffn_kernel, rows=rows, tiles_per_seq=seq // tm),
        grid=(m // tm, nf),
        in_specs=[
            pl.BlockSpec((tm, D_MODEL), lambda i, j: (i, 0)),
            pl.BlockSpec((HALO, D_MODEL), lambda i, j: (jnp.maximum(i * halo_blocks_per_tile - 1, 0), 0)),
            pl.BlockSpec((None, 1, D_MODEL), lambda i, j: (layer, 0, 0)),
            pl.BlockSpec((D_MODEL, tf), lambda i, j: (0, j)),
            pl.BlockSpec((D_MODEL, tf), lambda i, j: (0, nf + j)),
            pl.BlockSpec((None, FFN_CONV, tf), lambda i, j: (layer, 0, j)),
            pl.BlockSpec((None, FFN_CONV, tf), lambda i, j: (layer, 0, nf + j)),
            pl.BlockSpec((None, 1, tf), lambda i, j: (layer, 0, j)),
            pl.BlockSpec((None, 1, tf), lambda i, j: (layer, 0, nf + j)),
            pl.BlockSpec((tf, D_MODEL), lambda i, j: (j, 0)),
        ],
        out_specs=pl.BlockSpec((tm, D_MODEL), lambda i, j: (i, 0)),
        out_shape=jax.ShapeDtypeStruct((m, D_MODEL), F32),
        scratch_shapes=[
            pltpu.VMEM((tm + HALO, D_MODEL), BF16),
            pltpu.VMEM((tm + HALO, tf), F32),
            pltpu.VMEM((tm + HALO, tf), F32),
        ],
        compiler_params=_params(("parallel", "arbitrary")),
        name="conv_ffn",
    )(x, x, g, w_up, w_up, conv_w, conv_w, conv_b, conv_b, w_down)


def _lstm_in_kernel(x_ref, xh_ref, g_ref, w_ref, wgate_ref, bgate_ref, cw_ref, cb_ref,
                    z_ref, gates_ref, h_ref, u_ref, *, rows, tiles_per_seq, n_conv_blocks):
    i = pl.program_id(0)
    j = pl.program_id(1)

    @pl.when(j == 0)
    def _():
        _fill_normed(h_ref, x_ref, xh_ref, g_ref, i % tiles_per_seq == 0)
        wg = wgate_ref[...]
        wg = jnp.concatenate([wg, jnp.zeros((LANES - wg.shape[0], wg.shape[1]), F32)], axis=0).astype(BF16)
        gates_ref[...] = _dot_nt(h_ref[HALO:, :], wg) + bgate_ref[...]

    @pl.when(j < n_conv_blocks)
    def _():
        scale = jnp.where(j < n_conv_blocks // 2, 1.0, LSTM_QK_DIM ** -0.5)

        def conv_silu(c):
            y = _causal_conv(u_ref, c, cw_ref, cb_ref, LSTM_CONV, rows)
            z_ref[c * rows:(c + 1) * rows, :] = y * scale / (1.0 + jnp.exp(-y))

        _row_chunks_conv(h_ref, [w_ref], [u_ref], rows, conv_silu, dot=_dot_nt)

    @pl.when(j >= n_conv_blocks)
    def _():
        def copy(c, us):
            z_ref[c * rows:(c + 1) * rows, :] = us[0]

        _row_chunks(h_ref, [w_ref], rows, HALO, False, copy, dot=_dot_nt)


def _lstm_in_call(x, g, w_main_t, w_in_t, b_gate, conv_w, conv_b, seq, *, tm=1024, tn=512, rows=256):
    m = x.shape[0]
    n = LSTM_MAIN_WIDTH
    n_gates = 2 * LSTM_HEADS
    assert w_in_t.shape[0] == n + n_gates and n % n_gates == 0 and n_gates == SUBLANES
    n_conv_blocks = 2 * LSTM_QK_WIDTH // tn
    halo_blocks_per_tile = tm // HALO
    return pl.pallas_call(
        functools.partial(_lstm_in_kernel, rows=rows, tiles_per_seq=seq // tm, n_conv_blocks=n_conv_blocks),
        grid=(m // tm, n // tn),
        in_specs=[
            pl.BlockSpec((tm, D_MODEL), lambda i, j: (i, 0)),
            pl.BlockSpec((HALO, D_MODEL), lambda i, j: (jnp.maximum(i * halo_blocks_per_tile - 1, 0), 0)),
            pl.BlockSpec((1, D_MODEL), lambda i, j: (0, 0)),
            pl.BlockSpec((tn, D_MODEL), lambda i, j: (j, 0)),
            pl.BlockSpec((n_gates, D_MODEL), lambda i, j: (n // n_gates, 0)),
            pl.BlockSpec((1, LANES), lambda i, j: (0, 0)),
            pl.BlockSpec((LSTM_CONV, tn), lambda i, j: (0, jnp.minimum(j, n_conv_blocks - 1))),
            pl.BlockSpec((1, tn), lambda i, j: (0, jnp.minimum(j, n_conv_blocks - 1))),
        ],
        out_specs=[
            pl.BlockSpec((tm, tn), lambda i, j: (i, j)),
            pl.BlockSpec((tm, LANES), lambda i, j: (i, 0)),
        ],
        out_shape=[
            jax.ShapeDtypeStruct((m, n), F32),
            jax.ShapeDtypeStruct((m, LANES), F32),
        ],
        scratch_shapes=[
            pltpu.VMEM((tm + HALO, D_MODEL), BF16),
            pltpu.VMEM((tm + HALO, tn), F32),
        ],
        compiler_params=_params(("parallel", "arbitrary")),
        name="lstm_in",
    )(x, x, g, w_main_t, w_in_t, b_gate, conv_w, conv_b)


def _log_sigmoid(x):
    return -(jnp.maximum(-x, 0.0) + jnp.log1p(jnp.exp(-jnp.abs(x))))


def _cumsum_rows(x):
    n = x.shape[0]
    row = lax.broadcasted_iota(jnp.int32, x.shape, 0)
    shift = 1
    while shift < n:
        x = x + jnp.where(row >= shift, pltpu.roll(x, shift, axis=0), 0.0)
        shift *= 2
    return x


def _mlstm_kernel(q_ref, k_ref, v_ref, og_ref, gates_ref, hg_ref, o_ref, c_ref, n_ref, m_ref):
    chunk = q_ref.shape[0]
    dk, dv, heads = LSTM_QK_DIM, LSTM_V_DIM, LSTM_HEADS

    @pl.when(pl.program_id(1) == 0)
    def _():
        c_ref[...] = jnp.zeros_like(c_ref)
        n_ref[...] = jnp.zeros_like(n_ref)
        m_ref[...] = jnp.zeros_like(m_ref)

    gates = gates_ref[...]
    cum_f = _cumsum_rows(_log_sigmoid(gates))
    gates_t = gates.T
    cum_f_t = cum_f.T
    row = lax.broadcasted_iota(jnp.int32, (chunk, chunk), 0)
    col = lax.broadcasted_iota(jnp.int32, (chunk, chunk), 1)
    causal = col <= row

    for hd in range(heads):
        qs = slice(hd * dk, (hd + 1) * dk)
        vs = slice(hd * dv, (hd + 1) * dv)
        b_col = cum_f[:, heads + hd:heads + hd + 1]
        b_row = cum_f_t[heads + hd:heads + hd + 1, :]
        li_col = gates[:, hd:hd + 1]
        li_row = gates_t[hd:hd + 1, :]
        m_prev = m_ref[hd]
        c_prev = c_ref[hd]
        n_prev = n_ref[hd]
        q = q_ref[:, qs]
        k = k_ref[:, qs]
        v = v_ref[:, vs]
        qb = q.astype(BF16)
        kb = k.astype(BF16)

        dmat = jnp.where(causal, b_col - b_row + li_row, -jnp.inf)
        g = b_col + m_prev
        m_t = jnp.maximum(g, jnp.max(dmat, axis=-1, keepdims=True))
        p = jnp.exp(dmat - m_t)
        inter = jnp.exp(g - m_t)
        w = p * _dot_nt(qb, kb)
        num = inter * _dot(qb, c_prev.astype(BF16)) + _dot(w.astype(BF16), v.astype(BF16))
        den = inter * jnp.sum(q * n_prev, axis=-1, keepdims=True) + jnp.sum(w, axis=-1, keepdims=True)
        h = num / jnp.maximum(jnp.abs(den), jnp.exp(-m_t))

        b_last = b_col[chunk - 1:chunk, :]
        a = b_last - b_col + li_col
        m_new = jnp.maximum(b_last + m_prev, jnp.max(a, axis=0, keepdims=True))
        decay = jnp.exp(b_last + m_prev - m_new)
        wts = jnp.exp(a - m_new)
        c_ref[hd] = decay * c_prev + _dot_tn(kb, (wts * v).astype(BF16))
        n_ref[hd] = decay * n_prev + jnp.sum(wts * k, axis=0, keepdims=True)
        m_ref[hd] = m_new

        hs = _rms(h, hg_ref[:, vs]) * jax.nn.sigmoid(og_ref[:, vs])
        o_ref[:, vs] = hs.astype(o_ref.dtype)


def _mlstm_call(z, gates, head_gain, batch, seq):
    chunk = LSTM_CHUNK
    z = z.reshape(batch, seq, LSTM_MAIN_WIDTH)
    gates = gates.reshape(batch, seq, LANES)
    qkw = LSTM_QK_WIDTH
    return pl.pallas_call(
        _mlstm_kernel,
        grid=(batch, seq // chunk),
        in_specs=[
            pl.BlockSpec((None, chunk, qkw), lambda b, c: (b, c, 0)),
            pl.BlockSpec((None, chunk, qkw), lambda b, c: (b, c, 1)),
            pl.BlockSpec((None, chunk, D_MODEL), lambda b, c: (b, c, 2 * qkw // D_MODEL)),
            pl.BlockSpec((None, chunk, D_MODEL), lambda b, c: (b, c, 2 * qkw // D_MODEL + 1)),
            pl.BlockSpec((None, chunk, LANES), lambda b, c: (b, c, 0)),
            pl.BlockSpec((1, D_MODEL), lambda b, c: (0, 0)),
        ],
        out_specs=pl.BlockSpec((None, chunk, D_MODEL), lambda b, c: (b, c, 0)),
        out_shape=jax.ShapeDtypeStruct((batch, seq, D_MODEL), BF16),
        scratch_shapes=[
            pltpu.VMEM((LSTM_HEADS, LSTM_QK_DIM, LSTM_V_DIM), F32),
            pltpu.VMEM((LSTM_HEADS, 1, LSTM_QK_DIM), F32),
            pltpu.VMEM((LSTM_HEADS, 1, 1), F32),
        ],
        compiler_params=_params(("parallel", "arbitrary")),
        name="mlstm",
    )(z, z, z, z, gates, head_gain)


def kernel(x, attn_norm, attn_w_qkv, attn_q_gain, attn_k_gain, attn_w_o, lstm_norm, lstm_w_in, lstm_gate_bias,
           lstm_conv_w, lstm_conv_b, lstm_head_gain, lstm_w_out, ffn_norm, ffn_w_up, ffn_conv_w, ffn_conv_b,
           ffn_w_down):
    batch, seq, d = x.shape
    assert d == D_MODEL and ffn_norm.shape[0] == 2 and attn_norm.shape[0] == 1 and lstm_norm.shape[0] == 1
    m = batch * seq
    xf = x.reshape(m, d)

    def ffn(xin, layer, w_up, w_down):
        return _ffn_call(xin, ffn_norm[:, None, :], w_up, ffn_conv_w, ffn_conv_b[:, None, :], w_down, layer, seq)

    w_in_t = jnp.swapaxes(lstm_w_in[0], 0, 1)
    qkv, (w_up0, w_down0, w_o, w_in_main_t) = _qkv_call(
        xf, attn_norm[0][None], attn_w_qkv[0].astype(BF16), attn_q_gain[0][None], attn_k_gain[0][None],
        [_Cast(ffn_w_up, 0, 32), _Cast(ffn_w_down, 0, 64), _Cast(attn_w_o, 0, 32),
         _Cast(w_in_t, None, 64, LSTM_MAIN_WIDTH)])
    att, (w_up1, w_down1) = _attn_call(qkv, batch, seq, [_Cast(ffn_w_up, 1, 64), _Cast(ffn_w_down, 1, 176)])
    xf = _proj_call(att, w_o, xf)
    xf = ffn(xf, 0, w_up0, w_down0)

    n_gates = 2 * LSTM_HEADS
    b_gate = jnp.pad(lstm_gate_bias[0], (0, LANES - n_gates))[None]
    z, gates = _lstm_in_call(xf, lstm_norm[0][None], w_in_main_t, w_in_t, b_gate,
                             lstm_conv_w[0], lstm_conv_b[0][None], seq)
    hs = _mlstm_call(z, gates, lstm_head_gain[0][None], batch, seq)
    xf = _proj_call(hs.reshape(m, d), lstm_w_out[0].astype(BF16), xf)
    xf = ffn(xf, 1, w_up1, w_down1)
    return xf.reshape(batch, seq, d)
```

```python
import functools
from typing import NamedTuple, Optional

import jax
import jax.numpy as jnp
from jax import lax
from jax.experimental import pallas as pl
from jax.experimental.pallas import tpu as pltpu

F32 = jnp.float32
BF16 = jnp.bfloat16

D_MODEL = 2048
ATT_HEADS = 16
ATT_HEAD_DIM = D_MODEL // ATT_HEADS
DILATED_PATTERNS = ((128, 1), (512, 4), (2048, 16))
ATT_BLOCK = 128
ATT_GROUP = 8
LSTM_HEADS = 4
LSTM_V_DIM = D_MODEL // LSTM_HEADS
LSTM_QK_DIM = LSTM_V_DIM // 2
LSTM_QK_WIDTH = LSTM_HEADS * LSTM_QK_DIM
LSTM_MAIN_WIDTH = 2 * LSTM_QK_WIDTH + 2 * D_MODEL
LSTM_CONV = 4
FFN_DIM = ((8 * D_MODEL // 3 + 255) // 256) * 256
FFN_CONV = 3
NORM_EPS = 1e-6

LANES = 128
SUBLANES = 8
BF16_ROWS_PER_VREG = 2 * SUBLANES
HALO = BF16_ROWS_PER_VREG
VMEM_LIMIT = 56 * 1024 * 1024

LSTM_CHUNK = 256

NORM_ROWS = 128


def _rms(x, g):
    ms = jnp.mean(x * x, axis=-1, keepdims=True)
    return x * lax.rsqrt(ms + NORM_EPS) * g


def _norm_rows(h_ref, lead, x_ref, g, copy_ref=None):
    def body(r, carry):
        start = pl.multiple_of(r * NORM_ROWS, NORM_ROWS)
        x = x_ref[pl.ds(start, NORM_ROWS), :]
        h_ref[pl.ds(pl.multiple_of(start + lead, BF16_ROWS_PER_VREG), NORM_ROWS), :] = _rms(x, g).astype(BF16)
        if copy_ref is not None:
            copy_ref[pl.ds(start, NORM_ROWS), :] = x
        return carry

    lax.fori_loop(0, x_ref.shape[0] // NORM_ROWS, body, 0)


def _dot(a, b):
    return jnp.dot(a, b, preferred_element_type=F32)


def _dot_nt(a, b):
    return lax.dot_general(a, b, (((1,), (1,)), ((), ())), preferred_element_type=F32)


def _dot_tn(a, b):
    return lax.dot_general(a, b, (((0,), (0,)), ((), ())), preferred_element_type=F32)


def _params(sem):
    return pltpu.CompilerParams(dimension_semantics=sem, vmem_limit_bytes=VMEM_LIMIT)


class _Cast(NamedTuple):
    src: jax.Array
    layer: Optional[int]
    rows: int
    total_rows: Optional[int] = None


def _cast_plan(casts, step_of):
    in_specs, out_specs, out_shapes, n_blocks = [], [], [], []
    for c in casts:
        r, cols = c.src.shape[-2:]
        r = c.total_rows or r
        assert r % c.rows == 0 and c.rows % BF16_ROWS_PER_VREG == 0
        nb = r // c.rows

        def block(*ids, nb=nb):
            return jnp.minimum(step_of(*ids), nb - 1)

        if c.layer is None:
            in_specs.append(pl.BlockSpec((c.rows, cols), lambda *ids, block=block: (block(*ids), 0)))
        else:
            in_specs.append(pl.BlockSpec((None, c.rows, cols),
                                         lambda *ids, block=block, layer=c.layer: (layer, block(*ids), 0)))
        out_specs.append(pl.BlockSpec((c.rows, cols), lambda *ids, block=block: (block(*ids), 0)))
        out_shapes.append(jax.ShapeDtypeStruct((r, cols), BF16))
        n_blocks.append(nb)
    return in_specs, out_specs, out_shapes, tuple(n_blocks)


def _cast_step(step, src_refs, dst_refs, n_blocks):
    for src, dst, nb in zip(src_refs, dst_refs, n_blocks):
        @pl.when(step < nb)
        def _(src=src, dst=dst):
            for r0 in range(0, src.shape[0], BF16_ROWS_PER_VREG):
                sl = slice(r0, r0 + BF16_ROWS_PER_VREG)
                dst[sl, :] = src[sl, :].astype(BF16)


def _row_chunks(h_ref, w_refs, rows, lead, with_halo, consume, dot=_dot):
    n_chunks = (h_ref.shape[0] - lead) // rows
    assert not with_halo or lead == HALO

    def project(c, tails):
        if with_halo and c == 0:
            h = h_ref[0:rows + lead, :]
            return [dot(h, w[...]) for w in w_refs]
        h = h_ref[c * rows + lead:(c + 1) * rows + lead, :]
        us = [dot(h, w[...]) for w in w_refs]
        if with_halo:
            us = [jnp.concatenate([t, u], axis=0) for t, u in zip(tails, us)]
        return us

    us = project(0, None)
    for c in range(n_chunks):
        nxt = project(c + 1, [u[rows:, :] for u in us]) if c + 1 < n_chunks else None
        consume(c, us)
        us = nxt


def _row_chunks_conv(h_ref, w_refs, u_refs, rows, consume, dot=_dot):
    n_chunks = (h_ref.shape[0] - HALO) // rows

    def project(c):
        lo = 0 if c == 0 else c * rows + HALO
        hi = (c + 1) * rows + HALO
        h = h_ref[lo:hi, :]
        for w, u in zip(w_refs, u_refs):
            u[lo:hi, :] = dot(h, w[...])

    project(0)
    for c in range(n_chunks):
        if c + 1 < n_chunks:
            project(c + 1)
        consume(c)


def _qkv_kernel(*refs, rows, n_head_blocks, heads_per_block, cast_blocks):
    nc = len(cast_blocks)
    x_ref, g_ref, w_ref, qg_ref, kg_ref = refs[:5]
    cast_src, o_ref, cast_dst, h_ref = refs[5:5 + nc], refs[5 + nc], refs[6 + nc:6 + 2 * nc], refs[6 + 2 * nc]
    j = pl.program_id(1)
    _cast_step(pl.program_id(0) * pl.num_programs(1) + j, cast_src, cast_dst, cast_blocks)

    @pl.when(j == 0)
    def _():
        _norm_rows(h_ref, 0, x_ref, g_ref[...])

    @pl.when(j < 2 * n_head_blocks)
    def _():
        gain = jnp.where(j < n_head_blocks, qg_ref[...] * (ATT_HEAD_DIM ** -0.5), kg_ref[...])

        def qk_norm(c, us):
            for hh in range(heads_per_block):
                sl = slice(hh * ATT_HEAD_DIM, (hh + 1) * ATT_HEAD_DIM)
                o_ref[c * rows:(c + 1) * rows, sl] = _rms(us[0][:, sl], gain)

        _row_chunks(h_ref, [w_ref], rows, 0, False, qk_norm)

    @pl.when(j >= 2 * n_head_blocks)
    def _():
        def copy(c, us):
            o_ref[c * rows:(c + 1) * rows, :] = us[0]

        _row_chunks(h_ref, [w_ref], rows, 0, False, copy)


def _qkv_call(x, g, w, qg, kg, casts, *, tm=1024, tn=1024, rows=256):
    m = x.shape[0]
    n = w.shape[1]
    heads_per_block = tn // ATT_HEAD_DIM
    n_head_blocks = D_MODEL // tn
    n_col_blocks = n // tn
    cast_in, cast_out, cast_shapes, cast_blocks = _cast_plan(casts, lambda i, j: i * n_col_blocks + j)
    qkv, *cast = pl.pallas_call(
        functools.partial(_qkv_kernel, rows=rows, n_head_blocks=n_head_blocks, heads_per_block=heads_per_block,
                          cast_blocks=cast_blocks),
        grid=(m // tm, n_col_blocks),
        in_specs=[
            pl.BlockSpec((tm, D_MODEL), lambda i, j: (i, 0)),
            pl.BlockSpec((1, D_MODEL), lambda i, j: (0, 0)),
            pl.BlockSpec((D_MODEL, tn), lambda i, j: (0, j)),
            pl.BlockSpec((1, ATT_HEAD_DIM), lambda i, j: (0, 0)),
            pl.BlockSpec((1, ATT_HEAD_DIM), lambda i, j: (0, 0)),
        ] + cast_in,
        out_specs=[pl.BlockSpec((tm, tn), lambda i, j: (i, j))] + cast_out,
        out_shape=[jax.ShapeDtypeStruct((m, n), F32)] + cast_shapes,
        scratch_shapes=[pltpu.VMEM((tm, D_MODEL), BF16)],
        compiler_params=_params(("arbitrary", "arbitrary")),
        name="attn_qkv",
    )(x, g, w, qg, kg, *[c.src for c in casts])
    return qkv, cast


ROW_INTERLEAVE = 4


def _attn_kernel(*refs, seq, cast_blocks):
    nc = len(cast_blocks)
    q_ref, k_ref, v_ref = refs[:3]
    cast_src, o_ref, cast_dst = refs[3:3 + nc], refs[3 + nc], refs[4 + nc:4 + 2 * nc]
    z_ref, acc_ref, den_ref, mx_ref, bias_ref, s_ref = refs[4 + 2 * nc:]
    _cast_step(pl.program_id(0) * pl.num_programs(1) + pl.program_id(1), cast_src, cast_dst, cast_blocks)
    _attn_body(q_ref, k_ref, v_ref, o_ref, z_ref, acc_ref, den_ref, mx_ref, bias_ref, s_ref, seq=seq)


def _attn_body(q_ref, k_ref, v_ref, o_ref, z_ref, acc_ref, den_ref, mx_ref, bias_ref, s_ref, *, seq):
    blk, hd, grp, il = ATT_BLOCK, ATT_HEAD_DIM, ATT_GROUP, ROW_INTERLEAVE
    part = seq // il
    n_chunks = seq // blk
    assert part % blk == 0

    def chunk(c):
        return pl.ds(pl.multiple_of(c * blk, blk), blk)

    def natural_rows(c):
        lo = c // (part // blk)
        b0 = (c % (part // blk)) * blk
        return pl.ds(lo + il * b0, blk, stride=il)

    for ti, src_ref in enumerate((q_ref, k_ref, v_ref)):
        def interleave(c, carry, ti=ti, src_ref=src_ref):
            z_ref[ti, chunk(c), :] = src_ref[natural_rows(c), :]
            return carry

        lax.fori_loop(0, n_chunks, interleave, 0, unroll=2)

    qz_ref, kz_ref, vz_ref = z_ref.at[0], z_ref.at[1], z_ref.at[2]
    row = lax.broadcasted_iota(jnp.int32, (blk, 2 * blk), 0)
    col = lax.broadcasted_iota(jnp.int32, (blk, 2 * blk), 1)
    ones = jnp.ones((grp, 2 * blk, hd), BF16)

    for gi, (window, dil) in enumerate(DILATED_PATTERNS):
        assert window // dil == blk and (dil % il == 0 or il % dil == 0)
        pieces = max(il // dil, 1)
        plen = blk // pieces
        stride = max(dil // il, 1)
        n_blocks = seq // (dil * blk)
        assert n_blocks & (n_blocks - 1) == 0 and (dil * n_blocks) % grp == 0 and plen % SUBLANES == 0
        log_nb = n_blocks.bit_length() - 1
        log_plen = plen.bit_length() - 1

        def step_in_block(i, pieces=pieces, plen=plen, log_plen=log_plen):
            return pieces * jnp.bitwise_and(i, plen - 1) + lax.shift_right_logical(i, log_plen)

        dist = step_in_block(row) - step_in_block(jnp.bitwise_and(col, blk - 1)) + jnp.where(col < blk, blk, 0)
        band = jnp.logical_and(dist >= 0, dist <= blk)
        bias_ref[2 * gi + 1] = jnp.where(band, 0.0, -jnp.inf)
        bias_ref[2 * gi] = jnp.where(jnp.logical_and(band, col >= blk), 0.0, -jnp.inf)

        def block_rows(a, n, dil=dil, pieces=pieces, plen=plen, stride=stride):
            if pieces > 1:
                return [pl.ds(pl.multiple_of((p * dil + a) * part + n * plen, SUBLANES), plen) for p in range(pieces)]
            lo = jnp.bitwise_and(a, il - 1)
            hi = lax.shift_right_logical(a, il.bit_length() - 1)
            start = lo * part + hi + stride * blk * n
            if stride == 1:
                return [pl.ds(pl.multiple_of(start, blk), blk)]
            return [pl.ds(start, blk, stride=stride)]

        def load(ref, parts):
            return jnp.concatenate([ref[d, :] for d in parts], axis=0)

        def store(ref, gi, parts, val, plen=plen):
            for p, d in enumerate(parts):
                ref[gi, d, :] = val[p * plen:(p + 1) * plen, :]

        def blocks_of(it, n_blocks=n_blocks, log_nb=log_nb, block_rows=block_rows):
            out = []
            for gg in range(grp):
                idx = jnp.asarray(it * grp + gg, jnp.int32)
                a = lax.shift_right_logical(idx, log_nb)
                n = jnp.bitwise_and(idx, n_blocks - 1)
                out.append((block_rows(a, n), block_rows(a, jnp.maximum(n - 1, 0)), jnp.minimum(n, 1)))
            return out

        def scores(it, gi=gi, blocks_of=blocks_of, load=load):
            qs, ks, bias = [], [], []
            for c, p, has_prev in blocks_of(it):
                qs.append(load(qz_ref, c))
                ks.append(jnp.concatenate([load(kz_ref, p), load(kz_ref, c)], axis=0))
                bias.append(bias_ref[2 * gi + has_prev])
            q = jnp.stack(qs).astype(BF16)
            k = jnp.stack(ks).astype(BF16)
            return jnp.einsum("gqd,gkd->gqk", q, k, preferred_element_type=F32) + jnp.stack(bias)

        def softmax_pv(it, s, gi=gi, blocks_of=blocks_of, load=load, store=store):
            blocks = blocks_of(it)
            vs = [jnp.concatenate([load(vz_ref, p), load(vz_ref, c)], axis=0) for c, p, _ in blocks]
            v = jnp.concatenate([jnp.stack(vs).astype(BF16), ones], axis=-1)
            mx = jnp.max(s, axis=-1, keepdims=True)
            p = jnp.exp(s - mx).astype(BF16)
            pv = jnp.einsum("gqk,gkd->gqd", p, v, preferred_element_type=F32)
            for gg, (c, _, _) in enumerate(blocks):
                store(acc_ref, gi, c, pv[gg, :, :hd])
                store(den_ref, gi, c, pv[gg, :, hd:])
                store(mx_ref, gi, c, jnp.broadcast_to(mx[gg], (blk, hd)))

        n_steps = dil * n_blocks // grp
        s_ref[0] = scores(0)
        for it in range(1, n_steps):
            s_ref[it & 1] = scores(it)
            softmax_pv(it - 1, s_ref[(it - 1) & 1])
        softmax_pv(n_steps - 1, s_ref[(n_steps - 1) & 1])

    nat_ref = z_ref.at[0]

    def combine(c, carry):
        sl = chunk(c)
        m0 = mx_ref[0, sl, :]
        m1 = mx_ref[1, sl, :]
        m2 = mx_ref[2, sl, :]
        mx = jnp.maximum(jnp.maximum(m0, m1), m2)
        w0 = jnp.exp(m0 - mx)
        w1 = jnp.exp(m1 - mx)
        w2 = jnp.exp(m2 - mx)
        num = w0 * acc_ref[0, sl, :] + w1 * acc_ref[1, sl, :] + w2 * acc_ref[2, sl, :]
        den = w0 * den_ref[0, sl, :] + w1 * den_ref[1, sl, :] + w2 * den_ref[2, sl, :]
        nat_ref[natural_rows(c), :] = num / den
        return carry

    lax.fori_loop(0, n_chunks, combine, 0)

    def emit(c, carry):
        o_ref[chunk(c), :] = nat_ref[chunk(c), :].astype(o_ref.dtype)
        return carry

    lax.fori_loop(0, n_chunks, emit, 0, unroll=2)


def _attn_call(qkv, batch, seq, casts):
    qkv = qkv.reshape(batch, seq, 3 * D_MODEL)
    hd = ATT_HEAD_DIM
    n_groups = len(DILATED_PATTERNS)
    cast_in, cast_out, cast_shapes, cast_blocks = _cast_plan(casts, lambda b, h: b * ATT_HEADS + h)
    out, *cast = pl.pallas_call(
        functools.partial(_attn_kernel, seq=seq, cast_blocks=cast_blocks),
        grid=(batch, ATT_HEADS),
        in_specs=[
            pl.BlockSpec((None, seq, hd), lambda b, h: (b, 0, h)),
            pl.BlockSpec((None, seq, hd), lambda b, h: (b, 0, ATT_HEADS + h)),
            pl.BlockSpec((None, seq, hd), lambda b, h: (b, 0, 2 * ATT_HEADS + h)),
        ] + cast_in,
        out_specs=[pl.BlockSpec((None, seq, hd), lambda b, h: (b, 0, h))] + cast_out,
        out_shape=[jax.ShapeDtypeStruct((batch, seq, D_MODEL), BF16)] + cast_shapes,
        scratch_shapes=[
            pltpu.VMEM((3, seq, hd), F32),
            pltpu.VMEM((n_groups, seq, hd), F32),
            pltpu.VMEM((n_groups, seq, hd), F32),
            pltpu.VMEM((n_groups, seq, hd), F32),
            pltpu.VMEM((2 * n_groups, ATT_BLOCK, 2 * ATT_BLOCK), F32),
            pltpu.VMEM((2, ATT_GROUP, ATT_BLOCK, 2 * ATT_BLOCK), F32),
        ],
        compiler_params=_params(("arbitrary", "arbitrary")),
        name="dilated_attn",
    )(qkv, qkv, qkv, *[c.src for c in casts])
    return out.reshape(batch * seq, D_MODEL), cast


def _proj_kernel(a_ref, w_ref, x_ref, o_ref, *, rows):
    for c in range(o_ref.shape[0] // rows):
        sl = slice(c * rows, (c + 1) * rows)
        o_ref[sl, :] = x_ref[sl, :] + _dot(a_ref[sl, :], w_ref[...])


def _proj_call(a, w, x, *, tm=512, rows=256):
    m, k = a.shape
    n = w.shape[1]
    return pl.pallas_call(
        functools.partial(_proj_kernel, rows=rows),
        grid=(m // tm,),
        in_specs=[
            pl.BlockSpec((tm, k), lambda i: (i, 0)),
            pl.BlockSpec((k, n), lambda i: (0, 0)),
            pl.BlockSpec((tm, n), lambda i: (i, 0)),
        ],
        out_specs=pl.BlockSpec((tm, n), lambda i: (i, 0)),
        out_shape=jax.ShapeDtypeStruct((m, n), F32),
        compiler_params=_params(("parallel",)),
        name="proj_residual",
    )(a, w, x)


def _fill_normed(h_ref, x_ref, xh_ref, g_ref, seq_start, copy_ref=None):
    g = g_ref[...]
    halo = jnp.where(seq_start, 0.0, _rms(xh_ref[...], g))
    h_ref[0:HALO, :] = halo.astype(BF16)
    _norm_rows(h_ref, HALO, x_ref, g, copy_ref)


def _causal_conv(u_ref, c, w_ref, b_ref, taps, rows):
    w = w_ref[...]
    y = b_ref[...]
    for j in range(taps):
        off = c * rows + HALO - (taps - 1) + j
        y = y + w[j:j + 1, :] * u_ref[off:off + rows, :]
    return y


def _ffn_kernel(x_ref, xh_ref, g_ref, wg_ref, wu_ref, cwg_ref, cwu_ref, cbg_ref, cbu_ref, wd_ref,
                o_ref, h_ref, ug_ref, uu_ref, *, rows, tiles_per_seq):
    i = pl.program_id(0)
    j = pl.program_id(1)

    @pl.when(j == 0)
    def _():
        _fill_normed(h_ref, x_ref, xh_ref, g_ref, i % tiles_per_seq == 0, copy_ref=o_ref)

    def down_proj(c):
        gate = _causal_conv(ug_ref, c, cwg_ref, cbg_ref, FFN_CONV, rows)
        up = _causal_conv(uu_ref, c, cwu_ref, cbu_ref, FFN_CONV, rows)
        act = (gate * up / (1.0 + jnp.exp(-gate))).astype(BF16)
        o_ref[c * rows:(c + 1) * rows, :] += _dot(act, wd_ref[...])

    _row_chunks_conv(h_ref, [wg_ref, wu_ref], [ug_ref, uu_ref], rows, down_proj)


def _ffn_call(x, g, w_up, conv_w, conv_b, w_down, layer, seq, *, tm=1024, tf=512, rows=256):
    m = x.shape[0]
    nf = FFN_DIM // tf
    halo_blocks_per_tile = tm // HALO
    return pl.pallas_call(
        functools.partial(_ffn_kernel, rows=rows, tiles_per_seq=seq // tm),
        grid=(m // tm, nf),
        in_specs=[
            pl.BlockSpec((tm, D_MODEL), lambda i, j: (i, 0)),
            pl.BlockSpec((HALO, D_MODEL), lambda i, j: (jnp.maximum(i * halo_blocks_per_tile - 1, 0), 0)),
            pl.BlockSpec((None, 1, D_MODEL), lambda i, j: (layer, 0, 0)),
            pl.BlockSpec((D_MODEL, tf), lambda i, j: (0, j)),
            pl.BlockSpec((D_MODEL, tf), lambda i, j: (0, nf + j)),
            pl.BlockSpec((None, FFN_CONV, tf), lambda i, j: (layer, 0, j)),
            pl.BlockSpec((None, FFN_CONV, tf), lambda i, j: (layer, 0, nf + j)),
            pl.BlockSpec((None, 1, tf), lambda i, j: (layer, 0, j)),
            pl.BlockSpec((None, 1, tf), lambda i, j: (layer, 0, nf + j)),
            pl.BlockSpec((tf, D_MODEL), lambda i, j: (j, 0)),
        ],
        out_specs=pl.BlockSpec((tm, D_MODEL), lambda i, j: (i, 0)),
        out_shape=jax.ShapeDtypeStruct((m, D_MODEL), F32),
        scratch_shapes=[
            pltpu.VMEM((tm + HALO, D_MODEL), BF16),
            pltpu.VMEM((tm + HALO, tf), F32),
            pltpu.VMEM((tm + HALO, tf), F32),
        ],
        compiler_params=_params(("parallel", "arbitrary")),
        name="conv_ffn",
    )(x, x, g, w_up, w_up, conv_w, conv_w, conv_b, conv_b, w_down)


def _lstm_in_kernel(x_ref, xh_ref, g_ref, w_ref, wgate_ref, bgate_ref, cw_ref, cb_ref,
                    z_ref, gates_ref, h_ref, u_ref, *, rows, tiles_per_seq, n_conv_blocks):
    i = pl.program_id(0)
    j = pl.program_id(1)

    @pl.when(j == 0)
    def _():
        _fill_normed(h_ref, x_ref, xh_ref, g_ref, i % tiles_per_seq == 0)
        wg = wgate_ref[...]
        wg = jnp.concatenate([wg, jnp.zeros((LANES - wg.shape[0], wg.shape[1]), F32)], axis=0).astype(BF16)
        gates_ref[...] = _dot_nt(h_ref[HALO:, :], wg) + bgate_ref[...]

    @pl.when(j < n_conv_blocks)
    def _():
        scale = jnp.where(j < n_conv_blocks // 2, 1.0, LSTM_QK_DIM ** -0.5)

        def conv_silu(c):
            y = _causal_conv(u_ref, c, cw_ref, cb_ref, LSTM_CONV, rows)
            z_ref[c * rows:(c + 1) * rows, :] = y * scale / (1.0 + jnp.exp(-y))

        _row_chunks_conv(h_ref, [w_ref], [u_ref], rows, conv_silu, dot=_dot_nt)

    @pl.when(j >= n_conv_blocks)
    def _():
        def copy(c, us):
            z_ref[c * rows:(c + 1) * rows, :] = us[0]

        _row_chunks(h_ref, [w_ref], rows, HALO, False, copy, dot=_dot_nt)


def _lstm_in_call(x, g, w_main_t, w_in_t, b_gate, conv_w, conv_b, seq, *, tm=1024, tn=1024, rows=256):
    m = x.shape[0]
    n = LSTM_MAIN_WIDTH
    n_gates = 2 * LSTM_HEADS
    assert w_in_t.shape[0] == n + n_gates and n % n_gates == 0 and n_gates == SUBLANES
    n_conv_blocks = 2 * LSTM_QK_WIDTH // tn
    halo_blocks_per_tile = tm // HALO
    return pl.pallas_call(
        functools.partial(_lstm_in_kernel, rows=rows, tiles_per_seq=seq // tm, n_conv_blocks=n_conv_blocks),
        grid=(m // tm, n // tn),
        in_specs=[
            pl.BlockSpec((tm, D_MODEL), lambda i, j: (i, 0)),
            pl.BlockSpec((HALO, D_MODEL), lambda i, j: (jnp.maximum(i * halo_blocks_per_tile - 1, 0), 0)),
            pl.BlockSpec((1, D_MODEL), lambda i, j: (0, 0)),
            pl.BlockSpec((tn, D_MODEL), lambda i, j: (j, 0)),
            pl.BlockSpec((n_gates, D_MODEL), lambda i, j: (n // n_gates, 0)),
            pl.BlockSpec((1, LANES), lambda i, j: (0, 0)),
            pl.BlockSpec((LSTM_CONV, tn), lambda i, j: (0, jnp.minimum(j, n_conv_blocks - 1))),
            pl.BlockSpec((1, tn), lambda i, j: (0, jnp.minimum(j, n_conv_blocks - 1))),
        ],
        out_specs=[
            pl.BlockSpec((tm, tn), lambda i, j: (i, j)),
            pl.BlockSpec((tm, LANES), lambda i, j: (i, 0)),
        ],
        out_shape=[
            jax.ShapeDtypeStruct((m, n), F32),
            jax.ShapeDtypeStruct((m, LANES), F32),
        ],
        scratch_shapes=[
            pltpu.VMEM((tm + HALO, D_MODEL), BF16),
            pltpu.VMEM((tm + HALO, tn), F32),
        ],
        compiler_params=_params(("parallel", "arbitrary")),
        name="lstm_in",
    )(x, x, g, w_main_t, w_in_t, b_gate, conv_w, conv_b)


def _log_sigmoid(x):
    return -(jnp.maximum(-x, 0.0) + jnp.log1p(jnp.exp(-jnp.abs(x))))


def _cumsum_rows(x):
    n = x.shape[0]
    row = lax.broadcasted_iota(jnp.int32, x.shape, 0)
    shift = 1
    while shift < n:
        x = x + jnp.where(row >= shift, pltpu.roll(x, shift, axis=0), 0.0)
        shift *= 2
    return x


def _mlstm_kernel(q_ref, k_ref, v_ref, og_ref, gates_ref, hg_ref, o_ref, c_ref, n_ref, m_ref):
    chunk = q_ref.shape[0]
    dk, dv, heads = LSTM_QK_DIM, LSTM_V_DIM, LSTM_HEADS

    @pl.when(pl.program_id(1) == 0)
    def _():
        c_ref[...] = jnp.zeros_like(c_ref)
        n_ref[...] = jnp.zeros_like(n_ref)
        m_ref[...] = jnp.zeros_like(m_ref)

    gates = gates_ref[...]
    cum_f = _cumsum_rows(_log_sigmoid(gates))
    gates_t = gates.T
    cum_f_t = cum_f.T
    row = lax.broadcasted_iota(jnp.int32, (chunk, chunk), 0)
    col = lax.broadcasted_iota(jnp.int32, (chunk, chunk), 1)
    causal = col <= row

    for hd in range(heads):
        qs = slice(hd * dk, (hd + 1) * dk)
        vs = slice(hd * dv, (hd + 1) * dv)
        b_col = cum_f[:, heads + hd:heads + hd + 1]
        b_row = cum_f_t[heads + hd:heads + hd + 1, :]
        li_col = gates[:, hd:hd + 1]
        li_row = gates_t[hd:hd + 1, :]
        m_prev = m_ref[hd]
        c_prev = c_ref[hd]
        n_prev = n_ref[hd]
        q = q_ref[:, qs]
        k = k_ref[:, qs]
        v = v_ref[:, vs]
        qb = q.astype(BF16)
        kb = k.astype(BF16)

        dmat = jnp.where(causal, b_col - b_row + li_row, -jnp.inf)
        g = b_col + m_prev
        m_t = jnp.maximum(g, jnp.max(dmat, axis=-1, keepdims=True))
        p = jnp.exp(dmat - m_t)
        inter = jnp.exp(g - m_t)
        w = p * _dot_nt(qb, kb)
        num = inter * _dot(qb, c_prev.astype(BF16)) + _dot(w.astype(BF16), v.astype(BF16))
        den = inter * jnp.sum(q * n_prev, axis=-1, keepdims=True) + jnp.sum(w, axis=-1, keepdims=True)
        h = num / jnp.maximum(jnp.abs(den), jnp.exp(-m_t))

        b_last = b_col[chunk - 1:chunk, :]
        a = b_last - b_col + li_col
        m_new = jnp.maximum(b_last + m_prev, jnp.max(a, axis=0, keepdims=True))
        decay = jnp.exp(b_last + m_prev - m_new)
        wts = jnp.exp(a - m_new)
        c_ref[hd] = decay * c_prev + _dot_tn(kb, (wts * v).astype(BF16))
        n_ref[hd] = decay * n_prev + jnp.sum(wts * k, axis=0, keepdims=True)
        m_ref[hd] = m_new

        hs = _rms(h, hg_ref[:, vs]) * jax.nn.sigmoid(og_ref[:, vs])
        o_ref[:, vs] = hs.astype(o_ref.dtype)


def _mlstm_call(z, gates, head_gain, batch, seq):
    chunk = LSTM_CHUNK
    z = z.reshape(batch, seq, LSTM_MAIN_WIDTH)
    gates = gates.reshape(batch, seq, LANES)
    qkw = LSTM_QK_WIDTH
    return pl.pallas_call(
        _mlstm_kernel,
        grid=(batch, seq // chunk),
        in_specs=[
            pl.BlockSpec((None, chunk, qkw), lambda b, c: (b, c, 0)),
            pl.BlockSpec((None, chunk, qkw), lambda b, c: (b, c, 1)),
            pl.BlockSpec((None, chunk, D_MODEL), lambda b, c: (b, c, 2 * qkw // D_MODEL)),
            pl.BlockSpec((None, chunk, D_MODEL), lambda b, c: (b, c, 2 * qkw // D_MODEL + 1)),
            pl.BlockSpec((None, chunk, LANES), lambda b, c: (b, c, 0)),
            pl.BlockSpec((1, D_MODEL), lambda b, c: (0, 0)),
        ],
        out_specs=pl.BlockSpec((None, chunk, D_MODEL), lambda b, c: (b, c, 0)),
        out_shape=jax.ShapeDtypeStruct((batch, seq, D_MODEL), BF16),
        scratch_shapes=[
            pltpu.VMEM((LSTM_HEADS, LSTM_QK_DIM, LSTM_V_DIM), F32),
            pltpu.VMEM((LSTM_HEADS, 1, LSTM_QK_DIM), F32),
            pltpu.VMEM((LSTM_HEADS, 1, 1), F32),
        ],
        compiler_params=_params(("parallel", "arbitrary")),
        name="mlstm",
    )(z, z, z, z, gates, head_gain)


def kernel(x, attn_norm, attn_w_qkv, attn_q_gain, attn_k_gain, attn_w_o, lstm_norm, lstm_w_in, lstm_gate_bias,
           lstm_conv_w, lstm_conv_b, lstm_head_gain, lstm_w_out, ffn_norm, ffn_w_up, ffn_conv_w, ffn_conv_b,
           ffn_w_down):
    batch, seq, d = x.shape
    assert d == D_MODEL and ffn_norm.shape[0] == 2 and attn_norm.shape[0] == 1 and lstm_norm.shape[0] == 1
    m = batch * seq
    xf = x.reshape(m, d)

    def ffn(xin, layer, w_up, w_down):
        return _ffn_call(xin, ffn_norm[:, None, :], w_up, ffn_conv_w, ffn_conv_b[:, None, :], w_down, layer, seq)

    w_in_t = jnp.swapaxes(lstm_w_in[0], 0, 1)
    qkv, (w_up0, w_down0, w_o, w_in_main_t) = _qkv_call(
        xf, attn_norm[0][None], attn_w_qkv[0].astype(BF16), attn_q_gain[0][None], attn_k_gain[0][None],
        [_Cast(ffn_w_up, 0, 64), _Cast(ffn_w_down, 0, 128), _Cast(attn_w_o, 0, 64),
         _Cast(w_in_t, None, 128, LSTM_MAIN_WIDTH)])
    att, (w_up1, w_down1) = _attn_call(qkv, batch, seq, [_Cast(ffn_w_up, 1, 64), _Cast(ffn_w_down, 1, 176)])
    xf = _proj_call(att, w_o, xf)
    xf = ffn(xf, 0, w_up0, w_down0)

    n_gates = 2 * LSTM_HEADS
    b_gate = jnp.pad(lstm_gate_bias[0], (0, LANES - n_gates))[None]
    z, gates = _lstm_in_call(xf, lstm_norm[0][None], w_in_main_t, w_in_t, b_gate,
                             lstm_conv_w[0], lstm_conv_b[0][None], seq)
    hs = _mlstm_call(z, gates, lstm_head_gain[0][None], batch, seq)
    xf = _proj_call(hs.reshape(m, d), lstm_w_out[0].astype(BF16), xf)
    xf = ffn(xf, 1, w_up1, w_down1)
    return xf.reshape(batch, seq, d)
```

```python
import functools
from typing import NamedTuple, Optional

import jax
import jax.numpy as jnp
from jax import lax
from jax.experimental import pallas as pl
from jax.experimental.pallas import tpu as pltpu

F32 = jnp.float32
BF16 = jnp.bfloat16

D_MODEL = 2048
ATT_HEADS = 16
ATT_HEAD_DIM = D_MODEL // ATT_HEADS
DILATED_PATTERNS = ((128, 1), (512, 4), (2048, 16))
ATT_BLOCK = 128
ATT_GROUP = 8
LSTM_HEADS = 4
LSTM_V_DIM = D_MODEL // LSTM_HEADS
LSTM_QK_DIM = LSTM_V_DIM // 2
LSTM_QK_WIDTH = LSTM_HEADS * LSTM_QK_DIM
LSTM_MAIN_WIDTH = 2 * LSTM_QK_WIDTH + 2 * D_MODEL
LSTM_CONV = 4
FFN_DIM = ((8 * D_MODEL // 3 + 255) // 256) * 256
FFN_CONV = 3
NORM_EPS = 1e-6

LANES = 128
SUBLANES = 8
BF16_ROWS_PER_VREG = 2 * SUBLANES
HALO = BF16_ROWS_PER_VREG
VMEM_LIMIT = 56 * 1024 * 1024

LSTM_CHUNK = 256

NORM_ROWS = 128


def _rms(x, g):
    ms = jnp.mean(x * x, axis=-1, keepdims=True)
    return x * lax.rsqrt(ms + NORM_EPS) * g


def _norm_rows(h_ref, lead, x_ref, g, copy_ref=None):
    def body(r, carry):
        start = pl.multiple_of(r * NORM_ROWS, NORM_ROWS)
        x = x_ref[pl.ds(start, NORM_ROWS), :]
        h_ref[pl.ds(pl.multiple_of(start + lead, BF16_ROWS_PER_VREG), NORM_ROWS), :] = _rms(x, g).astype(BF16)
        if copy_ref is not None:
            copy_ref[pl.ds(start, NORM_ROWS), :] = x
        return carry

    lax.fori_loop(0, x_ref.shape[0] // NORM_ROWS, body, 0)


def _dot(a, b):
    return jnp.dot(a, b, preferred_element_type=F32)


def _dot_nt(a, b):
    return lax.dot_general(a, b, (((1,), (1,)), ((), ())), preferred_element_type=F32)


def _dot_tn(a, b):
    return lax.dot_general(a, b, (((0,), (0,)), ((), ())), preferred_element_type=F32)


def _params(sem):
    return pltpu.CompilerParams(dimension_semantics=sem, vmem_limit_bytes=VMEM_LIMIT)


class _Cast(NamedTuple):
    src: jax.Array
    layer: Optional[int]
    rows: int
    total_rows: Optional[int] = None


def _cast_plan(casts, step_of):
    in_specs, out_specs, out_shapes, n_blocks = [], [], [], []
    for c in casts:
        r, cols = c.src.shape[-2:]
        r = c.total_rows or r
        assert r % c.rows == 0 and c.rows % BF16_ROWS_PER_VREG == 0
        nb = r // c.rows

        def block(*ids, nb=nb):
            return jnp.minimum(step_of(*ids), nb - 1)

        if c.layer is None:
            in_specs.append(pl.BlockSpec((c.rows, cols), lambda *ids, block=block: (block(*ids), 0)))
        else:
            in_specs.append(pl.BlockSpec((None, c.rows, cols),
                                         lambda *ids, block=block, layer=c.layer: (layer, block(*ids), 0)))
        out_specs.append(pl.BlockSpec((c.rows, cols), lambda *ids, block=block: (block(*ids), 0)))
        out_shapes.append(jax.ShapeDtypeStruct((r, cols), BF16))
        n_blocks.append(nb)
    return in_specs, out_specs, out_shapes, tuple(n_blocks)


def _cast_step(step, src_refs, dst_refs, n_blocks):
    for src, dst, nb in zip(src_refs, dst_refs, n_blocks):
        @pl.when(step < nb)
        def _(src=src, dst=dst):
            for r0 in range(0, src.shape[0], BF16_ROWS_PER_VREG):
                sl = slice(r0, r0 + BF16_ROWS_PER_VREG)
                dst[sl, :] = src[sl, :].astype(BF16)


def _row_chunks(h_ref, w_refs, rows, lead, with_halo, consume, dot=_dot):
    n_chunks = (h_ref.shape[0] - lead) // rows
    assert not with_halo or lead == HALO

    def project(c, tails):
        if with_halo and c == 0:
            h = h_ref[0:rows + lead, :]
            return [dot(h, w[...]) for w in w_refs]
        h = h_ref[c * rows + lead:(c + 1) * rows + lead, :]
        us = [dot(h, w[...]) for w in w_refs]
        if with_halo:
            us = [jnp.concatenate([t, u], axis=0) for t, u in zip(tails, us)]
        return us

    us = project(0, None)
    for c in range(n_chunks):
        nxt = project(c + 1, [u[rows:, :] for u in us]) if c + 1 < n_chunks else None
        consume(c, us)
        us = nxt


def _row_chunks_conv(h_ref, w_refs, u_refs, rows, consume, dot=_dot):
    n_chunks = (h_ref.shape[0] - HALO) // rows

    def project(c):
        lo = 0 if c == 0 else c * rows + HALO
        hi = (c + 1) * rows + HALO
        h = h_ref[lo:hi, :]
        for w, u in zip(w_refs, u_refs):
            u[lo:hi, :] = dot(h, w[...])

    project(0)
    for c in range(n_chunks):
        if c + 1 < n_chunks:
            project(c + 1)
        consume(c)


def _qkv_kernel(*refs, rows, n_head_blocks, heads_per_block, cast_blocks):
    nc = len(cast_blocks)
    x_ref, g_ref, w_ref, qg_ref, kg_ref = refs[:5]
    cast_src, o_ref, cast_dst, h_ref = refs[5:5 + nc], refs[5 + nc], refs[6 + nc:6 + 2 * nc], refs[6 + 2 * nc]
    j = pl.program_id(1)
    _cast_step(pl.program_id(0) * pl.num_programs(1) + j, cast_src, cast_dst, cast_blocks)

    @pl.when(j == 0)
    def _():
        _norm_rows(h_ref, 0, x_ref, g_ref[...])

    @pl.when(j < 2 * n_head_blocks)
    def _():
        gain = jnp.where(j < n_head_blocks, qg_ref[...] * (ATT_HEAD_DIM ** -0.5), kg_ref[...])

        def qk_norm(c, us):
            for hh in range(heads_per_block):
                sl = slice(hh * ATT_HEAD_DIM, (hh + 1) * ATT_HEAD_DIM)
                o_ref[c * rows:(c + 1) * rows, sl] = _rms(us[0][:, sl], gain)

        _row_chunks(h_ref, [w_ref], rows, 0, False, qk_norm)

    @pl.when(j >= 2 * n_head_blocks)
    def _():
        def copy(c, us):
            o_ref[c * rows:(c + 1) * rows, :] = us[0]

        _row_chunks(h_ref, [w_ref], rows, 0, False, copy)


def _qkv_call(x, g, w, qg, kg, casts, *, tm=1024, tn=1024, rows=256):
    m = x.shape[0]
    n = w.shape[1]
    heads_per_block = tn // ATT_HEAD_DIM
    n_head_blocks = D_MODEL // tn
    n_col_blocks = n // tn
    cast_in, cast_out, cast_shapes, cast_blocks = _cast_plan(casts, lambda i, j: i * n_col_blocks + j)
    qkv, *cast = pl.pallas_call(
        functools.partial(_qkv_kernel, rows=rows, n_head_blocks=n_head_blocks, heads_per_block=heads_per_block,
                          cast_blocks=cast_blocks),
        grid=(m // tm, n_col_blocks),
        in_specs=[
            pl.BlockSpec((tm, D_MODEL), lambda i, j: (i, 0)),
            pl.BlockSpec((1, D_MODEL), lambda i, j: (0, 0)),
            pl.BlockSpec((D_MODEL, tn), lambda i, j: (0, j)),
            pl.BlockSpec((1, ATT_HEAD_DIM), lambda i, j: (0, 0)),
            pl.BlockSpec((1, ATT_HEAD_DIM), lambda i, j: (0, 0)),
        ] + cast_in,
        out_specs=[pl.BlockSpec((tm, tn), lambda i, j: (i, j))] + cast_out,
        out_shape=[jax.ShapeDtypeStruct((m, n), F32)] + cast_shapes,
        scratch_shapes=[pltpu.VMEM((tm, D_MODEL), BF16)],
        compiler_params=_params(("arbitrary", "arbitrary")),
        name="attn_qkv",
    )(x, g, w, qg, kg, *[c.src for c in casts])
    return qkv, cast


ROW_INTERLEAVE = 4


def _attn_kernel(*refs, seq, cast_blocks):
    nc = len(cast_blocks)
    q_ref, k_ref, v_ref = refs[:3]
    cast_src, o_ref, cast_dst = refs[3:3 + nc], refs[3 + nc], refs[4 + nc:4 + 2 * nc]
    z_ref, acc_ref, den_ref, mx_ref, bias_ref, s_ref = refs[4 + 2 * nc:]
    _cast_step(pl.program_id(0) * pl.num_programs(1) + pl.program_id(1), cast_src, cast_dst, cast_blocks)
    _attn_body(q_ref, k_ref, v_ref, o_ref, z_ref, acc_ref, den_ref, mx_ref, bias_ref, s_ref, seq=seq)


def _attn_body(q_ref, k_ref, v_ref, o_ref, z_ref, acc_ref, den_ref, mx_ref, bias_ref, s_ref, *, seq):
    blk, hd, grp, il = ATT_BLOCK, ATT_HEAD_DIM, ATT_GROUP, ROW_INTERLEAVE
    part = seq // il
    n_chunks = seq // blk
    assert part % blk == 0

    def chunk(c):
        return pl.ds(pl.multiple_of(c * blk, blk), blk)

    def natural_rows(c):
        lo = c // (part // blk)
        b0 = (c % (part // blk)) * blk
        return pl.ds(lo + il * b0, blk, stride=il)

    for ti, src_ref in enumerate((q_ref, k_ref, v_ref)):
        def interleave(c, carry, ti=ti, src_ref=src_ref):
            z_ref[ti, chunk(c), :] = src_ref[natural_rows(c), :]
            return carry

        lax.fori_loop(0, n_chunks, interleave, 0, unroll=2)

    qz_ref, kz_ref, vz_ref = z_ref.at[0], z_ref.at[1], z_ref.at[2]
    row = lax.broadcasted_iota(jnp.int32, (blk, 2 * blk), 0)
    col = lax.broadcasted_iota(jnp.int32, (blk, 2 * blk), 1)
    ones = jnp.ones((grp, 2 * blk, hd), BF16)

    for gi, (window, dil) in enumerate(DILATED_PATTERNS):
        assert window // dil == blk and (dil % il == 0 or il % dil == 0)
        pieces = max(il // dil, 1)
        plen = blk // pieces
        stride = max(dil // il, 1)
        n_blocks = seq // (dil * blk)
        assert n_blocks & (n_blocks - 1) == 0 and (dil * n_blocks) % grp == 0 and plen % SUBLANES == 0
        log_nb = n_blocks.bit_length() - 1
        log_plen = plen.bit_length() - 1

        def step_in_block(i, pieces=pieces, plen=plen, log_plen=log_plen):
            return pieces * jnp.bitwise_and(i, plen - 1) + lax.shift_right_logical(i, log_plen)

        dist = step_in_block(row) - step_in_block(jnp.bitwise_and(col, blk - 1)) + jnp.where(col < blk, blk, 0)
        band = jnp.logical_and(dist >= 0, dist <= blk)
        bias_ref[2 * gi + 1] = jnp.where(band, 0.0, -jnp.inf)
        bias_ref[2 * gi] = jnp.where(jnp.logical_and(band, col >= blk), 0.0, -jnp.inf)

        def block_rows(a, n, dil=dil, pieces=pieces, plen=plen, stride=stride):
            if pieces > 1:
                return [pl.ds(pl.multiple_of((p * dil + a) * part + n * plen, SUBLANES), plen) for p in range(pieces)]
            lo = jnp.bitwise_and(a, il - 1)
            hi = lax.shift_right_logical(a, il.bit_length() - 1)
            start = lo * part + hi + stride * blk * n
            if stride == 1:
                return [pl.ds(pl.multiple_of(start, blk), blk)]
            return [pl.ds(start, blk, stride=stride)]

        def load(ref, parts):
            return jnp.concatenate([ref[d, :] for d in parts], axis=0)

        def store(ref, gi, parts, val, plen=plen):
            for p, d in enumerate(parts):
                ref[gi, d, :] = val[p * plen:(p + 1) * plen, :]

        def blocks_of(it, n_blocks=n_blocks, log_nb=log_nb, block_rows=block_rows):
            out = []
            for gg in range(grp):
                idx = jnp.asarray(it * grp + gg, jnp.int32)
                a = lax.shift_right_logical(idx, log_nb)
                n = jnp.bitwise_and(idx, n_blocks - 1)
                out.append((block_rows(a, n), block_rows(a, jnp.maximum(n - 1, 0)), jnp.minimum(n, 1)))
            return out

        def scores(it, gi=gi, blocks_of=blocks_of, load=load):
            qs, ks, bias = [], [], []
            for c, p, has_prev in blocks_of(it):
                qs.append(load(qz_ref, c))
                ks.append(jnp.concatenate([load(kz_ref, p), load(kz_ref, c)], axis=0))
                bias.append(bias_ref[2 * gi + has_prev])
            q = jnp.stack(qs).astype(BF16)
            k = jnp.stack(ks).astype(BF16)
            return jnp.einsum("gqd,gkd->gqk", q, k, preferred_element_type=F32) + jnp.stack(bias)

        def softmax_pv(it, s, gi=gi, blocks_of=blocks_of, load=load, store=store):
            blocks = blocks_of(it)
            vs = [jnp.concatenate([load(vz_ref, p), load(vz_ref, c)], axis=0) for c, p, _ in blocks]
            v = jnp.concatenate([jnp.stack(vs).astype(BF16), ones], axis=-1)
            mx = jnp.max(s, axis=-1, keepdims=True)
            p = jnp.exp(s - mx).astype(BF16)
            pv = jnp.einsum("gqk,gkd->gqd", p, v, preferred_element_type=F32)
            for gg, (c, _, _) in enumerate(blocks):
                store(acc_ref, gi, c, pv[gg, :, :hd])
                store(den_ref, gi, c, pv[gg, :, hd:])
                store(mx_ref, gi, c, jnp.broadcast_to(mx[gg], (blk, hd)))

        n_steps = dil * n_blocks // grp
        s_ref[0] = scores(0)
        for it in range(1, n_steps):
            s_ref[it & 1] = scores(it)
            softmax_pv(it - 1, s_ref[(it - 1) & 1])
        softmax_pv(n_steps - 1, s_ref[(n_steps - 1) & 1])

    nat_ref = z_ref.at[0]

    def combine(c, carry):
        sl = chunk(c)
        m0 = mx_ref[0, sl, :]
        m1 = mx_ref[1, sl, :]
        m2 = mx_ref[2, sl, :]
        mx = jnp.maximum(jnp.maximum(m0, m1), m2)
        w0 = jnp.exp(m0 - mx)
        w1 = jnp.exp(m1 - mx)
        w2 = jnp.exp(m2 - mx)
        num = w0 * acc_ref[0, sl, :] + w1 * acc_ref[1, sl, :] + w2 * acc_ref[2, sl, :]
        den = w0 * den_ref[0, sl, :] + w1 * den_ref[1, sl, :] + w2 * den_ref[2, sl, :]
        nat_ref[natural_rows(c), :] = num / den
        return carry

    lax.fori_loop(0, n_chunks, combine, 0)

    def emit(c, carry):
        o_ref[chunk(c), :] = nat_ref[chunk(c), :].astype(o_ref.dtype)
        return carry

    lax.fori_loop(0, n_chunks, emit, 0, unroll=2)


def _attn_call(qkv, batch, seq, casts):
    qkv = qkv.reshape(batch, seq, 3 * D_MODEL)
    hd = ATT_HEAD_DIM
    n_groups = len(DILATED_PATTERNS)
    cast_in, cast_out, cast_shapes, cast_blocks = _cast_plan(casts, lambda b, h: b * ATT_HEADS + h)
    out, *cast = pl.pallas_call(
        functools.partial(_attn_kernel, seq=seq, cast_blocks=cast_blocks),
        grid=(batch, ATT_HEADS),
        in_specs=[
            pl.BlockSpec((None, seq, hd), lambda b, h: (b, 0, h)),
            pl.BlockSpec((None, seq, hd), lambda b, h: (b, 0, ATT_HEADS + h)),
            pl.BlockSpec((None, seq, hd), lambda b, h: (b, 0, 2 * ATT_HEADS + h)),
        ] + cast_in,
        out_specs=[pl.BlockSpec((None, seq, hd), lambda b, h: (b, 0, h))] + cast_out,
        out_shape=[jax.ShapeDtypeStruct((batch, seq, D_MODEL), BF16)] + cast_shapes,
        scratch_shapes=[
            pltpu.VMEM((3, seq, hd), F32),
            pltpu.VMEM((n_groups, seq, hd), F32),
            pltpu.VMEM((n_groups, seq, hd), F32),
            pltpu.VMEM((n_groups, seq, hd), F32),
            pltpu.VMEM((2 * n_groups, ATT_BLOCK, 2 * ATT_BLOCK), F32),
            pltpu.VMEM((2, ATT_GROUP, ATT_BLOCK, 2 * ATT_BLOCK), F32),
        ],
        compiler_params=_params(("arbitrary", "arbitrary")),
        name="dilated_attn",
    )(qkv, qkv, qkv, *[c.src for c in casts])
    return out.reshape(batch * seq, D_MODEL), cast


def _proj_kernel(a_ref, w_ref, x_ref, o_ref, *, rows):
    for c in range(o_ref.shape[0] // rows):
        sl = slice(c * rows, (c + 1) * rows)
        o_ref[sl, :] = x_ref[sl, :] + _dot(a_ref[sl, :], w_ref[...])


def _proj_call(a, w, x, *, tm=512, rows=256):
    m, k = a.shape
    n = w.shape[1]
    return pl.pallas_call(
        functools.partial(_proj_kernel, rows=rows),
        grid=(m // tm,),
        in_specs=[
            pl.BlockSpec((tm, k), lambda i: (i, 0)),
            pl.BlockSpec((k, n), lambda i: (0, 0)),
            pl.BlockSpec((tm, n), lambda i: (i, 0)),
        ],
        out_specs=pl.BlockSpec((tm, n), lambda i: (i, 0)),
        out_shape=jax.ShapeDtypeStruct((m, n), F32),
        compiler_params=_params(("parallel",)),
        name="proj_residual",
    )(a, w, x)


def _fill_normed(h_ref, x_ref, xh_ref, g_ref, seq_start, copy_ref=None):
    g = g_ref[...]
    halo = jnp.where(seq_start, 0.0, _rms(xh_ref[...], g))
    h_ref[0:HALO, :] = halo.astype(BF16)
    _norm_rows(h_ref, HALO, x_ref, g, copy_ref)


def _causal_conv(u_ref, c, w_ref, b_ref, taps, rows):
    w = w_ref[...]
    y = b_ref[...]
    for j in range(taps):
        off = c * rows + HALO - (taps - 1) + j
        y = y + w[j:j + 1, :] * u_ref[off:off + rows, :]
    return y


def _ffn_kernel(x_ref, xh_ref, g_ref, wg_ref, wu_ref, cwg_ref, cwu_ref, cbg_ref, cbu_ref, wd_ref,
                o_ref, h_ref, ug_ref, uu_ref, *, rows, tiles_per_seq):
    i = pl.program_id(0)
    j = pl.program_id(1)

    @pl.when(j == 0)
    def _():
        _fill_normed(h_ref, x_ref, xh_ref, g_ref, i % tiles_per_seq == 0, copy_ref=o_ref)

    def down_proj(c):
        gate = _causal_conv(ug_ref, c, cwg_ref, cbg_ref, FFN_CONV, rows)
        up = _causal_conv(uu_ref, c, cwu_ref, cbu_ref, FFN_CONV, rows)
        act = (gate * up / (1.0 + jnp.exp(-gate))).astype(BF16)
        o_ref[c * rows:(c + 1) * rows, :] += _dot(act, wd_ref[...])

    _row_chunks_conv(h_ref, [wg_ref, wu_ref], [ug_ref, uu_ref], rows, down_proj)


def _ffn_call(x, g, w_up, conv_w, conv_b, w_down, layer, seq, *, tm=1024, tf=512, rows=256):
    m = x.shape[0]
    nf = FFN_DIM // tf
    halo_blocks_per_tile = tm // HALO
    return pl.pallas_call(
        functools.partial(_ffn_kernel, rows=rows, tiles_per_seq=seq // tm),
        grid=(m // tm, nf),
        in_specs=[
            pl.BlockSpec((tm, D_MODEL), lambda i, j: (i, 0)),
            pl.BlockSpec((HALO, D_MODEL), lambda i, j: (jnp.maximum(i * halo_blocks_per_tile - 1, 0), 0)),
            pl.BlockSpec((None, 1, D_MODEL), lambda i, j: (layer, 0, 0)),
            pl.BlockSpec((D_MODEL, tf), lambda i, j: (0, j)),
            pl.BlockSpec((D_MODEL, tf), lambda i, j: (0, nf + j)),
            pl.BlockSpec((None, FFN_CONV, tf), lambda i, j: (layer, 0, j)),
            pl.BlockSpec((None, FFN_CONV, tf), lambda i, j: (layer, 0, nf + j)),
            pl.BlockSpec((None, 1, tf), lambda i, j: (layer, 0, j)),
            pl.BlockSpec((None, 1, tf), lambda i, j: (layer, 0, nf + j)),
            pl.BlockSpec((tf, D_MODEL), lambda i, j: (j, 0)),
        ],
        out_specs=pl.BlockSpec((tm, D_MODEL), lambda i, j: (i, 0)),
        out_shape=jax.ShapeDtypeStruct((m, D_MODEL), F32),
        scratch_shapes=[
            pltpu.VMEM((tm + HALO, D_MODEL), BF16),
            pltpu.VMEM((tm + HALO, tf), F32),
            pltpu.VMEM((tm + HALO, tf), F32),
        ],
        compiler_params=_params(("parallel", "arbitrary")),
        name="conv_ffn",
    )(x, x, g, w_up, w_up, conv_w, conv_w, conv_b, conv_b, w_down)


def _lstm_in_kernel(x_ref, xh_ref, g_ref, w_ref, wgate_ref, bgate_ref, cw_ref, cb_ref,
                    z_ref, gates_ref, h_ref, u_ref, *, rows, tiles_per_seq, n_conv_blocks):
    i = pl.program_id(0)
    j = pl.program_id(1)

    @pl.when(j == 0)
    def _():
        _fill_normed(h_ref, x_ref, xh_ref, g_ref, i % tiles_per_seq == 0)
        wg = wgate_ref[...]
        wg = jnp.concatenate([wg, jnp.zeros((LANES - wg.shape[0], wg.shape[1]), F32)], axis=0).astype(BF16)
        gates_ref[...] = _dot_nt(h_ref[HALO:, :], wg) + bgate_ref[...]

    @pl.when(j < n_conv_blocks)
    def _():
        scale = jnp.where(j < n_conv_blocks // 2, 1.0, LSTM_QK_DIM ** -0.5)

        def conv_silu(c):
            y = _causal_conv(u_ref, c, cw_ref, cb_ref, LSTM_CONV, rows)
            z_ref[c * rows:(c + 1) * rows, :] = y * scale / (1.0 + jnp.exp(-y))

        _row_chunks_conv(h_ref, [w_ref], [u_ref], rows, conv_silu, dot=_dot_nt)

    @pl.when(j >= n_conv_blocks)
    def _():
        def copy(c, us):
            z_ref[c * rows:(c + 1) * rows, :] = us[0]

        _row_chunks(h_ref, [w_ref], rows, HALO, False, copy, dot=_dot_nt)


def _lstm_in_call(x, g, w_main_t, w_in_t, b_gate, conv_w, conv_b, seq, *, tm=1024, tn=1024, rows=256):
    m = x.shape[0]
    n = LSTM_MAIN_WIDTH
    n_gates = 2 * LSTM_HEADS
    assert w_in_t.shape[0] == n + n_gates and n % n_gates == 0 and n_gates == SUBLANES
    n_conv_blocks = 2 * LSTM_QK_WIDTH // tn
    halo_blocks_per_tile = tm // HALO
    return pl.pallas_call(
        functools.partial(_lstm_in_kernel, rows=rows, tiles_per_seq=seq // tm, n_conv_blocks=n_conv_blocks),
        grid=(m // tm, n // tn),
        in_specs=[
            pl.BlockSpec((tm, D_MODEL), lambda i, j: (i, 0)),
            pl.BlockSpec((HALO, D_MODEL), lambda i, j: (jnp.maximum(i * halo_blocks_per_tile - 1, 0), 0)),
            pl.BlockSpec((1, D_MODEL), lambda i, j: (0, 0)),
            pl.BlockSpec((tn, D_MODEL), lambda i, j: (j, 0)),
            pl.BlockSpec((n_gates, D_MODEL), lambda i, j: (n // n_gates, 0)),
            pl.BlockSpec((1, LANES), lambda i, j: (0, 0)),
            pl.BlockSpec((LSTM_CONV, tn), lambda i, j: (0, jnp.minimum(j, n_conv_blocks - 1))),
            pl.BlockSpec((1, tn), lambda i, j: (0, jnp.minimum(j, n_conv_blocks - 1))),
        ],
        out_specs=[
            pl.BlockSpec((tm, tn), lambda i, j: (i, j)),
            pl.BlockSpec((tm, LANES), lambda i, j: (i, 0)),
        ],
        out_shape=[
            jax.ShapeDtypeStruct((m, n), F32),
            jax.ShapeDtypeStruct((m, LANES), F32),
        ],
        scratch_shapes=[
            pltpu.VMEM((tm + HALO, D_MODEL), BF16),
            pltpu.VMEM((tm + HALO, tn), F32),
        ],
        compiler_params=_params(("parallel", "arbitrary")),
        name="lstm_in",
    )(x, x, g, w_main_t, w_in_t, b_gate, conv_w, conv_b)


def _log_sigmoid(x):
    return -(jnp.maximum(-x, 0.0) + jnp.log1p(jnp.exp(-jnp.abs(x))))


def _cumsum_rows(x):
    n = x.shape[0]
    row = lax.broadcasted_iota(jnp.int32, x.shape, 0)
    shift = 1
    while shift < n:
        x = x + jnp.where(row >= shift, pltpu.roll(x, shift, axis=0), 0.0)
        shift *= 2
    return x


def _mlstm_kernel(q_ref, k_ref, v_ref, og_ref, gates_ref, hg_ref, wo_ref, x_ref, o_ref, c_ref, n_ref, m_ref, hs_ref):
    chunk = q_ref.shape[0]
    dk, dv, heads = LSTM_QK_DIM, LSTM_V_DIM, LSTM_HEADS

    @pl.when(pl.program_id(1) == 0)
    def _():
        c_ref[...] = jnp.zeros_like(c_ref)
        n_ref[...] = jnp.zeros_like(n_ref)
        m_ref[...] = jnp.zeros_like(m_ref)
        hs_ref[...] = jnp.zeros_like(hs_ref)

    o_ref[...] = x_ref[...] + _dot(hs_ref[...], wo_ref[...])

    gates = gates_ref[...]
    cum_f = _cumsum_rows(_log_sigmoid(gates))
    gates_t = gates.T
    cum_f_t = cum_f.T
    row = lax.broadcasted_iota(jnp.int32, (chunk, chunk), 0)
    col = lax.broadcasted_iota(jnp.int32, (chunk, chunk), 1)
    causal = col <= row

    for hd in range(heads):
        qs = slice(hd * dk, (hd + 1) * dk)
        vs = slice(hd * dv, (hd + 1) * dv)
        b_col = cum_f[:, heads + hd:heads + hd + 1]
        b_row = cum_f_t[heads + hd:heads + hd + 1, :]
        li_col = gates[:, hd:hd + 1]
        li_row = gates_t[hd:hd + 1, :]
        m_prev = m_ref[hd]
        c_prev = c_ref[hd]
        n_prev = n_ref[hd]
        q = q_ref[:, qs]
        k = k_ref[:, qs]
        v = v_ref[:, vs]
        qb = q.astype(BF16)
        kb = k.astype(BF16)

        dmat = jnp.where(causal, b_col - b_row + li_row, -jnp.inf)
        g = b_col + m_prev
        m_t = jnp.maximum(g, jnp.max(dmat, axis=-1, keepdims=True))
        p = jnp.exp(dmat - m_t)
        inter = jnp.exp(g - m_t)
        w = p * _dot_nt(qb, kb)
        num = inter * _dot(qb, c_prev.astype(BF16)) + _dot(w.astype(BF16), v.astype(BF16))
        den = inter * jnp.sum(q * n_prev, axis=-1, keepdims=True) + jnp.sum(w, axis=-1, keepdims=True)
        h = num / jnp.maximum(jnp.abs(den), jnp.exp(-m_t))

        b_last = b_col[chunk - 1:chunk, :]
        a = b_last - b_col + li_col
        m_new = jnp.maximum(b_last + m_prev, jnp.max(a, axis=0, keepdims=True))
        decay = jnp.exp(b_last + m_prev - m_new)
        wts = jnp.exp(a - m_new)
        c_ref[hd] = decay * c_prev + _dot_tn(kb, (wts * v).astype(BF16))
        n_ref[hd] = decay * n_prev + jnp.sum(wts * k, axis=0, keepdims=True)
        m_ref[hd] = m_new

        hs = _rms(h, hg_ref[:, vs]) * jax.nn.sigmoid(og_ref[:, vs])
        hs_ref[:, vs] = hs.astype(hs_ref.dtype)


def _mlstm_call(z, gates, head_gain, w_out, x, batch, seq):
    chunk = LSTM_CHUNK
    nc = seq // chunk
    z = z.reshape(batch, seq, LSTM_MAIN_WIDTH)
    gates = gates.reshape(batch, seq, LANES)
    qkw = LSTM_QK_WIDTH

    def cur(c):
        return jnp.minimum(c, nc - 1)

    def lagged(b, c):
        return b * nc + jnp.maximum(c - 1, 0)

    return pl.pallas_call(
        _mlstm_kernel,
        grid=(batch, nc + 1),
        in_specs=[
            pl.BlockSpec((None, chunk, qkw), lambda b, c: (b, cur(c), 0)),
            pl.BlockSpec((None, chunk, qkw), lambda b, c: (b, cur(c), 1)),
            pl.BlockSpec((None, chunk, D_MODEL), lambda b, c: (b, cur(c), 2 * qkw // D_MODEL)),
            pl.BlockSpec((None, chunk, D_MODEL), lambda b, c: (b, cur(c), 2 * qkw // D_MODEL + 1)),
            pl.BlockSpec((None, chunk, LANES), lambda b, c: (b, cur(c), 0)),
            pl.BlockSpec((1, D_MODEL), lambda b, c: (0, 0)),
            pl.BlockSpec((D_MODEL, D_MODEL), lambda b, c: (0, 0), pipeline_mode=pl.Buffered(1)),
            pl.BlockSpec((chunk, D_MODEL), lambda b, c: (lagged(b, c), 0)),
        ],
        out_specs=pl.BlockSpec((chunk, D_MODEL), lambda b, c: (lagged(b, c), 0)),
        out_shape=jax.ShapeDtypeStruct((batch * seq, D_MODEL), F32),
        scratch_shapes=[
            pltpu.VMEM((LSTM_HEADS, LSTM_QK_DIM, LSTM_V_DIM), F32),
            pltpu.VMEM((LSTM_HEADS, 1, LSTM_QK_DIM), F32),
            pltpu.VMEM((LSTM_HEADS, 1, 1), F32),
            pltpu.VMEM((chunk, D_MODEL), BF16),
        ],
        compiler_params=_params(("arbitrary", "arbitrary")),
        name="mlstm_proj",
    )(z, z, z, z, gates, head_gain, w_out, x)


def kernel(x, attn_norm, attn_w_qkv, attn_q_gain, attn_k_gain, attn_w_o, lstm_norm, lstm_w_in, lstm_gate_bias,
           lstm_conv_w, lstm_conv_b, lstm_head_gain, lstm_w_out, ffn_norm, ffn_w_up, ffn_conv_w, ffn_conv_b,
           ffn_w_down):
    batch, seq, d = x.shape
    assert d == D_MODEL and ffn_norm.shape[0] == 2 and attn_norm.shape[0] == 1 and lstm_norm.shape[0] == 1
    m = batch * seq
    xf = x.reshape(m, d)

    def ffn(xin, layer, w_up, w_down):
        return _ffn_call(xin, ffn_norm[:, None, :], w_up, ffn_conv_w, ffn_conv_b[:, None, :], w_down, layer, seq)

    w_in_t = jnp.swapaxes(lstm_w_in[0], 0, 1)
    qkv, (w_up0, w_down0, w_o, w_in_main_t) = _qkv_call(
        xf, attn_norm[0][None], attn_w_qkv[0].astype(BF16), attn_q_gain[0][None], attn_k_gain[0][None],
        [_Cast(ffn_w_up, 0, 64), _Cast(ffn_w_down, 0, 128), _Cast(attn_w_o, 0, 64),
         _Cast(w_in_t, None, 128, LSTM_MAIN_WIDTH)])
    att, (w_up1, w_down1) = _attn_call(qkv, batch, seq, [_Cast(ffn_w_up, 1, 64), _Cast(ffn_w_down, 1, 176)])
    xf = _proj_call(att, w_o, xf)
    xf = ffn(xf, 0, w_up0, w_down0)

    n_gates = 2 * LSTM_HEADS
    b_gate = jnp.pad(lstm_gate_bias[0], (0, LANES - n_gates))[None]
    z, gates = _lstm_in_call(xf, lstm_norm[0][None], w_in_main_t, w_in_t, b_gate,
                             lstm_conv_w[0], lstm_conv_b[0][None], seq)
    xf = _mlstm_call(z, gates, lstm_head_gain[0][None], lstm_w_out[0].astype(BF16), xf, batch, seq)
    xf = ffn(xf, 1, w_up1, w_down1)
    return xf.reshape(batch, seq, d)
```

```python
import functools
from typing import NamedTuple, Optional

import jax
import jax.numpy as jnp
from jax import lax
from jax.experimental import pallas as pl
from jax.experimental.pallas import tpu as pltpu

F32 = jnp.float32
BF16 = jnp.bfloat16

D_MODEL = 2048
ATT_HEADS = 16
ATT_HEAD_DIM = D_MODEL // ATT_HEADS
DILATED_PATTERNS = ((128, 1), (512, 4), (2048, 16))
ATT_BLOCK = 128
ATT_GROUP = 8
LSTM_HEADS = 4
LSTM_V_DIM = D_MODEL // LSTM_HEADS
LSTM_QK_DIM = LSTM_V_DIM // 2
LSTM_QK_WIDTH = LSTM_HEADS * LSTM_QK_DIM
LSTM_MAIN_WIDTH = 2 * LSTM_QK_WIDTH + 2 * D_MODEL
LSTM_CONV = 4
FFN_DIM = ((8 * D_MODEL // 3 + 255) // 256) * 256
FFN_CONV = 3
NORM_EPS = 1e-6

LANES = 128
SUBLANES = 8
BF16_ROWS_PER_VREG = 2 * SUBLANES
HALO = BF16_ROWS_PER_VREG
VMEM_LIMIT = 56 * 1024 * 1024

LSTM_CHUNK = 256

NORM_ROWS = 128


def _rms(x, g):
    ms = jnp.mean(x * x, axis=-1, keepdims=True)
    return x * lax.rsqrt(ms + NORM_EPS) * g


def _norm_rows(h_ref, lead, x_ref, g, copy_ref=None):
    def body(r, carry):
        start = pl.multiple_of(r * NORM_ROWS, NORM_ROWS)
        x = x_ref[pl.ds(start, NORM_ROWS), :]
        h_ref[pl.ds(pl.multiple_of(start + lead, BF16_ROWS_PER_VREG), NORM_ROWS), :] = _rms(x, g).astype(BF16)
        if copy_ref is not None:
            copy_ref[pl.ds(start, NORM_ROWS), :] = x
        return carry

    lax.fori_loop(0, x_ref.shape[0] // NORM_ROWS, body, 0)


def _dot(a, b):
    return jnp.dot(a, b, preferred_element_type=F32)


def _dot_nt(a, b):
    return lax.dot_general(a, b, (((1,), (1,)), ((), ())), preferred_element_type=F32)


def _dot_tn(a, b):
    return lax.dot_general(a, b, (((0,), (0,)), ((), ())), preferred_element_type=F32)


def _params(sem):
    return pltpu.CompilerParams(dimension_semantics=sem, vmem_limit_bytes=VMEM_LIMIT)


class _Cast(NamedTuple):
    src: jax.Array
    layer: Optional[int]
    rows: int
    total_rows: Optional[int] = None


def _cast_plan(casts, step_of):
    in_specs, out_specs, out_shapes, n_blocks = [], [], [], []
    for c in casts:
        r, cols = c.src.shape[-2:]
        r = c.total_rows or r
        assert r % c.rows == 0 and c.rows % BF16_ROWS_PER_VREG == 0
        nb = r // c.rows

        def block(*ids, nb=nb):
            return jnp.minimum(step_of(*ids), nb - 1)

        if c.layer is None:
            in_specs.append(pl.BlockSpec((c.rows, cols), lambda *ids, block=block: (block(*ids), 0)))
        else:
            in_specs.append(pl.BlockSpec((None, c.rows, cols),
                                         lambda *ids, block=block, layer=c.layer: (layer, block(*ids), 0)))
        out_specs.append(pl.BlockSpec((c.rows, cols), lambda *ids, block=block: (block(*ids), 0)))
        out_shapes.append(jax.ShapeDtypeStruct((r, cols), BF16))
        n_blocks.append(nb)
    return in_specs, out_specs, out_shapes, tuple(n_blocks)


def _cast_step(step, src_refs, dst_refs, n_blocks):
    for src, dst, nb in zip(src_refs, dst_refs, n_blocks):
        @pl.when(step < nb)
        def _(src=src, dst=dst):
            for r0 in range(0, src.shape[0], BF16_ROWS_PER_VREG):
                sl = slice(r0, r0 + BF16_ROWS_PER_VREG)
                dst[sl, :] = src[sl, :].astype(BF16)


def _cast_part(src_refs, dst_refs, part, n_parts):
    for src, dst in zip(src_refs, dst_refs):
        rows = src.shape[0] // n_parts
        assert rows % BF16_ROWS_PER_VREG == 0
        for r0 in range(part * rows, (part + 1) * rows, BF16_ROWS_PER_VREG):
            sl = slice(r0, r0 + BF16_ROWS_PER_VREG)
            dst[sl, :] = src[sl, :].astype(BF16)


def _row_chunks(h_ref, w_refs, rows, lead, with_halo, consume, dot=_dot):
    n_chunks = (h_ref.shape[0] - lead) // rows
    assert not with_halo or lead == HALO

    def project(c, tails):
        if with_halo and c == 0:
            h = h_ref[0:rows + lead, :]
            return [dot(h, w[...]) for w in w_refs]
        h = h_ref[c * rows + lead:(c + 1) * rows + lead, :]
        us = [dot(h, w[...]) for w in w_refs]
        if with_halo:
            us = [jnp.concatenate([t, u], axis=0) for t, u in zip(tails, us)]
        return us

    us = project(0, None)
    for c in range(n_chunks):
        nxt = project(c + 1, [u[rows:, :] for u in us]) if c + 1 < n_chunks else None
        consume(c, us)
        us = nxt


def _row_chunks_conv(h_ref, w_refs, u_refs, rows, consume, dot=_dot):
    n_chunks = (h_ref.shape[0] - HALO) // rows

    def project(c):
        lo = 0 if c == 0 else c * rows + HALO
        hi = (c + 1) * rows + HALO
        h = h_ref[lo:hi, :]
        for w, u in zip(w_refs, u_refs):
            u[lo:hi, :] = dot(h, w[...])

    project(0)
    for c in range(n_chunks):
        if c + 1 < n_chunks:
            project(c + 1)
        consume(c)


def _qkv_kernel(*refs, rows, n_head_blocks, heads_per_block, cast_blocks):
    nc = len(cast_blocks)
    x_ref, g_ref, w_ref, qg_ref, kg_ref = refs[:5]
    cast_src, o_ref, cast_dst, h_ref = refs[5:5 + nc], refs[5 + nc], refs[6 + nc:6 + 2 * nc], refs[6 + 2 * nc]
    j = pl.program_id(1)
    n_chunks = h_ref.shape[0] // rows

    @pl.when(j == 0)
    def _():
        _norm_rows(h_ref, 0, x_ref, g_ref[...])

    @pl.when(j < 2 * n_head_blocks)
    def _():
        gain = jnp.where(j < n_head_blocks, qg_ref[...] * (ATT_HEAD_DIM ** -0.5), kg_ref[...])

        def qk_norm(c, us):
            _cast_part(cast_src, cast_dst, c, n_chunks)
            for hh in range(heads_per_block):
                sl = slice(hh * ATT_HEAD_DIM, (hh + 1) * ATT_HEAD_DIM)
                o_ref[c * rows:(c + 1) * rows, sl] = _rms(us[0][:, sl], gain)

        _row_chunks(h_ref, [w_ref], rows, 0, False, qk_norm)

    @pl.when(j >= 2 * n_head_blocks)
    def _():
        def copy(c, us):
            _cast_part(cast_src, cast_dst, c, n_chunks)
            o_ref[c * rows:(c + 1) * rows, :] = us[0]

        _row_chunks(h_ref, [w_ref], rows, 0, False, copy)


def _qkv_call(x, g, w, qg, kg, casts, *, tm=1024, tn=1024, rows=256):
    m = x.shape[0]
    n = w.shape[1]
    heads_per_block = tn // ATT_HEAD_DIM
    n_head_blocks = D_MODEL // tn
    n_col_blocks = n // tn
    cast_in, cast_out, cast_shapes, cast_blocks = _cast_plan(casts, lambda i, j: i * n_col_blocks + j)
    qkv, *cast = pl.pallas_call(
        functools.partial(_qkv_kernel, rows=rows, n_head_blocks=n_head_blocks, heads_per_block=heads_per_block,
                          cast_blocks=cast_blocks),
        grid=(m // tm, n_col_blocks),
        in_specs=[
            pl.BlockSpec((tm, D_MODEL), lambda i, j: (i, 0)),
            pl.BlockSpec((1, D_MODEL), lambda i, j: (0, 0)),
            pl.BlockSpec((D_MODEL, tn), lambda i, j: (0, j)),
            pl.BlockSpec((1, ATT_HEAD_DIM), lambda i, j: (0, 0)),
            pl.BlockSpec((1, ATT_HEAD_DIM), lambda i, j: (0, 0)),
        ] + cast_in,
        out_specs=[pl.BlockSpec((tm, tn), lambda i, j: (i, j))] + cast_out,
        out_shape=[jax.ShapeDtypeStruct((m, n), F32)] + cast_shapes,
        scratch_shapes=[pltpu.VMEM((tm, D_MODEL), BF16)],
        compiler_params=_params(("arbitrary", "arbitrary")),
        name="attn_qkv",
    )(x, g, w, qg, kg, *[c.src for c in casts])
    return qkv, cast


ROW_INTERLEAVE = 4


def _attn_kernel(*refs, seq, cast_blocks):
    nc = len(cast_blocks)
    q_ref, k_ref, v_ref = refs[:3]
    cast_src, o_ref, cast_dst = refs[3:3 + nc], refs[3 + nc], refs[4 + nc:4 + 2 * nc]
    z_ref, acc_ref, den_ref, mx_ref, bias_ref, s_ref = refs[4 + 2 * nc:]
    _cast_step(pl.program_id(0) * pl.num_programs(1) + pl.program_id(1), cast_src, cast_dst, cast_blocks)
    _attn_body(q_ref, k_ref, v_ref, o_ref, z_ref, acc_ref, den_ref, mx_ref, bias_ref, s_ref, seq=seq)


def _attn_body(q_ref, k_ref, v_ref, o_ref, z_ref, acc_ref, den_ref, mx_ref, bias_ref, s_ref, *, seq):
    blk, hd, grp, il = ATT_BLOCK, ATT_HEAD_DIM, ATT_GROUP, ROW_INTERLEAVE
    part = seq // il
    n_chunks = seq // blk
    assert part % blk == 0

    def chunk(c):
        return pl.ds(pl.multiple_of(c * blk, blk), blk)

    def natural_rows(c):
        lo = c // (part // blk)
        b0 = (c % (part // blk)) * blk
        return pl.ds(lo + il * b0, blk, stride=il)

    for ti, src_ref in enumerate((q_ref, k_ref, v_ref)):
        for c in range(n_chunks):
            z_ref[ti, c * blk:(c + 1) * blk, :] = src_ref[natural_rows(c), :]

    qz_ref, kz_ref, vz_ref = z_ref.at[0], z_ref.at[1], z_ref.at[2]
    row = lax.broadcasted_iota(jnp.int32, (blk, 2 * blk), 0)
    col = lax.broadcasted_iota(jnp.int32, (blk, 2 * blk), 1)
    ones = jnp.ones((grp, 2 * blk, hd), BF16)

    for gi, (window, dil) in enumerate(DILATED_PATTERNS):
        assert window // dil == blk and (dil % il == 0 or il % dil == 0)
        pieces = max(il // dil, 1)
        plen = blk // pieces
        stride = max(dil // il, 1)
        n_blocks = seq // (dil * blk)
        assert n_blocks & (n_blocks - 1) == 0 and (dil * n_blocks) % grp == 0 and plen % SUBLANES == 0
        log_nb = n_blocks.bit_length() - 1
        log_plen = plen.bit_length() - 1

        def step_in_block(i, pieces=pieces, plen=plen, log_plen=log_plen):
            return pieces * jnp.bitwise_and(i, plen - 1) + lax.shift_right_logical(i, log_plen)

        dist = step_in_block(row) - step_in_block(jnp.bitwise_and(col, blk - 1)) + jnp.where(col < blk, blk, 0)
        band = jnp.logical_and(dist >= 0, dist <= blk)
        bias_ref[2 * gi + 1] = jnp.where(band, 0.0, -jnp.inf)
        bias_ref[2 * gi] = jnp.where(jnp.logical_and(band, col >= blk), 0.0, -jnp.inf)

        def block_rows(a, n, dil=dil, pieces=pieces, plen=plen, stride=stride):
            if pieces > 1:
                return [pl.ds(pl.multiple_of((p * dil + a) * part + n * plen, SUBLANES), plen) for p in range(pieces)]
            lo = jnp.bitwise_and(a, il - 1)
            hi = lax.shift_right_logical(a, il.bit_length() - 1)
            start = lo * part + hi + stride * blk * n
            if stride == 1:
                return [pl.ds(pl.multiple_of(start, blk), blk)]
            return [pl.ds(start, blk, stride=stride)]

        def load(ref, parts):
            return jnp.concatenate([ref[d, :] for d in parts], axis=0)

        def store(ref, gi, parts, val, plen=plen):
            for p, d in enumerate(parts):
                ref[gi, d, :] = val[p * plen:(p + 1) * plen, :]

        def blocks_of(it, n_blocks=n_blocks, log_nb=log_nb, block_rows=block_rows):
            out = []
            for gg in range(grp):
                idx = jnp.asarray(it * grp + gg, jnp.int32)
                a = lax.shift_right_logical(idx, log_nb)
                n = jnp.bitwise_and(idx, n_blocks - 1)
                out.append((block_rows(a, n), block_rows(a, jnp.maximum(n - 1, 0)), jnp.minimum(n, 1)))
            return out

        def scores(it, gi=gi, blocks_of=blocks_of, load=load):
            qs, ks, bias = [], [], []
            for c, p, has_prev in blocks_of(it):
                qs.append(load(qz_ref, c))
                ks.append(jnp.concatenate([load(kz_ref, p), load(kz_ref, c)], axis=0))
                bias.append(bias_ref[2 * gi + has_prev])
            q = jnp.stack(qs).astype(BF16)
            k = jnp.stack(ks).astype(BF16)
            return jnp.einsum("gqd,gkd->gqk", q, k, preferred_element_type=F32) + jnp.stack(bias)

        def softmax_pv(it, s, gi=gi, blocks_of=blocks_of, load=load, store=store):
            blocks = blocks_of(it)
            vs = [jnp.concatenate([load(vz_ref, p), load(vz_ref, c)], axis=0) for c, p, _ in blocks]
            v = jnp.concatenate([jnp.stack(vs).astype(BF16), ones], axis=-1)
            mx = jnp.max(s, axis=-1, keepdims=True)
            p = jnp.exp(s - mx).astype(BF16)
            pv = jnp.einsum("gqk,gkd->gqd", p, v, preferred_element_type=F32)
            for gg, (c, _, _) in enumerate(blocks):
                store(acc_ref, gi, c, pv[gg, :, :hd])
                store(den_ref, gi, c, pv[gg, :, hd:])
                store(mx_ref, gi, c, jnp.broadcast_to(mx[gg], (blk, hd)))

        n_steps = dil * n_blocks // grp
        s_ref[0] = scores(0)
        for it in range(1, n_steps):
            s_ref[it & 1] = scores(it)
            softmax_pv(it - 1, s_ref[(it - 1) & 1])
        softmax_pv(n_steps - 1, s_ref[(n_steps - 1) & 1])

    nat_ref = z_ref.at[0]

    def combine(c, carry):
        sl = chunk(c)
        m0 = mx_ref[0, sl, :]
        m1 = mx_ref[1, sl, :]
        m2 = mx_ref[2, sl, :]
        mx = jnp.maximum(jnp.maximum(m0, m1), m2)
        w0 = jnp.exp(m0 - mx)
        w1 = jnp.exp(m1 - mx)
        w2 = jnp.exp(m2 - mx)
        num = w0 * acc_ref[0, sl, :] + w1 * acc_ref[1, sl, :] + w2 * acc_ref[2, sl, :]
        den = w0 * den_ref[0, sl, :] + w1 * den_ref[1, sl, :] + w2 * den_ref[2, sl, :]
        nat_ref[natural_rows(c), :] = num / den
        return carry

    lax.fori_loop(0, n_chunks, combine, 0)

    def emit(c, carry):
        o_ref[chunk(c), :] = nat_ref[chunk(c), :].astype(o_ref.dtype)
        return carry

    lax.fori_loop(0, n_chunks, emit, 0, unroll=2)


def _attn_call(qkv, batch, seq, casts):
    qkv = qkv.reshape(batch, seq, 3 * D_MODEL)
    hd = ATT_HEAD_DIM
    n_groups = len(DILATED_PATTERNS)
    cast_in, cast_out, cast_shapes, cast_blocks = _cast_plan(casts, lambda b, h: b * ATT_HEADS + h)
    out, *cast = pl.pallas_call(
        functools.partial(_attn_kernel, seq=seq, cast_blocks=cast_blocks),
        grid=(batch, ATT_HEADS),
        in_specs=[
            pl.BlockSpec((None, seq, hd), lambda b, h: (b, 0, h)),
            pl.BlockSpec((None, seq, hd), lambda b, h: (b, 0, ATT_HEADS + h)),
            pl.BlockSpec((None, seq, hd), lambda b, h: (b, 0, 2 * ATT_HEADS + h)),
        ] + cast_in,
        out_specs=[pl.BlockSpec((None, seq, hd), lambda b, h: (b, 0, h))] + cast_out,
        out_shape=[jax.ShapeDtypeStruct((batch, seq, D_MODEL), BF16)] + cast_shapes,
        scratch_shapes=[
            pltpu.VMEM((3, seq, hd), F32),
            pltpu.VMEM((n_groups, seq, hd), F32),
            pltpu.VMEM((n_groups, seq, hd), F32),
            pltpu.VMEM((n_groups, seq, hd), F32),
            pltpu.VMEM((2 * n_groups, ATT_BLOCK, 2 * ATT_BLOCK), F32),
            pltpu.VMEM((2, ATT_GROUP, ATT_BLOCK, 2 * ATT_BLOCK), F32),
        ],
        compiler_params=_params(("arbitrary", "arbitrary")),
        name="dilated_attn",
    )(qkv, qkv, qkv, *[c.src for c in casts])
    return out.reshape(batch * seq, D_MODEL), cast


def _proj_kernel(a_ref, w_ref, x_ref, o_ref, *, rows):
    for c in range(o_ref.shape[0] // rows):
        sl = slice(c * rows, (c + 1) * rows)
        o_ref[sl, :] = x_ref[sl, :] + _dot(a_ref[sl, :], w_ref[...])


def _proj_call(a, w, x, *, tm=512, rows=256):
    m, k = a.shape
    n = w.shape[1]
    return pl.pallas_call(
        functools.partial(_proj_kernel, rows=rows),
        grid=(m // tm,),
        in_specs=[
            pl.BlockSpec((tm, k), lambda i: (i, 0)),
            pl.BlockSpec((k, n), lambda i: (0, 0)),
            pl.BlockSpec((tm, n), lambda i: (i, 0)),
        ],
        out_specs=pl.BlockSpec((tm, n), lambda i: (i, 0)),
        out_shape=jax.ShapeDtypeStruct((m, n), F32),
        compiler_params=_params(("parallel",)),
        name="proj_residual",
    )(a, w, x)


def _fill_normed(h_ref, x_ref, xh_ref, g_ref, seq_start, copy_ref=None):
    g = g_ref[...]
    halo = jnp.where(seq_start, 0.0, _rms(xh_ref[...], g))
    h_ref[0:HALO, :] = halo.astype(BF16)
    _norm_rows(h_ref, HALO, x_ref, g, copy_ref)


def _causal_conv(u_ref, c, w_ref, b_ref, taps, rows):
    w = w_ref[...]
    y = b_ref[...]
    for j in range(taps):
        off = c * rows + HALO - (taps - 1) + j
        y = y + w[j:j + 1, :] * u_ref[off:off + rows, :]
    return y


def _ffn_kernel(x_ref, xh_ref, g_ref, wg_ref, wu_ref, cwg_ref, cwu_ref, cbg_ref, cbu_ref, wd_ref,
                o_ref, h_ref, ug_ref, uu_ref, *, rows, tiles_per_seq):
    i = pl.program_id(0)
    j = pl.program_id(1)

    @pl.when(j == 0)
    def _():
        _fill_normed(h_ref, x_ref, xh_ref, g_ref, i % tiles_per_seq == 0, copy_ref=o_ref)

    def down_proj(c):
        gate = _causal_conv(ug_ref, c, cwg_ref, cbg_ref, FFN_CONV, rows)
        up = _causal_conv(uu_ref, c, cwu_ref, cbu_ref, FFN_CONV, rows)
        act = (gate * up / (1.0 + jnp.exp(-gate))).astype(BF16)
        o_ref[c * rows:(c + 1) * rows, :] += _dot(act, wd_ref[...])

    _row_chunks_conv(h_ref, [wg_ref, wu_ref], [ug_ref, uu_ref], rows, down_proj)


def _ffn_call(x, g, w_up, conv_w, conv_b, w_down, layer, seq, *, tm=1024, tf=512, rows=256):
    m = x.shape[0]
    nf = FFN_DIM // tf
    halo_blocks_per_tile = tm // HALO
    return pl.pallas_call(
        functools.partial(_ffn_kernel, rows=rows, tiles_per_seq=seq // tm),
        grid=(m // tm, nf),
        in_specs=[
            pl.BlockSpec((tm, D_MODEL), lambda i, j: (i, 0)),
            pl.BlockSpec((HALO, D_MODEL), lambda i, j: (jnp.maximum(i * halo_blocks_per_tile - 1, 0), 0)),
            pl.BlockSpec((None, 1, D_MODEL), lambda i, j: (layer, 0, 0)),
            pl.BlockSpec((D_MODEL, tf), lambda i, j: (0, j)),
            pl.BlockSpec((D_MODEL, tf), lambda i, j: (0, nf + j)),
            pl.BlockSpec((None, FFN_CONV, tf), lambda i, j: (layer, 0, j)),
            pl.BlockSpec((None, FFN_CONV, tf), lambda i, j: (layer, 0, nf + j)),
            pl.BlockSpec((None, 1, tf), lambda i, j: (layer, 0, j)),
            pl.BlockSpec((None, 1, tf), lambda i, j: (layer, 0, nf + j)),
            pl.BlockSpec((tf, D_MODEL), lambda i, j: (j, 0)),
        ],
        out_specs=pl.BlockSpec((tm, D_MODEL), lambda i, j: (i, 0)),
        out_shape=jax.ShapeDtypeStruct((m, D_MODEL), F32),
        scratch_shapes=[
            pltpu.VMEM((tm + HALO, D_MODEL), BF16),
            pltpu.VMEM((tm + HALO, tf), F32),
            pltpu.VMEM((tm + HALO, tf), F32),
        ],
        compiler_params=_params(("parallel", "arbitrary")),
        name="conv_ffn",
    )(x, x, g, w_up, w_up, conv_w, conv_w, conv_b, conv_b, w_down)


def _lstm_in_kernel(x_ref, xh_ref, g_ref, w_ref, wgate_ref, bgate_ref, cw_ref, cb_ref,
                    z_ref, gates_ref, h_ref, u_ref, *, rows, tiles_per_seq, n_conv_blocks):
    i = pl.program_id(0)
    j = pl.program_id(1)

    @pl.when(j == 0)
    def _():
        _fill_normed(h_ref, x_ref, xh_ref, g_ref, i % tiles_per_seq == 0)
        wg = wgate_ref[...]
        wg = jnp.concatenate([wg, jnp.zeros((LANES - wg.shape[0], wg.shape[1]), F32)], axis=0).astype(BF16)
        gates_ref[...] = _dot_nt(h_ref[HALO:, :], wg) + bgate_ref[...]

    @pl.when(j < n_conv_blocks)
    def _():
        scale = jnp.where(j < n_conv_blocks // 2, 1.0, LSTM_QK_DIM ** -0.5)

        def conv_silu(c):
            y = _causal_conv(u_ref, c, cw_ref, cb_ref, LSTM_CONV, rows)
            z_ref[c * rows:(c + 1) * rows, :] = y * scale / (1.0 + jnp.exp(-y))

        _row_chunks_conv(h_ref, [w_ref], [u_ref], rows, conv_silu, dot=_dot_nt)

    @pl.when(j >= n_conv_blocks)
    def _():
        def copy(c, us):
            z_ref[c * rows:(c + 1) * rows, :] = us[0]

        _row_chunks(h_ref, [w_ref], rows, HALO, False, copy, dot=_dot_nt)


def _lstm_in_call(x, g, w_main_t, w_in_t, b_gate, conv_w, conv_b, seq, *, tm=1024, tn=1024, rows=256):
    m = x.shape[0]
    n = LSTM_MAIN_WIDTH
    n_gates = 2 * LSTM_HEADS
    assert w_in_t.shape[0] == n + n_gates and n % n_gates == 0 and n_gates == SUBLANES
    n_conv_blocks = 2 * LSTM_QK_WIDTH // tn
    halo_blocks_per_tile = tm // HALO
    return pl.pallas_call(
        functools.partial(_lstm_in_kernel, rows=rows, tiles_per_seq=seq // tm, n_conv_blocks=n_conv_blocks),
        grid=(m // tm, n // tn),
        in_specs=[
            pl.BlockSpec((tm, D_MODEL), lambda i, j: (i, 0)),
            pl.BlockSpec((HALO, D_MODEL), lambda i, j: (jnp.maximum(i * halo_blocks_per_tile - 1, 0), 0)),
            pl.BlockSpec((1, D_MODEL), lambda i, j: (0, 0)),
            pl.BlockSpec((tn, D_MODEL), lambda i, j: (j, 0)),
            pl.BlockSpec((n_gates, D_MODEL), lambda i, j: (n // n_gates, 0)),
            pl.BlockSpec((1, LANES), lambda i, j: (0, 0)),
            pl.BlockSpec((LSTM_CONV, tn), lambda i, j: (0, jnp.minimum(j, n_conv_blocks - 1))),
            pl.BlockSpec((1, tn), lambda i, j: (0, jnp.minimum(j, n_conv_blocks - 1))),
        ],
        out_specs=[
            pl.BlockSpec((tm, tn), lambda i, j: (i, j)),
            pl.BlockSpec((tm, LANES), lambda i, j: (i, 0)),
        ],
        out_shape=[
            jax.ShapeDtypeStruct((m, n), F32),
            jax.ShapeDtypeStruct((m, LANES), F32),
        ],
        scratch_shapes=[
            pltpu.VMEM((tm + HALO, D_MODEL), BF16),
            pltpu.VMEM((tm + HALO, tn), F32),
        ],
        compiler_params=_params(("parallel", "arbitrary")),
        name="lstm_in",
    )(x, x, g, w_main_t, w_in_t, b_gate, conv_w, conv_b)


def _log_sigmoid(x):
    return -(jnp.maximum(-x, 0.0) + jnp.log1p(jnp.exp(-jnp.abs(x))))


def _cumsum_rows(x):
    n = x.shape[0]
    row = lax.broadcasted_iota(jnp.int32, x.shape, 0)
    shift = 1
    while shift < n:
        x = x + jnp.where(row >= shift, pltpu.roll(x, shift, axis=0), 0.0)
        shift *= 2
    return x


def _mlstm_kernel(q_ref, k_ref, v_ref, og_ref, gates_ref, hg_ref, wo_ref, x_ref, o_ref, c_ref, n_ref, m_ref, hs_ref):
    chunk = q_ref.shape[0]
    dk, dv, heads = LSTM_QK_DIM, LSTM_V_DIM, LSTM_HEADS

    @pl.when(pl.program_id(1) == 0)
    def _():
        c_ref[...] = jnp.zeros_like(c_ref)
        n_ref[...] = jnp.zeros_like(n_ref)
        m_ref[...] = jnp.zeros_like(m_ref)
        hs_ref[...] = jnp.zeros_like(hs_ref)

    o_ref[...] = x_ref[...] + _dot(hs_ref[...], wo_ref[...])

    gates = gates_ref[...]
    cum_f = _cumsum_rows(_log_sigmoid(gates))
    gates_t = gates.T
    cum_f_t = cum_f.T
    row = lax.broadcasted_iota(jnp.int32, (chunk, chunk), 0)
    col = lax.broadcasted_iota(jnp.int32, (chunk, chunk), 1)
    causal = col <= row

    for hd in range(heads):
        qs = slice(hd * dk, (hd + 1) * dk)
        vs = slice(hd * dv, (hd + 1) * dv)
        b_col = cum_f[:, heads + hd:heads + hd + 1]
        b_row = cum_f_t[heads + hd:heads + hd + 1, :]
        li_col = gates[:, hd:hd + 1]
        li_row = gates_t[hd:hd + 1, :]
        m_prev = m_ref[hd]
        c_prev = c_ref[hd]
        n_prev = n_ref[hd]
        q = q_ref[:, qs]
        k = k_ref[:, qs]
        v = v_ref[:, vs]
        qb = q.astype(BF16)
        kb = k.astype(BF16)

        dmat = jnp.where(causal, b_col - b_row + li_row, -jnp.inf)
        g = b_col + m_prev
        m_t = jnp.maximum(g, jnp.max(dmat, axis=-1, keepdims=True))
        p = jnp.exp(dmat - m_t)
        inter = jnp.exp(g - m_t)
        w = p * _dot_nt(qb, kb)
        num = inter * _dot(qb, c_prev.astype(BF16)) + _dot(w.astype(BF16), v.astype(BF16))
        den = inter * jnp.sum(q * n_prev, axis=-1, keepdims=True) + jnp.sum(w, axis=-1, keepdims=True)
        h = num / jnp.maximum(jnp.abs(den), jnp.exp(-m_t))

        b_last = b_col[chunk - 1:chunk, :]
        a = b_last - b_col + li_col
        m_new = jnp.maximum(b_last + m_prev, jnp.max(a, axis=0, keepdims=True))
        decay = jnp.exp(b_last + m_prev - m_new)
        wts = jnp.exp(a - m_new)
        c_ref[hd] = decay * c_prev + _dot_tn(kb, (wts * v).astype(BF16))
        n_ref[hd] = decay * n_prev + jnp.sum(wts * k, axis=0, keepdims=True)
        m_ref[hd] = m_new

        hs = _rms(h, hg_ref[:, vs]) * jax.nn.sigmoid(og_ref[:, vs])
        hs_ref[:, vs] = hs.astype(hs_ref.dtype)


def _mlstm_call(z, gates, head_gain, w_out, x, batch, seq):
    chunk = LSTM_CHUNK
    nc = seq // chunk
    z = z.reshape(batch, seq, LSTM_MAIN_WIDTH)
    gates = gates.reshape(batch, seq, LANES)
    qkw = LSTM_QK_WIDTH

    def cur(c):
        return jnp.minimum(c, nc - 1)

    def lagged(b, c):
        return b * nc + jnp.maximum(c - 1, 0)

    return pl.pallas_call(
        _mlstm_kernel,
        grid=(batch, nc + 1),
        in_specs=[
            pl.BlockSpec((None, chunk, qkw), lambda b, c: (b, cur(c), 0)),
            pl.BlockSpec((None, chunk, qkw), lambda b, c: (b, cur(c), 1)),
            pl.BlockSpec((None, chunk, D_MODEL), lambda b, c: (b, cur(c), 2 * qkw // D_MODEL)),
            pl.BlockSpec((None, chunk, D_MODEL), lambda b, c: (b, cur(c), 2 * qkw // D_MODEL + 1)),
            pl.BlockSpec((None, chunk, LANES), lambda b, c: (b, cur(c), 0)),
            pl.BlockSpec((1, D_MODEL), lambda b, c: (0, 0)),
            pl.BlockSpec((D_MODEL, D_MODEL), lambda b, c: (0, 0), pipeline_mode=pl.Buffered(1)),
            pl.BlockSpec((chunk, D_MODEL), lambda b, c: (lagged(b, c), 0)),
        ],
        out_specs=pl.BlockSpec((chunk, D_MODEL), lambda b, c: (lagged(b, c), 0)),
        out_shape=jax.ShapeDtypeStruct((batch * seq, D_MODEL), F32),
        scratch_shapes=[
            pltpu.VMEM((LSTM_HEADS, LSTM_QK_DIM, LSTM_V_DIM), F32),
            pltpu.VMEM((LSTM_HEADS, 1, LSTM_QK_DIM), F32),
            pltpu.VMEM((LSTM_HEADS, 1, 1), F32),
            pltpu.VMEM((chunk, D_MODEL), BF16),
        ],
        compiler_params=_params(("arbitrary", "arbitrary")),
        name="mlstm_proj",
    )(z, z, z, z, gates, head_gain, w_out, x)


def kernel(x, attn_norm, attn_w_qkv, attn_q_gain, attn_k_gain, attn_w_o, lstm_norm, lstm_w_in, lstm_gate_bias,
           lstm_conv_w, lstm_conv_b, lstm_head_gain, lstm_w_out, ffn_norm, ffn_w_up, ffn_conv_w, ffn_conv_b,
           ffn_w_down):
    batch, seq, d = x.shape
    assert d == D_MODEL and ffn_norm.shape[0] == 2 and attn_norm.shape[0] == 1 and lstm_norm.shape[0] == 1
    m = batch * seq
    xf = x.reshape(m, d)

    def ffn(xin, layer, w_up, w_down):
        return _ffn_call(xin, ffn_norm[:, None, :], w_up, ffn_conv_w, ffn_conv_b[:, None, :], w_down, layer, seq)

    w_in_t = jnp.swapaxes(lstm_w_in[0], 0, 1)
    qkv, (w_up0, w_down0, w_o, w_in_main_t) = _qkv_call(
        xf, attn_norm[0][None], attn_w_qkv[0].astype(BF16), attn_q_gain[0][None], attn_k_gain[0][None],
        [_Cast(ffn_w_up, 0, 64), _Cast(ffn_w_down, 0, 128), _Cast(attn_w_o, 0, 64),
         _Cast(w_in_t, None, 128, LSTM_MAIN_WIDTH)])
    att, (w_up1, w_down1, w_out) = _attn_call(
        qkv, batch, seq, [_Cast(ffn_w_up, 1, 64), _Cast(ffn_w_down, 1, 176), _Cast(lstm_w_out, 0, 64)])
    xf = _proj_call(att, w_o, xf)
    xf = ffn(xf, 0, w_up0, w_down0)

    n_gates = 2 * LSTM_HEADS
    b_gate = jnp.pad(lstm_gate_bias[0], (0, LANES - n_gates))[None]
    z, gates = _lstm_in_call(xf, lstm_norm[0][None], w_in_main_t, w_in_t, b_gate,
                             lstm_conv_w[0], lstm_conv_b[0][None], seq)
    xf = _mlstm_call(z, gates, lstm_head_gain[0][None], w_out, xf, batch, seq)
    xf = ffn(xf, 1, w_up1, w_down1)
    return xf.reshape(batch, seq, d)
```

```python
import functools
from typing import NamedTuple, Optional

import jax
import jax.numpy as jnp
from jax import lax
from jax.experimental import pallas as pl
from jax.experimental.pallas import tpu as pltpu

F32 = jnp.float32
BF16 = jnp.bfloat16

D_MODEL = 2048
ATT_HEADS = 16
ATT_HEAD_DIM = D_MODEL // ATT_HEADS
DILATED_PATTERNS = ((128, 1), (512, 4), (2048, 16))
ATT_BLOCK = 128
ATT_GROUP = 8
LSTM_HEADS = 4
LSTM_V_DIM = D_MODEL // LSTM_HEADS
LSTM_QK_DIM = LSTM_V_DIM // 2
LSTM_QK_WIDTH = LSTM_HEADS * LSTM_QK_DIM
LSTM_MAIN_WIDTH = 2 * LSTM_QK_WIDTH + 2 * D_MODEL
LSTM_CONV = 4
FFN_DIM = ((8 * D_MODEL // 3 + 255) // 256) * 256
FFN_CONV = 3
NORM_EPS = 1e-6

LANES = 128
SUBLANES = 8
BF16_ROWS_PER_VREG = 2 * SUBLANES
HALO = BF16_ROWS_PER_VREG
VMEM_LIMIT = 56 * 1024 * 1024

LSTM_CHUNK = 256

NORM_ROWS = 128


def _rms(x, g):
    ms = jnp.mean(x * x, axis=-1, keepdims=True)
    return x * lax.rsqrt(ms + NORM_EPS) * g


def _norm_rows(h_ref, lead, x_ref, g, copy_ref=None):
    def body(r, carry):
        start = pl.multiple_of(r * NORM_ROWS, NORM_ROWS)
        x = x_ref[pl.ds(start, NORM_ROWS), :]
        h_ref[pl.ds(pl.multiple_of(start + lead, BF16_ROWS_PER_VREG), NORM_ROWS), :] = _rms(x, g).astype(BF16)
        if copy_ref is not None:
            copy_ref[pl.ds(start, NORM_ROWS), :] = x
        return carry

    lax.fori_loop(0, x_ref.shape[0] // NORM_ROWS, body, 0)


def _dot(a, b):
    return jnp.dot(a, b, preferred_element_type=F32)


def _dot_nt(a, b):
    return lax.dot_general(a, b, (((1,), (1,)), ((), ())), preferred_element_type=F32)


def _dot_tn(a, b):
    return lax.dot_general(a, b, (((0,), (0,)), ((), ())), preferred_element_type=F32)


def _params(sem):
    return pltpu.CompilerParams(dimension_semantics=sem, vmem_limit_bytes=VMEM_LIMIT)


class _Cast(NamedTuple):
    src: jax.Array
    layer: Optional[int]
    rows: int
    total_rows: Optional[int] = None


def _cast_plan(casts, step_of):
    in_specs, out_specs, out_shapes, n_blocks = [], [], [], []
    for c in casts:
        r, cols = c.src.shape[-2:]
        r = c.total_rows or r
        assert r % c.rows == 0 and c.rows % BF16_ROWS_PER_VREG == 0
        nb = r // c.rows

        def block(*ids, nb=nb):
            return jnp.minimum(step_of(*ids), nb - 1)

        if c.layer is None:
            in_specs.append(pl.BlockSpec((c.rows, cols), lambda *ids, block=block: (block(*ids), 0)))
        else:
            in_specs.append(pl.BlockSpec((None, c.rows, cols),
                                         lambda *ids, block=block, layer=c.layer: (layer, block(*ids), 0)))
        out_specs.append(pl.BlockSpec((c.rows, cols), lambda *ids, block=block: (block(*ids), 0)))
        out_shapes.append(jax.ShapeDtypeStruct((r, cols), BF16))
        n_blocks.append(nb)
    return in_specs, out_specs, out_shapes, tuple(n_blocks)


def _cast_step(step, src_refs, dst_refs, n_blocks):
    for src, dst, nb in zip(src_refs, dst_refs, n_blocks):
        @pl.when(step < nb)
        def _(src=src, dst=dst):
            for r0 in range(0, src.shape[0], BF16_ROWS_PER_VREG):
                sl = slice(r0, r0 + BF16_ROWS_PER_VREG)
                dst[sl, :] = src[sl, :].astype(BF16)


def _cast_part(src_refs, dst_refs, part, n_parts):
    for src, dst in zip(src_refs, dst_refs):
        rows = src.shape[0] // n_parts
        assert rows % BF16_ROWS_PER_VREG == 0
        for r0 in range(part * rows, (part + 1) * rows, BF16_ROWS_PER_VREG):
            sl = slice(r0, r0 + BF16_ROWS_PER_VREG)
            dst[sl, :] = src[sl, :].astype(BF16)


def _row_chunks(h_ref, w_refs, rows, lead, with_halo, consume, dot=_dot):
    n_chunks = (h_ref.shape[0] - lead) // rows
    assert not with_halo or lead == HALO

    def project(c, tails):
        if with_halo and c == 0:
            h = h_ref[0:rows + lead, :]
            return [dot(h, w[...]) for w in w_refs]
        h = h_ref[c * rows + lead:(c + 1) * rows + lead, :]
        us = [dot(h, w[...]) for w in w_refs]
        if with_halo:
            us = [jnp.concatenate([t, u], axis=0) for t, u in zip(tails, us)]
        return us

    us = project(0, None)
    for c in range(n_chunks):
        nxt = project(c + 1, [u[rows:, :] for u in us]) if c + 1 < n_chunks else None
        consume(c, us)
        us = nxt


def _row_chunks_conv(h_ref, w_refs, u_refs, rows, consume, dot=_dot):
    n_chunks = (h_ref.shape[0] - HALO) // rows

    def project(c):
        lo = 0 if c == 0 else c * rows + HALO
        hi = (c + 1) * rows + HALO
        h = h_ref[lo:hi, :]
        for w, u in zip(w_refs, u_refs):
            u[lo:hi, :] = dot(h, w[...])

    project(0)
    for c in range(n_chunks):
        if c + 1 < n_chunks:
            project(c + 1)
        consume(c)


def _qkv_kernel(*refs, rows, n_head_blocks, heads_per_block, cast_blocks):
    nc = len(cast_blocks)
    x_ref, g_ref, w_ref, qg_ref, kg_ref = refs[:5]
    cast_src, o_ref, cast_dst, h_ref = refs[5:5 + nc], refs[5 + nc], refs[6 + nc:6 + 2 * nc], refs[6 + 2 * nc]
    j = pl.program_id(1)
    n_chunks = h_ref.shape[0] // rows

    @pl.when(j == 0)
    def _():
        _norm_rows(h_ref, 0, x_ref, g_ref[...])

    @pl.when(j < 2 * n_head_blocks)
    def _():
        gain = jnp.where(j < n_head_blocks, qg_ref[...] * (ATT_HEAD_DIM ** -0.5), kg_ref[...])

        def qk_norm(c, us):
            _cast_part(cast_src, cast_dst, c, n_chunks)
            for hh in range(heads_per_block):
                sl = slice(hh * ATT_HEAD_DIM, (hh + 1) * ATT_HEAD_DIM)
                o_ref[c * rows:(c + 1) * rows, sl] = _rms(us[0][:, sl], gain)

        _row_chunks(h_ref, [w_ref], rows, 0, False, qk_norm)

    @pl.when(j >= 2 * n_head_blocks)
    def _():
        def copy(c, us):
            _cast_part(cast_src, cast_dst, c, n_chunks)
            o_ref[c * rows:(c + 1) * rows, :] = us[0]

        _row_chunks(h_ref, [w_ref], rows, 0, False, copy)


def _qkv_call(x, g, w, qg, kg, casts, *, tm=1024, tn=1024, rows=256):
    m = x.shape[0]
    n = w.shape[1]
    heads_per_block = tn // ATT_HEAD_DIM
    n_head_blocks = D_MODEL // tn
    n_col_blocks = n // tn
    cast_in, cast_out, cast_shapes, cast_blocks = _cast_plan(casts, lambda i, j: i * n_col_blocks + j)
    qkv, *cast = pl.pallas_call(
        functools.partial(_qkv_kernel, rows=rows, n_head_blocks=n_head_blocks, heads_per_block=heads_per_block,
                          cast_blocks=cast_blocks),
        grid=(m // tm, n_col_blocks),
        in_specs=[
            pl.BlockSpec((tm, D_MODEL), lambda i, j: (i, 0)),
            pl.BlockSpec((1, D_MODEL), lambda i, j: (0, 0)),
            pl.BlockSpec((D_MODEL, tn), lambda i, j: (0, j)),
            pl.BlockSpec((1, ATT_HEAD_DIM), lambda i, j: (0, 0)),
            pl.BlockSpec((1, ATT_HEAD_DIM), lambda i, j: (0, 0)),
        ] + cast_in,
        out_specs=[pl.BlockSpec((tm, tn), lambda i, j: (i, j))] + cast_out,
        out_shape=[jax.ShapeDtypeStruct((m, n), F32)] + cast_shapes,
        scratch_shapes=[pltpu.VMEM((tm, D_MODEL), BF16)],
        compiler_params=_params(("arbitrary", "arbitrary")),
        name="attn_qkv",
    )(x, g, w, qg, kg, *[c.src for c in casts])
    return qkv, cast


ROW_INTERLEAVE = 4


def _attn_kernel(*refs, seq, cast_blocks):
    nc = len(cast_blocks)
    q_ref, k_ref, v_ref = refs[:3]
    cast_src, o_ref, cast_dst = refs[3:3 + nc], refs[3 + nc], refs[4 + nc:4 + 2 * nc]
    z_ref, acc_ref, den_ref, mx_ref, bias_ref, s_ref = refs[4 + 2 * nc:]
    _cast_step(pl.program_id(0) * pl.num_programs(1) + pl.program_id(1), cast_src, cast_dst, cast_blocks)
    _attn_body(q_ref, k_ref, v_ref, o_ref, z_ref, acc_ref, den_ref, mx_ref, bias_ref, s_ref, seq=seq)


def _attn_body(q_ref, k_ref, v_ref, o_ref, z_ref, acc_ref, den_ref, mx_ref, bias_ref, s_ref, *, seq):
    blk, hd, grp, il = ATT_BLOCK, ATT_HEAD_DIM, ATT_GROUP, ROW_INTERLEAVE
    part = seq // il
    n_chunks = seq // blk
    assert part % blk == 0

    def chunk(c):
        return pl.ds(pl.multiple_of(c * blk, blk), blk)

    def natural_rows(c):
        lo = c // (part // blk)
        b0 = (c % (part // blk)) * blk
        return pl.ds(lo + il * b0, blk, stride=il)

    for ti, src_ref in enumerate((q_ref, k_ref, v_ref)):
        for c in range(n_chunks):
            z_ref[ti, c * blk:(c + 1) * blk, :] = src_ref[natural_rows(c), :]

    qz_ref, kz_ref, vz_ref = z_ref.at[0], z_ref.at[1], z_ref.at[2]
    row = lax.broadcasted_iota(jnp.int32, (blk, 2 * blk), 0)
    col = lax.broadcasted_iota(jnp.int32, (blk, 2 * blk), 1)
    ones = jnp.ones((grp, 2 * blk, hd), BF16)

    for gi, (window, dil) in enumerate(DILATED_PATTERNS):
        assert window // dil == blk and (dil % il == 0 or il % dil == 0)
        pieces = max(il // dil, 1)
        plen = blk // pieces
        stride = max(dil // il, 1)
        n_blocks = seq // (dil * blk)
        assert n_blocks & (n_blocks - 1) == 0 and (dil * n_blocks) % grp == 0 and plen % SUBLANES == 0
        log_nb = n_blocks.bit_length() - 1
        log_plen = plen.bit_length() - 1

        def step_in_block(i, pieces=pieces, plen=plen, log_plen=log_plen):
            return pieces * jnp.bitwise_and(i, plen - 1) + lax.shift_right_logical(i, log_plen)

        dist = step_in_block(row) - step_in_block(jnp.bitwise_and(col, blk - 1)) + jnp.where(col < blk, blk, 0)
        band = jnp.logical_and(dist >= 0, dist <= blk)
        bias_ref[2 * gi + 1] = jnp.where(band, 0.0, -jnp.inf)
        bias_ref[2 * gi] = jnp.where(jnp.logical_and(band, col >= blk), 0.0, -jnp.inf)

        def block_rows(a, n, dil=dil, pieces=pieces, plen=plen, stride=stride):
            if pieces > 1:
                return [pl.ds(pl.multiple_of((p * dil + a) * part + n * plen, SUBLANES), plen) for p in range(pieces)]
            lo = jnp.bitwise_and(a, il - 1)
            hi = lax.shift_right_logical(a, il.bit_length() - 1)
            start = lo * part + hi + stride * blk * n
            if stride == 1:
                return [pl.ds(pl.multiple_of(start, blk), blk)]
            return [pl.ds(start, blk, stride=stride)]

        def load(ref, parts):
            return jnp.concatenate([ref[d, :] for d in parts], axis=0)

        def store(ref, gi, parts, val, plen=plen):
            for p, d in enumerate(parts):
                ref[gi, d, :] = val[p * plen:(p + 1) * plen, :]

        def blocks_of(it, n_blocks=n_blocks, log_nb=log_nb, block_rows=block_rows):
            out = []
            for gg in range(grp):
                idx = jnp.asarray(it * grp + gg, jnp.int32)
                a = lax.shift_right_logical(idx, log_nb)
                n = jnp.bitwise_and(idx, n_blocks - 1)
                out.append((block_rows(a, n), block_rows(a, jnp.maximum(n - 1, 0)), jnp.minimum(n, 1)))
            return out

        def scores(it, gi=gi, blocks_of=blocks_of, load=load):
            qs, ks, bias = [], [], []
            for c, p, has_prev in blocks_of(it):
                qs.append(load(qz_ref, c))
                ks.append(jnp.concatenate([load(kz_ref, p), load(kz_ref, c)], axis=0))
                bias.append(bias_ref[2 * gi + has_prev])
            q = jnp.stack(qs).astype(BF16)
            k = jnp.stack(ks).astype(BF16)
            return jnp.einsum("gqd,gkd->gqk", q, k, preferred_element_type=F32) + jnp.stack(bias)

        def softmax_pv(it, s, gi=gi, blocks_of=blocks_of, load=load, store=store):
            blocks = blocks_of(it)
            vs = [jnp.concatenate([load(vz_ref, p), load(vz_ref, c)], axis=0) for c, p, _ in blocks]
            v = jnp.concatenate([jnp.stack(vs).astype(BF16), ones], axis=-1)
            mx = jnp.max(s, axis=-1, keepdims=True)
            p = jnp.exp(s - mx).astype(BF16)
            pv = jnp.einsum("gqk,gkd->gqd", p, v, preferred_element_type=F32)
            for gg, (c, _, _) in enumerate(blocks):
                store(acc_ref, gi, c, pv[gg, :, :hd])
                store(den_ref, gi, c, pv[gg, :, hd:])
                store(mx_ref, gi, c, jnp.broadcast_to(mx[gg], (blk, hd)))

        n_steps = dil * n_blocks // grp
        s_ref[0] = scores(0)
        for it in range(1, n_steps):
            s_ref[it & 1] = scores(it)
            softmax_pv(it - 1, s_ref[(it - 1) & 1])
        softmax_pv(n_steps - 1, s_ref[(n_steps - 1) & 1])

    nat_ref = z_ref.at[0]

    def combine(c, carry):
        sl = chunk(c)
        m0 = mx_ref[0, sl, :]
        m1 = mx_ref[1, sl, :]
        m2 = mx_ref[2, sl, :]
        mx = jnp.maximum(jnp.maximum(m0, m1), m2)
        w0 = jnp.exp(m0 - mx)
        w1 = jnp.exp(m1 - mx)
        w2 = jnp.exp(m2 - mx)
        num = w0 * acc_ref[0, sl, :] + w1 * acc_ref[1, sl, :] + w2 * acc_ref[2, sl, :]
        den = w0 * den_ref[0, sl, :] + w1 * den_ref[1, sl, :] + w2 * den_ref[2, sl, :]
        nat_ref[natural_rows(c), :] = num / den
        return carry

    lax.fori_loop(0, n_chunks, combine, 0)

    def emit(c, carry):
        o_ref[chunk(c), :] = nat_ref[chunk(c), :].astype(o_ref.dtype)
        return carry

    lax.fori_loop(0, n_chunks, emit, 0, unroll=2)


def _attn_call(qkv, batch, seq, casts):
    qkv = qkv.reshape(batch, seq, 3 * D_MODEL)
    hd = ATT_HEAD_DIM
    n_groups = len(DILATED_PATTERNS)
    cast_in, cast_out, cast_shapes, cast_blocks = _cast_plan(casts, lambda b, h: b * ATT_HEADS + h)
    out, *cast = pl.pallas_call(
        functools.partial(_attn_kernel, seq=seq, cast_blocks=cast_blocks),
        grid=(batch, ATT_HEADS),
        in_specs=[
            pl.BlockSpec((None, seq, hd), lambda b, h: (b, 0, h)),
            pl.BlockSpec((None, seq, hd), lambda b, h: (b, 0, ATT_HEADS + h)),
            pl.BlockSpec((None, seq, hd), lambda b, h: (b, 0, 2 * ATT_HEADS + h)),
        ] + cast_in,
        out_specs=[pl.BlockSpec((None, seq, hd), lambda b, h: (b, 0, h))] + cast_out,
        out_shape=[jax.ShapeDtypeStruct((batch, seq, D_MODEL), BF16)] + cast_shapes,
        scratch_shapes=[
            pltpu.VMEM((3, seq, hd), F32),
            pltpu.VMEM((n_groups, seq, hd), F32),
            pltpu.VMEM((n_groups, seq, hd), F32),
            pltpu.VMEM((n_groups, seq, hd), F32),
            pltpu.VMEM((2 * n_groups, ATT_BLOCK, 2 * ATT_BLOCK), F32),
            pltpu.VMEM((2, ATT_GROUP, ATT_BLOCK, 2 * ATT_BLOCK), F32),
        ],
        compiler_params=_params(("arbitrary", "arbitrary")),
        name="dilated_attn",
    )(qkv, qkv, qkv, *[c.src for c in casts])
    return out.reshape(batch * seq, D_MODEL), cast


def _proj_kernel(a_ref, w_ref, x_ref, o_ref, *, rows):
    for c in range(o_ref.shape[0] // rows):
        sl = slice(c * rows, (c + 1) * rows)
        o_ref[sl, :] = x_ref[sl, :] + _dot(a_ref[sl, :], w_ref[...])


def _proj_call(a, w, x, *, tm=1024, rows=512):
    m, k = a.shape
    n = w.shape[1]
    return pl.pallas_call(
        functools.partial(_proj_kernel, rows=rows),
        grid=(m // tm,),
        in_specs=[
            pl.BlockSpec((tm, k), lambda i: (i, 0)),
            pl.BlockSpec((k, n), lambda i: (0, 0)),
            pl.BlockSpec((tm, n), lambda i: (i, 0)),
        ],
        out_specs=pl.BlockSpec((tm, n), lambda i: (i, 0)),
        out_shape=jax.ShapeDtypeStruct((m, n), F32),
        compiler_params=_params(("parallel",)),
        name="proj_residual",
    )(a, w, x)


def _fill_normed(h_ref, x_ref, xh_ref, g_ref, seq_start, copy_ref=None):
    g = g_ref[...]
    halo = jnp.where(seq_start, 0.0, _rms(xh_ref[...], g))
    h_ref[0:HALO, :] = halo.astype(BF16)
    _norm_rows(h_ref, HALO, x_ref, g, copy_ref)


def _causal_conv(u_ref, c, w_ref, b_ref, taps, rows):
    w = w_ref[...]
    y = b_ref[...]
    for j in range(taps):
        off = c * rows + HALO - (taps - 1) + j
        y = y + w[j:j + 1, :] * u_ref[off:off + rows, :]
    return y


def _ffn_kernel(x_ref, xh_ref, g_ref, wg_ref, wu_ref, cwg_ref, cwu_ref, cbg_ref, cbu_ref, wd_ref,
                o_ref, h_ref, ug_ref, uu_ref, *, rows, tiles_per_seq):
    i = pl.program_id(0)
    j = pl.program_id(1)

    @pl.when(j == 0)
    def _():
        _fill_normed(h_ref, x_ref, xh_ref, g_ref, i % tiles_per_seq == 0, copy_ref=o_ref)

    def down_proj(c):
        gate = _causal_conv(ug_ref, c, cwg_ref, cbg_ref, FFN_CONV, rows)
        up = _causal_conv(uu_ref, c, cwu_ref, cbu_ref, FFN_CONV, rows)
        act = (gate * up / (1.0 + jnp.exp(-gate))).astype(BF16)
        o_ref[c * rows:(c + 1) * rows, :] += _dot(act, wd_ref[...])

    _row_chunks_conv(h_ref, [wg_ref, wu_ref], [ug_ref, uu_ref], rows, down_proj)


def _ffn_call(x, g, w_up, conv_w, conv_b, w_down, layer, seq, *, tm=1024, tf=512, rows=512):
    m = x.shape[0]
    nf = FFN_DIM // tf
    halo_blocks_per_tile = tm // HALO
    return pl.pallas_call(
        functools.partial(_ffn_kernel, rows=rows, tiles_per_seq=seq // tm),
        grid=(m // tm, nf),
        in_specs=[
            pl.BlockSpec((tm, D_MODEL), lambda i, j: (i, 0)),
            pl.BlockSpec((HALO, D_MODEL), lambda i, j: (jnp.maximum(i * halo_blocks_per_tile - 1, 0), 0)),
            pl.BlockSpec((None, 1, D_MODEL), lambda i, j: (layer, 0, 0)),
            pl.BlockSpec((D_MODEL, tf), lambda i, j: (0, j)),
            pl.BlockSpec((D_MODEL, tf), lambda i, j: (0, nf + j)),
            pl.BlockSpec((None, FFN_CONV, tf), lambda i, j: (layer, 0, j)),
            pl.BlockSpec((None, FFN_CONV, tf), lambda i, j: (layer, 0, nf + j)),
            pl.BlockSpec((None, 1, tf), lambda i, j: (layer, 0, j)),
            pl.BlockSpec((None, 1, tf), lambda i, j: (layer, 0, nf + j)),
            pl.BlockSpec((tf, D_MODEL), lambda i, j: (j, 0)),
        ],
        out_specs=pl.BlockSpec((tm, D_MODEL), lambda i, j: (i, 0)),
        out_shape=jax.ShapeDtypeStruct((m, D_MODEL), F32),
        scratch_shapes=[
            pltpu.VMEM((tm + HALO, D_MODEL), BF16),
            pltpu.VMEM((tm + HALO, tf), F32),
            pltpu.VMEM((tm + HALO, tf), F32),
        ],
        compiler_params=_params(("parallel", "arbitrary")),
        name="conv_ffn",
    )(x, x, g, w_up, w_up, conv_w, conv_w, conv_b, conv_b, w_down)


def _lstm_in_kernel(x_ref, xh_ref, g_ref, w_ref, wgate_ref, bgate_ref, cw_ref, cb_ref,
                    z_ref, gates_ref, h_ref, u_ref, *, rows, tiles_per_seq, n_conv_blocks):
    i = pl.program_id(0)
    j = pl.program_id(1)

    @pl.when(j == 0)
    def _():
        _fill_normed(h_ref, x_ref, xh_ref, g_ref, i % tiles_per_seq == 0)
        wg = wgate_ref[...]
        wg = jnp.concatenate([wg, jnp.zeros((LANES - wg.shape[0], wg.shape[1]), F32)], axis=0).astype(BF16)
        gates_ref[...] = _dot_nt(h_ref[HALO:, :], wg) + bgate_ref[...]

    @pl.when(j < n_conv_blocks)
    def _():
        scale = jnp.where(j < n_conv_blocks // 2, 1.0, LSTM_QK_DIM ** -0.5)

        def conv_silu(c):
            y = _causal_conv(u_ref, c, cw_ref, cb_ref, LSTM_CONV, rows)
            z_ref[c * rows:(c + 1) * rows, :] = y * scale / (1.0 + jnp.exp(-y))

        _row_chunks_conv(h_ref, [w_ref], [u_ref], rows, conv_silu, dot=_dot_nt)

    @pl.when(j >= n_conv_blocks)
    def _():
        def copy(c, us):
            z_ref[c * rows:(c + 1) * rows, :] = us[0]

        _row_chunks(h_ref, [w_ref], rows, HALO, False, copy, dot=_dot_nt)


def _lstm_in_call(x, g, w_main_t, w_in_t, b_gate, conv_w, conv_b, seq, *, tm=1024, tn=1024, rows=256):
    m = x.shape[0]
    n = LSTM_MAIN_WIDTH
    n_gates = 2 * LSTM_HEADS
    assert w_in_t.shape[0] == n + n_gates and n % n_gates == 0 and n_gates == SUBLANES
    n_conv_blocks = 2 * LSTM_QK_WIDTH // tn
    halo_blocks_per_tile = tm // HALO
    return pl.pallas_call(
        functools.partial(_lstm_in_kernel, rows=rows, tiles_per_seq=seq // tm, n_conv_blocks=n_conv_blocks),
        grid=(m // tm, n // tn),
        in_specs=[
            pl.BlockSpec((tm, D_MODEL), lambda i, j: (i, 0)),
            pl.BlockSpec((HALO, D_MODEL), lambda i, j: (jnp.maximum(i * halo_blocks_per_tile - 1, 0), 0)),
            pl.BlockSpec((1, D_MODEL), lambda i, j: (0, 0)),
            pl.BlockSpec((tn, D_MODEL), lambda i, j: (j, 0)),
            pl.BlockSpec((n_gates, D_MODEL), lambda i, j: (n // n_gates, 0)),
            pl.BlockSpec((1, LANES), lambda i, j: (0, 0)),
            pl.BlockSpec((LSTM_CONV, tn), lambda i, j: (0, jnp.minimum(j, n_conv_blocks - 1))),
            pl.BlockSpec((1, tn), lambda i, j: (0, jnp.minimum(j, n_conv_blocks - 1))),
        ],
        out_specs=[
            pl.BlockSpec((tm, tn), lambda i, j: (i, j)),
            pl.BlockSpec((tm, LANES), lambda i, j: (i, 0)),
        ],
        out_shape=[
            jax.ShapeDtypeStruct((m, n), F32),
            jax.ShapeDtypeStruct((m, LANES), F32),
        ],
        scratch_shapes=[
            pltpu.VMEM((tm + HALO, D_MODEL), BF16),
            pltpu.VMEM((tm + HALO, tn), F32),
        ],
        compiler_params=_params(("parallel", "arbitrary")),
        name="lstm_in",
    )(x, x, g, w_main_t, w_in_t, b_gate, conv_w, conv_b)


def _log_sigmoid(x):
    return -(jnp.maximum(-x, 0.0) + jnp.log1p(jnp.exp(-jnp.abs(x))))


def _cumsum_rows(x):
    n = x.shape[0]
    row = lax.broadcasted_iota(jnp.int32, x.shape, 0)
    shift = 1
    while shift < n:
        x = x + jnp.where(row >= shift, pltpu.roll(x, shift, axis=0), 0.0)
        shift *= 2
    return x


def _mlstm_kernel(q_ref, k_ref, v_ref, og_ref, gates_ref, hg_ref, wo_ref, x_ref, o_ref, c_ref, n_ref, m_ref, hs_ref):
    chunk = q_ref.shape[0]
    dk, dv, heads = LSTM_QK_DIM, LSTM_V_DIM, LSTM_HEADS

    @pl.when(pl.program_id(1) == 0)
    def _():
        c_ref[...] = jnp.zeros_like(c_ref)
        n_ref[...] = jnp.zeros_like(n_ref)
        m_ref[...] = jnp.zeros_like(m_ref)
        hs_ref[...] = jnp.zeros_like(hs_ref)

    o_ref[...] = x_ref[...] + _dot(hs_ref[...], wo_ref[...])

    gates = gates_ref[...]
    cum_f = _cumsum_rows(_log_sigmoid(gates))
    gates_t = gates.T
    cum_f_t = cum_f.T
    row = lax.broadcasted_iota(jnp.int32, (chunk, chunk), 0)
    col = lax.broadcasted_iota(jnp.int32, (chunk, chunk), 1)
    causal = col <= row

    for hd in range(heads):
        qs = slice(hd * dk, (hd + 1) * dk)
        vs = slice(hd * dv, (hd + 1) * dv)
        b_col = cum_f[:, heads + hd:heads + hd + 1]
        b_row = cum_f_t[heads + hd:heads + hd + 1, :]
        li_col = gates[:, hd:hd + 1]
        li_row = gates_t[hd:hd + 1, :]
        m_prev = m_ref[hd]
        c_prev = c_ref[hd]
        n_prev = n_ref[hd]
        q = q_ref[:, qs]
        k = k_ref[:, qs]
        v = v_ref[:, vs]
        qb = q.astype(BF16)
        kb = k.astype(BF16)

        dmat = jnp.where(causal, b_col - b_row + li_row, -jnp.inf)
        g = b_col + m_prev
        m_t = jnp.maximum(g, jnp.max(dmat, axis=-1, keepdims=True))
        p = jnp.exp(dmat - m_t)
        inter = jnp.exp(g - m_t)
        w = p * _dot_nt(qb, kb)
        num = inter * _dot(qb, c_prev.astype(BF16)) + _dot(w.astype(BF16), v.astype(BF16))
        den = inter * jnp.sum(q * n_prev, axis=-1, keepdims=True) + jnp.sum(w, axis=-1, keepdims=True)
        h = num / jnp.maximum(jnp.abs(den), jnp.exp(-m_t))

        b_last = b_col[chunk - 1:chunk, :]
        a = b_last - b_col + li_col
        m_new = jnp.maximum(b_last + m_prev, jnp.max(a, axis=0, keepdims=True))
        decay = jnp.exp(b_last + m_prev - m_new)
        wts = jnp.exp(a - m_new)
        c_ref[hd] = decay * c_prev + _dot_tn(kb, (wts * v).astype(BF16))
        n_ref[hd] = decay * n_prev + jnp.sum(wts * k, axis=0, keepdims=True)
        m_ref[hd] = m_new

        hs = _rms(h, hg_ref[:, vs]) * jax.nn.sigmoid(og_ref[:, vs])
        hs_ref[:, vs] = hs.astype(hs_ref.dtype)


def _mlstm_call(z, gates, head_gain, w_out, x, batch, seq):
    chunk = LSTM_CHUNK
    nc = seq // chunk
    z = z.reshape(batch, seq, LSTM_MAIN_WIDTH)
    gates = gates.reshape(batch, seq, LANES)
    qkw = LSTM_QK_WIDTH

    def cur(c):
        return jnp.minimum(c, nc - 1)

    def lagged(b, c):
        return b * nc + jnp.maximum(c - 1, 0)

    return pl.pallas_call(
        _mlstm_kernel,
        grid=(batch, nc + 1),
        in_specs=[
            pl.BlockSpec((None, chunk, qkw), lambda b, c: (b, cur(c), 0)),
            pl.BlockSpec((None, chunk, qkw), lambda b, c: (b, cur(c), 1)),
            pl.BlockSpec((None, chunk, D_MODEL), lambda b, c: (b, cur(c), 2 * qkw // D_MODEL)),
            pl.BlockSpec((None, chunk, D_MODEL), lambda b, c: (b, cur(c), 2 * qkw // D_MODEL + 1)),
            pl.BlockSpec((None, chunk, LANES), lambda b, c: (b, cur(c), 0)),
            pl.BlockSpec((1, D_MODEL), lambda b, c: (0, 0)),
            pl.BlockSpec((D_MODEL, D_MODEL), lambda b, c: (0, 0), pipeline_mode=pl.Buffered(1)),
            pl.BlockSpec((chunk, D_MODEL), lambda b, c: (lagged(b, c), 0)),
        ],
        out_specs=pl.BlockSpec((chunk, D_MODEL), lambda b, c: (lagged(b, c), 0)),
        out_shape=jax.ShapeDtypeStruct((batch * seq, D_MODEL), F32),
        scratch_shapes=[
            pltpu.VMEM((LSTM_HEADS, LSTM_QK_DIM, LSTM_V_DIM), F32),
            pltpu.VMEM((LSTM_HEADS, 1, LSTM_QK_DIM), F32),
            pltpu.VMEM((LSTM_HEADS, 1, 1), F32),
            pltpu.VMEM((chunk, D_MODEL), BF16),
        ],
        compiler_params=_params(("arbitrary", "arbitrary")),
        name="mlstm_proj",
    )(z, z, z, z, gates, head_gain, w_out, x)


def kernel(x, attn_norm, attn_w_qkv, attn_q_gain, attn_k_gain, attn_w_o, lstm_norm, lstm_w_in, lstm_gate_bias,
           lstm_conv_w, lstm_conv_b, lstm_head_gain, lstm_w_out, ffn_norm, ffn_w_up, ffn_conv_w, ffn_conv_b,
           ffn_w_down):
    batch, seq, d = x.shape
    assert d == D_MODEL and ffn_norm.shape[0] == 2 and attn_norm.shape[0] == 1 and lstm_norm.shape[0] == 1
    m = batch * seq
    xf = x.reshape(m, d)

    def ffn(xin, layer, w_up, w_down):
        return _ffn_call(xin, ffn_norm[:, None, :], w_up, ffn_conv_w, ffn_conv_b[:, None, :], w_down, layer, seq)

    w_in_t = jnp.swapaxes(lstm_w_in[0], 0, 1)
    qkv, (w_up0, w_down0, w_o, w_in_main_t) = _qkv_call(
        xf, attn_norm[0][None], attn_w_qkv[0].astype(BF16), attn_q_gain[0][None], attn_k_gain[0][None],
        [_Cast(ffn_w_up, 0, 64), _Cast(ffn_w_down, 0, 128), _Cast(attn_w_o, 0, 64),
         _Cast(w_in_t, None, 128, LSTM_MAIN_WIDTH)])
    att, (w_up1, w_down1, w_out) = _attn_call(
        qkv, batch, seq, [_Cast(ffn_w_up, 1, 64), _Cast(ffn_w_down, 1, 176), _Cast(lstm_w_out, 0, 64)])
    xf = _proj_call(att, w_o, xf)
    xf = ffn(xf, 0, w_up0, w_down0)

    n_gates = 2 * LSTM_HEADS
    b_gate = jnp.pad(lstm_gate_bias[0], (0, LANES - n_gates))[None]
    z, gates = _lstm_in_call(xf, lstm_norm[0][None], w_in_main_t, w_in_t, b_gate,
                             lstm_conv_w[0], lstm_conv_b[0][None], seq)
    xf = _mlstm_call(z, gates, lstm_head_gain[0][None], w_out, xf, batch, seq)
    xf = ffn(xf, 1, w_up1, w_down1)
    return xf.reshape(batch, seq, d)
```

```python
import functools
from typing import NamedTuple, Optional

import jax
import jax.numpy as jnp
from jax import lax
from jax.experimental import pallas as pl
from jax.experimental.pallas import tpu as pltpu

F32 = jnp.float32
BF16 = jnp.bfloat16

D_MODEL = 2048
ATT_HEADS = 16
ATT_HEAD_DIM = D_MODEL // ATT_HEADS
DILATED_PATTERNS = ((128, 1), (512, 4), (2048, 16))
ATT_BLOCK = 128
ATT_GROUP = 4
LSTM_HEADS = 4
LSTM_V_DIM = D_MODEL // LSTM_HEADS
LSTM_QK_DIM = LSTM_V_DIM // 2
LSTM_QK_WIDTH = LSTM_HEADS * LSTM_QK_DIM
LSTM_MAIN_WIDTH = 2 * LSTM_QK_WIDTH + 2 * D_MODEL
LSTM_CONV = 4
FFN_DIM = ((8 * D_MODEL // 3 + 255) // 256) * 256
FFN_CONV = 3
NORM_EPS = 1e-6

LANES = 128
SUBLANES = 8
BF16_ROWS_PER_VREG = 2 * SUBLANES
HALO = BF16_ROWS_PER_VREG
VMEM_LIMIT = 56 * 1024 * 1024

LSTM_CHUNK = 256

NORM_ROWS = 128


def _rms(x, g):
    ms = jnp.mean(x * x, axis=-1, keepdims=True)
    return x * lax.rsqrt(ms + NORM_EPS) * g


def _norm_rows(h_ref, lead, x_ref, g, copy_ref=None):
    def body(r, carry):
        start = pl.multiple_of(r * NORM_ROWS, NORM_ROWS)
        x = x_ref[pl.ds(start, NORM_ROWS), :]
        h_ref[pl.ds(pl.multiple_of(start + lead, BF16_ROWS_PER_VREG), NORM_ROWS), :] = _rms(x, g).astype(BF16)
        if copy_ref is not None:
            copy_ref[pl.ds(start, NORM_ROWS), :] = x
        return carry

    lax.fori_loop(0, x_ref.shape[0] // NORM_ROWS, body, 0)


def _dot(a, b):
    return jnp.dot(a, b, preferred_element_type=F32)


def _dot_nt(a, b):
    return lax.dot_general(a, b, (((1,), (1,)), ((), ())), preferred_element_type=F32)


def _dot_tn(a, b):
    return lax.dot_general(a, b, (((0,), (0,)), ((), ())), preferred_element_type=F32)


def _params(sem):
    return pltpu.CompilerParams(dimension_semantics=sem, vmem_limit_bytes=VMEM_LIMIT)


class _Cast(NamedTuple):
    src: jax.Array
    layer: Optional[int]
    rows: int
    total_rows: Optional[int] = None


def _cast_plan(casts, step_of):
    in_specs, out_specs, out_shapes, n_blocks = [], [], [], []
    for c in casts:
        r, cols = c.src.shape[-2:]
        r = c.total_rows or r
        assert r % c.rows == 0 and c.rows % BF16_ROWS_PER_VREG == 0
        nb = r // c.rows

        def block(*ids, nb=nb):
            return jnp.minimum(step_of(*ids), nb - 1)

        if c.layer is None:
            in_specs.append(pl.BlockSpec((c.rows, cols), lambda *ids, block=block: (block(*ids), 0)))
        else:
            in_specs.append(pl.BlockSpec((None, c.rows, cols),
                                         lambda *ids, block=block, layer=c.layer: (layer, block(*ids), 0)))
        out_specs.append(pl.BlockSpec((c.rows, cols), lambda *ids, block=block: (block(*ids), 0)))
        out_shapes.append(jax.ShapeDtypeStruct((r, cols), BF16))
        n_blocks.append(nb)
    return in_specs, out_specs, out_shapes, tuple(n_blocks)


def _cast_step(step, src_refs, dst_refs, n_blocks):
    for src, dst, nb in zip(src_refs, dst_refs, n_blocks):
        @pl.when(step < nb)
        def _(src=src, dst=dst):
            for r0 in range(0, src.shape[0], BF16_ROWS_PER_VREG):
                sl = slice(r0, r0 + BF16_ROWS_PER_VREG)
                dst[sl, :] = src[sl, :].astype(BF16)


def _cast_part(src_refs, dst_refs, part, n_parts):
    for src, dst in zip(src_refs, dst_refs):
        rows = src.shape[0] // n_parts
        assert rows % BF16_ROWS_PER_VREG == 0
        for r0 in range(part * rows, (part + 1) * rows, BF16_ROWS_PER_VREG):
            sl = slice(r0, r0 + BF16_ROWS_PER_VREG)
            dst[sl, :] = src[sl, :].astype(BF16)


def _row_chunks(h_ref, w_refs, rows, lead, with_halo, consume, dot=_dot):
    n_chunks = (h_ref.shape[0] - lead) // rows
    assert not with_halo or lead == HALO

    def project(c, tails):
        if with_halo and c == 0:
            h = h_ref[0:rows + lead, :]
            return [dot(h, w[...]) for w in w_refs]
        h = h_ref[c * rows + lead:(c + 1) * rows + lead, :]
        us = [dot(h, w[...]) for w in w_refs]
        if with_halo:
            us = [jnp.concatenate([t, u], axis=0) for t, u in zip(tails, us)]
        return us

    us = project(0, None)
    for c in range(n_chunks):
        nxt = project(c + 1, [u[rows:, :] for u in us]) if c + 1 < n_chunks else None
        consume(c, us)
        us = nxt


def _row_chunks_conv(h_ref, w_refs, u_refs, rows, consume, dot=_dot):
    n_chunks = (h_ref.shape[0] - HALO) // rows

    def project(c):
        lo = 0 if c == 0 else c * rows + HALO
        hi = (c + 1) * rows + HALO
        h = h_ref[lo:hi, :]
        for w, u in zip(w_refs, u_refs):
            u[lo:hi, :] = dot(h, w[...])

    project(0)
    for c in range(n_chunks):
        if c + 1 < n_chunks:
            project(c + 1)
        consume(c)


def _qkv_kernel(*refs, rows, n_head_blocks, heads_per_block, cast_blocks):
    nc = len(cast_blocks)
    x_ref, g_ref, w_ref, qg_ref, kg_ref = refs[:5]
    cast_src, o_ref, cast_dst, h_ref = refs[5:5 + nc], refs[5 + nc], refs[6 + nc:6 + 2 * nc], refs[6 + 2 * nc]
    j = pl.program_id(1)
    n_chunks = h_ref.shape[0] // rows

    @pl.when(j == 0)
    def _():
        _norm_rows(h_ref, 0, x_ref, g_ref[...])

    @pl.when(j < 2 * n_head_blocks)
    def _():
        gain = jnp.where(j < n_head_blocks, qg_ref[...] * (ATT_HEAD_DIM ** -0.5), kg_ref[...])

        def qk_norm(c, us):
            _cast_part(cast_src, cast_dst, c, n_chunks)
            for hh in range(heads_per_block):
                sl = slice(hh * ATT_HEAD_DIM, (hh + 1) * ATT_HEAD_DIM)
                o_ref[c * rows:(c + 1) * rows, sl] = _rms(us[0][:, sl], gain)

        _row_chunks(h_ref, [w_ref], rows, 0, False, qk_norm)

    @pl.when(j >= 2 * n_head_blocks)
    def _():
        def copy(c, us):
            _cast_part(cast_src, cast_dst, c, n_chunks)
            o_ref[c * rows:(c + 1) * rows, :] = us[0]

        _row_chunks(h_ref, [w_ref], rows, 0, False, copy)


def _qkv_call(x, g, w, qg, kg, casts, *, tm=1024, tn=1024, rows=256):
    m = x.shape[0]
    n = w.shape[1]
    heads_per_block = tn // ATT_HEAD_DIM
    n_head_blocks = D_MODEL // tn
    n_col_blocks = n // tn
    cast_in, cast_out, cast_shapes, cast_blocks = _cast_plan(casts, lambda i, j: i * n_col_blocks + j)
    qkv, *cast = pl.pallas_call(
        functools.partial(_qkv_kernel, rows=rows, n_head_blocks=n_head_blocks, heads_per_block=heads_per_block,
                          cast_blocks=cast_blocks),
        grid=(m // tm, n_col_blocks),
        in_specs=[
            pl.BlockSpec((tm, D_MODEL), lambda i, j: (i, 0)),
            pl.BlockSpec((1, D_MODEL), lambda i, j: (0, 0)),
            pl.BlockSpec((D_MODEL, tn), lambda i, j: (0, j)),
            pl.BlockSpec((1, ATT_HEAD_DIM), lambda i, j: (0, 0)),
            pl.BlockSpec((1, ATT_HEAD_DIM), lambda i, j: (0, 0)),
        ] + cast_in,
        out_specs=[pl.BlockSpec((tm, tn), lambda i, j: (i, j))] + cast_out,
        out_shape=[jax.ShapeDtypeStruct((m, n), F32)] + cast_shapes,
        scratch_shapes=[pltpu.VMEM((tm, D_MODEL), BF16)],
        compiler_params=_params(("arbitrary", "arbitrary")),
        name="attn_qkv",
    )(x, g, w, qg, kg, *[c.src for c in casts])
    return qkv, cast


ROW_INTERLEAVE = 4


def _attn_kernel(*refs, seq, cast_blocks):
    nc = len(cast_blocks)
    q_ref, k_ref, v_ref = refs[:3]
    cast_src, o_ref, cast_dst = refs[3:3 + nc], refs[3 + nc], refs[4 + nc:4 + 2 * nc]
    z_ref, acc_ref, den_ref, mx_ref, bias_ref, s_ref = refs[4 + 2 * nc:]
    _cast_step(pl.program_id(0) * pl.num_programs(1) + pl.program_id(1), cast_src, cast_dst, cast_blocks)
    _attn_body(q_ref, k_ref, v_ref, o_ref, z_ref, acc_ref, den_ref, mx_ref, bias_ref, s_ref, seq=seq)


def _attn_body(q_ref, k_ref, v_ref, o_ref, z_ref, acc_ref, den_ref, mx_ref, bias_ref, s_ref, *, seq):
    blk, hd, grp, il = ATT_BLOCK, ATT_HEAD_DIM, ATT_GROUP, ROW_INTERLEAVE
    part = seq // il
    n_chunks = seq // blk
    assert part % blk == 0

    def chunk(c):
        return pl.ds(pl.multiple_of(c * blk, blk), blk)

    def natural_rows(c):
        lo = c // (part // blk)
        b0 = (c % (part // blk)) * blk
        return pl.ds(lo + il * b0, blk, stride=il)

    for ti, src_ref in enumerate((q_ref, k_ref, v_ref)):
        for c in range(n_chunks):
            z_ref[ti, c * blk:(c + 1) * blk, :] = src_ref[natural_rows(c), :]

    qz_ref, kz_ref, vz_ref = z_ref.at[0], z_ref.at[1], z_ref.at[2]
    row = lax.broadcasted_iota(jnp.int32, (blk, 2 * blk), 0)
    col = lax.broadcasted_iota(jnp.int32, (blk, 2 * blk), 1)
    ones = jnp.ones((grp, 2 * blk, hd), BF16)

    for gi, (window, dil) in enumerate(DILATED_PATTERNS):
        assert window // dil == blk and (dil % il == 0 or il % dil == 0)
        pieces = max(il // dil, 1)
        plen = blk // pieces
        stride = max(dil // il, 1)
        n_blocks = seq // (dil * blk)
        assert n_blocks & (n_blocks - 1) == 0 and (dil * n_blocks) % grp == 0 and plen % SUBLANES == 0
        log_nb = n_blocks.bit_length() - 1
        log_plen = plen.bit_length() - 1

        def step_in_block(i, pieces=pieces, plen=plen, log_plen=log_plen):
            return pieces * jnp.bitwise_and(i, plen - 1) + lax.shift_right_logical(i, log_plen)

        dist = step_in_block(row) - step_in_block(jnp.bitwise_and(col, blk - 1)) + jnp.where(col < blk, blk, 0)
        band = jnp.logical_and(dist >= 0, dist <= blk)
        bias_ref[2 * gi + 1] = jnp.where(band, 0.0, -jnp.inf)
        bias_ref[2 * gi] = jnp.where(jnp.logical_and(band, col >= blk), 0.0, -jnp.inf)

        def block_rows(a, n, dil=dil, pieces=pieces, plen=plen, stride=stride):
            if pieces > 1:
                return [pl.ds(pl.multiple_of((p * dil + a) * part + n * plen, SUBLANES), plen) for p in range(pieces)]
            lo = jnp.bitwise_and(a, il - 1)
            hi = lax.shift_right_logical(a, il.bit_length() - 1)
            start = lo * part + hi + stride * blk * n
            if stride == 1:
                return [pl.ds(pl.multiple_of(start, blk), blk)]
            return [pl.ds(start, blk, stride=stride)]

        def load(ref, parts):
            return jnp.concatenate([ref[d, :] for d in parts], axis=0)

        def store(ref, gi, parts, val, plen=plen):
            for p, d in enumerate(parts):
                ref[gi, d, :] = val[p * plen:(p + 1) * plen, :]

        def blocks_of(it, n_blocks=n_blocks, log_nb=log_nb, block_rows=block_rows):
            out = []
            for gg in range(grp):
                idx = jnp.asarray(it * grp + gg, jnp.int32)
                a = lax.shift_right_logical(idx, log_nb)
                n = jnp.bitwise_and(idx, n_blocks - 1)
                out.append((block_rows(a, n), block_rows(a, jnp.maximum(n - 1, 0)), jnp.minimum(n, 1)))
            return out

        def scores(it, gi=gi, blocks_of=blocks_of, load=load):
            qs, ks, bias = [], [], []
            for c, p, has_prev in blocks_of(it):
                qs.append(load(qz_ref, c))
                ks.append(jnp.concatenate([load(kz_ref, p), load(kz_ref, c)], axis=0))
                bias.append(bias_ref[2 * gi + has_prev])
            q = jnp.stack(qs).astype(BF16)
            k = jnp.stack(ks).astype(BF16)
            return jnp.einsum("gqd,gkd->gqk", q, k, preferred_element_type=F32) + jnp.stack(bias)

        def softmax_pv(it, s, gi=gi, blocks_of=blocks_of, load=load, store=store):
            blocks = blocks_of(it)
            vs = [jnp.concatenate([load(vz_ref, p), load(vz_ref, c)], axis=0) for c, p, _ in blocks]
            v = jnp.concatenate([jnp.stack(vs).astype(BF16), ones], axis=-1)
            mx = jnp.max(s, axis=-1, keepdims=True)
            p = jnp.exp(s - mx).astype(BF16)
            pv = jnp.einsum("gqk,gkd->gqd", p, v, preferred_element_type=F32)
            for gg, (c, _, _) in enumerate(blocks):
                store(acc_ref, gi, c, pv[gg, :, :hd])
                store(den_ref, gi, c, pv[gg, :, hd:])
                store(mx_ref, gi, c, jnp.broadcast_to(mx[gg], (blk, hd)))

        n_steps = dil * n_blocks // grp
        s_ref[0] = scores(0)
        for it in range(1, n_steps):
            s_ref[it & 1] = scores(it)
            softmax_pv(it - 1, s_ref[(it - 1) & 1])
        softmax_pv(n_steps - 1, s_ref[(n_steps - 1) & 1])

    nat_ref = z_ref.at[0]

    def combine(c, carry):
        sl = chunk(c)
        m0 = mx_ref[0, sl, :]
        m1 = mx_ref[1, sl, :]
        m2 = mx_ref[2, sl, :]
        mx = jnp.maximum(jnp.maximum(m0, m1), m2)
        w0 = jnp.exp(m0 - mx)
        w1 = jnp.exp(m1 - mx)
        w2 = jnp.exp(m2 - mx)
        num = w0 * acc_ref[0, sl, :] + w1 * acc_ref[1, sl, :] + w2 * acc_ref[2, sl, :]
        den = w0 * den_ref[0, sl, :] + w1 * den_ref[1, sl, :] + w2 * den_ref[2, sl, :]
        nat_ref[natural_rows(c), :] = num / den
        return carry

    lax.fori_loop(0, n_chunks, combine, 0)

    def emit(c, carry):
        o_ref[chunk(c), :] = nat_ref[chunk(c), :].astype(o_ref.dtype)
        return carry

    lax.fori_loop(0, n_chunks, emit, 0, unroll=2)


def _attn_call(qkv, batch, seq, casts):
    qkv = qkv.reshape(batch, seq, 3 * D_MODEL)
    hd = ATT_HEAD_DIM
    n_groups = len(DILATED_PATTERNS)
    cast_in, cast_out, cast_shapes, cast_blocks = _cast_plan(casts, lambda b, h: b * ATT_HEADS + h)
    out, *cast = pl.pallas_call(
        functools.partial(_attn_kernel, seq=seq, cast_blocks=cast_blocks),
        grid=(batch, ATT_HEADS),
        in_specs=[
            pl.BlockSpec((None, seq, hd), lambda b, h: (b, 0, h)),
            pl.BlockSpec((None, seq, hd), lambda b, h: (b, 0, ATT_HEADS + h)),
            pl.BlockSpec((None, seq, hd), lambda b, h: (b, 0, 2 * ATT_HEADS + h)),
        ] + cast_in,
        out_specs=[pl.BlockSpec((None, seq, hd), lambda b, h: (b, 0, h))] + cast_out,
        out_shape=[jax.ShapeDtypeStruct((batch, seq, D_MODEL), BF16)] + cast_shapes,
        scratch_shapes=[
            pltpu.VMEM((3, seq, hd), F32),
            pltpu.VMEM((n_groups, seq, hd), F32),
            pltpu.VMEM((n_groups, seq, hd), F32),
            pltpu.VMEM((n_groups, seq, hd), F32),
            pltpu.VMEM((2 * n_groups, ATT_BLOCK, 2 * ATT_BLOCK), F32),
            pltpu.VMEM((2, ATT_GROUP, ATT_BLOCK, 2 * ATT_BLOCK), F32),
        ],
        compiler_params=_params(("arbitrary", "arbitrary")),
        name="dilated_attn",
    )(qkv, qkv, qkv, *[c.src for c in casts])
    return out.reshape(batch * seq, D_MODEL), cast


def _proj_kernel(a_ref, w_ref, x_ref, o_ref, *, rows):
    for c in range(o_ref.shape[0] // rows):
        sl = slice(c * rows, (c + 1) * rows)
        o_ref[sl, :] = x_ref[sl, :] + _dot(a_ref[sl, :], w_ref[...])


def _proj_call(a, w, x, *, tm=512, rows=256):
    m, k = a.shape
    n = w.shape[1]
    return pl.pallas_call(
        functools.partial(_proj_kernel, rows=rows),
        grid=(m // tm,),
        in_specs=[
            pl.BlockSpec((tm, k), lambda i: (i, 0)),
            pl.BlockSpec((k, n), lambda i: (0, 0)),
            pl.BlockSpec((tm, n), lambda i: (i, 0)),
        ],
        out_specs=pl.BlockSpec((tm, n), lambda i: (i, 0)),
        out_shape=jax.ShapeDtypeStruct((m, n), F32),
        compiler_params=_params(("parallel",)),
        name="proj_residual",
    )(a, w, x)


def _fill_normed(h_ref, x_ref, xh_ref, g_ref, seq_start, copy_ref=None):
    g = g_ref[...]
    halo = jnp.where(seq_start, 0.0, _rms(xh_ref[...], g))
    h_ref[0:HALO, :] = halo.astype(BF16)
    _norm_rows(h_ref, HALO, x_ref, g, copy_ref)


def _causal_conv(u_ref, c, w_ref, b_ref, taps, rows):
    w = w_ref[...]
    y = b_ref[...]
    for j in range(taps):
        off = c * rows + HALO - (taps - 1) + j
        y = y + w[j:j + 1, :] * u_ref[off:off + rows, :]
    return y


def _ffn_kernel(x_ref, xh_ref, g_ref, wg_ref, wu_ref, cwg_ref, cwu_ref, cbg_ref, cbu_ref, wd_ref,
                o_ref, h_ref, ug_ref, uu_ref, *, rows, tiles_per_seq):
    i = pl.program_id(0)
    j = pl.program_id(1)

    @pl.when(j == 0)
    def _():
        _fill_normed(h_ref, x_ref, xh_ref, g_ref, i % tiles_per_seq == 0, copy_ref=o_ref)

    def down_proj(c):
        gate = _causal_conv(ug_ref, c, cwg_ref, cbg_ref, FFN_CONV, rows)
        up = _causal_conv(uu_ref, c, cwu_ref, cbu_ref, FFN_CONV, rows)
        act = (gate * up / (1.0 + jnp.exp(-gate))).astype(BF16)
        o_ref[c * rows:(c + 1) * rows, :] += _dot(act, wd_ref[...])

    _row_chunks_conv(h_ref, [wg_ref, wu_ref], [ug_ref, uu_ref], rows, down_proj)


def _ffn_call(x, g, w_up, conv_w, conv_b, w_down, layer, seq, *, tm=1024, tf=512, rows=512):
    m = x.shape[0]
    nf = FFN_DIM // tf
    halo_blocks_per_tile = tm // HALO
    return pl.pallas_call(
        functools.partial(_ffn_kernel, rows=rows, tiles_per_seq=seq // tm),
        grid=(m // tm, nf),
        in_specs=[
            pl.BlockSpec((tm, D_MODEL), lambda i, j: (i, 0)),
            pl.BlockSpec((HALO, D_MODEL), lambda i, j: (jnp.maximum(i * halo_blocks_per_tile - 1, 0), 0)),
            pl.BlockSpec((None, 1, D_MODEL), lambda i, j: (layer, 0, 0)),
            pl.BlockSpec((D_MODEL, tf), lambda i, j: (0, j)),
            pl.BlockSpec((D_MODEL, tf), lambda i, j: (0, nf + j)),
            pl.BlockSpec((None, FFN_CONV, tf), lambda i, j: (layer, 0, j)),
            pl.BlockSpec((None, FFN_CONV, tf), lambda i, j: (layer, 0, nf + j)),
            pl.BlockSpec((None, 1, tf), lambda i, j: (layer, 0, j)),
            pl.BlockSpec((None, 1, tf), lambda i, j: (layer, 0, nf + j)),
            pl.BlockSpec((tf, D_MODEL), lambda i, j: (j, 0)),
        ],
        out_specs=pl.BlockSpec((tm, D_MODEL), lambda i, j: (i, 0)),
        out_shape=jax.ShapeDtypeStruct((m, D_MODEL), F32),
        scratch_shapes=[
            pltpu.VMEM((tm + HALO, D_MODEL), BF16),
            pltpu.VMEM((tm + HALO, tf), F32),
            pltpu.VMEM((tm + HALO, tf), F32),
        ],
        compiler_params=_params(("parallel", "arbitrary")),
        name="conv_ffn",
    )(x, x, g, w_up, w_up, conv_w, conv_w, conv_b, conv_b, w_down)


def _lstm_in_kernel(x_ref, xh_ref, g_ref, w_ref, wgate_ref, bgate_ref, cw_ref, cb_ref,
                    z_ref, gates_ref, h_ref, u_ref, *, rows, tiles_per_seq, n_conv_blocks):
    i = pl.program_id(0)
    j = pl.program_id(1)

    @pl.when(j == 0)
    def _():
        _fill_normed(h_ref, x_ref, xh_ref, g_ref, i % tiles_per_seq == 0)
        wg = wgate_ref[...]
        wg = jnp.concatenate([wg, jnp.zeros((LANES - wg.shape[0], wg.shape[1]), F32)], axis=0).astype(BF16)
        gates_ref[...] = _dot_nt(h_ref[HALO:, :], wg) + bgate_ref[...]

    @pl.when(j < n_conv_blocks)
    def _():
        scale = jnp.where(j < n_conv_blocks // 2, 1.0, LSTM_QK_DIM ** -0.5)

        def conv_silu(c):
            y = _causal_conv(u_ref, c, cw_ref, cb_ref, LSTM_CONV, rows)
            z_ref[c * rows:(c + 1) * rows, :] = y * scale / (1.0 + jnp.exp(-y))

        _row_chunks_conv(h_ref, [w_ref], [u_ref], rows, conv_silu, dot=_dot_nt)

    @pl.when(j >= n_conv_blocks)
    def _():
        def copy(c, us):
            z_ref[c * rows:(c + 1) * rows, :] = us[0]

        _row_chunks(h_ref, [w_ref], rows, HALO, False, copy, dot=_dot_nt)


def _lstm_in_call(x, g, w_main_t, w_in_t, b_gate, conv_w, conv_b, seq, *, tm=1024, tn=1024, rows=256):
    m = x.shape[0]
    n = LSTM_MAIN_WIDTH
    n_gates = 2 * LSTM_HEADS
    assert w_in_t.shape[0] == n + n_gates and n % n_gates == 0 and n_gates == SUBLANES
    n_conv_blocks = 2 * LSTM_QK_WIDTH // tn
    halo_blocks_per_tile = tm // HALO
    return pl.pallas_call(
        functools.partial(_lstm_in_kernel, rows=rows, tiles_per_seq=seq // tm, n_conv_blocks=n_conv_blocks),
        grid=(m // tm, n // tn),
        in_specs=[
            pl.BlockSpec((tm, D_MODEL), lambda i, j: (i, 0)),
            pl.BlockSpec((HALO, D_MODEL), lambda i, j: (jnp.maximum(i * halo_blocks_per_tile - 1, 0), 0)),
            pl.BlockSpec((1, D_MODEL), lambda i, j: (0, 0)),
            pl.BlockSpec((tn, D_MODEL), lambda i, j: (j, 0)),
            pl.BlockSpec((n_gates, D_MODEL), lambda i, j: (n // n_gates, 0)),
            pl.BlockSpec((1, LANES), lambda i, j: (0, 0)),
            pl.BlockSpec((LSTM_CONV, tn), lambda i, j: (0, jnp.minimum(j, n_conv_blocks - 1))),
            pl.BlockSpec((1, tn), lambda i, j: (0, jnp.minimum(j, n_conv_blocks - 1))),
        ],
        out_specs=[
            pl.BlockSpec((tm, tn), lambda i, j: (i, j)),
            pl.BlockSpec((tm, LANES), lambda i, j: (i, 0)),
        ],
        out_shape=[
            jax.ShapeDtypeStruct((m, n), F32),
            jax.ShapeDtypeStruct((m, LANES), F32),
        ],
        scratch_shapes=[
            pltpu.VMEM((tm + HALO, D_MODEL), BF16),
            pltpu.VMEM((tm + HALO, tn), F32),
        ],
        compiler_params=_params(("parallel", "arbitrary")),
        name="lstm_in",
    )(x, x, g, w_main_t, w_in_t, b_gate, conv_w, conv_b)


def _log_sigmoid(x):
    return -(jnp.maximum(-x, 0.0) + jnp.log1p(jnp.exp(-jnp.abs(x))))


def _cumsum_rows(x):
    n = x.shape[0]
    row = lax.broadcasted_iota(jnp.int32, x.shape, 0)
    shift = 1
    while shift < n:
        x = x + jnp.where(row >= shift, pltpu.roll(x, shift, axis=0), 0.0)
        shift *= 2
    return x


def _mlstm_kernel(q_ref, k_ref, v_ref, og_ref, gates_ref, hg_ref, wo_ref, x_ref, o_ref, c_ref, n_ref, m_ref, hs_ref):
    chunk = q_ref.shape[0]
    dk, dv, heads = LSTM_QK_DIM, LSTM_V_DIM, LSTM_HEADS

    @pl.when(pl.program_id(1) == 0)
    def _():
        c_ref[...] = jnp.zeros_like(c_ref)
        n_ref[...] = jnp.zeros_like(n_ref)
        m_ref[...] = jnp.zeros_like(m_ref)
        hs_ref[...] = jnp.zeros_like(hs_ref)

    o_ref[...] = x_ref[...] + _dot(hs_ref[...], wo_ref[...])

    gates = gates_ref[...]
    cum_f = _cumsum_rows(_log_sigmoid(gates))
    gates_t = gates.T
    cum_f_t = cum_f.T
    row = lax.broadcasted_iota(jnp.int32, (chunk, chunk), 0)
    col = lax.broadcasted_iota(jnp.int32, (chunk, chunk), 1)
    causal = col <= row

    for hd in range(heads):
        qs = slice(hd * dk, (hd + 1) * dk)
        vs = slice(hd * dv, (hd + 1) * dv)
        b_col = cum_f[:, heads + hd:heads + hd + 1]
        b_row = cum_f_t[heads + hd:heads + hd + 1, :]
        li_col = gates[:, hd:hd + 1]
        li_row = gates_t[hd:hd + 1, :]
        m_prev = m_ref[hd]
        c_prev = c_ref[hd]
        n_prev = n_ref[hd]
        q = q_ref[:, qs]
        k = k_ref[:, qs]
        v = v_ref[:, vs]
        qb = q.astype(BF16)
        kb = k.astype(BF16)

        dmat = jnp.where(causal, b_col - b_row + li_row, -jnp.inf)
        g = b_col + m_prev
        m_t = jnp.maximum(g, jnp.max(dmat, axis=-1, keepdims=True))
        p = jnp.exp(dmat - m_t)
        inter = jnp.exp(g - m_t)
        w = p * _dot_nt(qb, kb)
        num = inter * _dot(qb, c_prev.astype(BF16)) + _dot(w.astype(BF16), v.astype(BF16))
        den = inter * jnp.sum(q * n_prev, axis=-1, keepdims=True) + jnp.sum(w, axis=-1, keepdims=True)
        h = num / jnp.maximum(jnp.abs(den), jnp.exp(-m_t))

        b_last = b_col[chunk - 1:chunk, :]
        a = b_last - b_col + li_col
        m_new = jnp.maximum(b_last + m_prev, jnp.max(a, axis=0, keepdims=True))
        decay = jnp.exp(b_last + m_prev - m_new)
        wts = jnp.exp(a - m_new)
        c_ref[hd] = decay * c_prev + _dot_tn(kb, (wts * v).astype(BF16))
        n_ref[hd] = decay * n_prev + jnp.sum(wts * k, axis=0, keepdims=True)
        m_ref[hd] = m_new

        hs = _rms(h, hg_ref[:, vs]) * jax.nn.sigmoid(og_ref[:, vs])
        hs_ref[:, vs] = hs.astype(hs_ref.dtype)


def _mlstm_call(z, gates, head_gain, w_out, x, batch, seq):
    chunk = LSTM_CHUNK
    nc = seq // chunk
    z = z.reshape(batch, seq, LSTM_MAIN_WIDTH)
    gates = gates.reshape(batch, seq, LANES)
    qkw = LSTM_QK_WIDTH

    def cur(c):
        return jnp.minimum(c, nc - 1)

    def lagged(b, c):
        return b * nc + jnp.maximum(c - 1, 0)

    return pl.pallas_call(
        _mlstm_kernel,
        grid=(batch, nc + 1),
        in_specs=[
            pl.BlockSpec((None, chunk, qkw), lambda b, c: (b, cur(c), 0)),
            pl.BlockSpec((None, chunk, qkw), lambda b, c: (b, cur(c), 1)),
            pl.BlockSpec((None, chunk, D_MODEL), lambda b, c: (b, cur(c), 2 * qkw // D_MODEL)),
            pl.BlockSpec((None, chunk, D_MODEL), lambda b, c: (b, cur(c), 2 * qkw // D_MODEL + 1)),
            pl.BlockSpec((None, chunk, LANES), lambda b, c: (b, cur(c), 0)),
            pl.BlockSpec((1, D_MODEL), lambda b, c: (0, 0)),
            pl.BlockSpec((D_MODEL, D_MODEL), lambda b, c: (0, 0), pipeline_mode=pl.Buffered(1)),
            pl.BlockSpec((chunk, D_MODEL), lambda b, c: (lagged(b, c), 0)),
        ],
        out_specs=pl.BlockSpec((chunk, D_MODEL), lambda b, c: (lagged(b, c), 0)),
        out_shape=jax.ShapeDtypeStruct((batch * seq, D_MODEL), F32),
        scratch_shapes=[
            pltpu.VMEM((LSTM_HEADS, LSTM_QK_DIM, LSTM_V_DIM), F32),
            pltpu.VMEM((LSTM_HEADS, 1, LSTM_QK_DIM), F32),
            pltpu.VMEM((LSTM_HEADS, 1, 1), F32),
            pltpu.VMEM((chunk, D_MODEL), BF16),
        ],
        compiler_params=_params(("arbitrary", "arbitrary")),
        name="mlstm_proj",
    )(z, z, z, z, gates, head_gain, w_out, x)


def kernel(x, attn_norm, attn_w_qkv, attn_q_gain, attn_k_gain, attn_w_o, lstm_norm, lstm_w_in, lstm_gate_bias,
           lstm_conv_w, lstm_conv_b, lstm_head_gain, lstm_w_out, ffn_norm, ffn_w_up, ffn_conv_w, ffn_conv_b,
           ffn_w_down):
    batch, seq, d = x.shape
    assert d == D_MODEL and ffn_norm.shape[0] == 2 and attn_norm.shape[0] == 1 and lstm_norm.shape[0] == 1
    m = batch * seq
    xf = x.reshape(m, d)

    def ffn(xin, layer, w_up, w_down):
        return _ffn_call(xin, ffn_norm[:, None, :], w_up, ffn_conv_w, ffn_conv_b[:, None, :], w_down, layer, seq)

    w_in_t = jnp.swapaxes(lstm_w_in[0], 0, 1)
    qkv, (w_up0, w_down0, w_o, w_in_main_t) = _qkv_call(
        xf, attn_norm[0][None], attn_w_qkv[0].astype(BF16), attn_q_gain[0][None], attn_k_gain[0][None],
        [_Cast(ffn_w_up, 0, 64), _Cast(ffn_w_down, 0, 128), _Cast(attn_w_o, 0, 64),
         _Cast(w_in_t, None, 128, LSTM_MAIN_WIDTH)])
    att, (w_up1, w_down1, w_out) = _attn_call(
        qkv, batch, seq, [_Cast(ffn_w_up, 1, 64), _Cast(ffn_w_down, 1, 176), _Cast(lstm_w_out, 0, 64)])
    xf = _proj_call(att, w_o, xf)
    xf = ffn(xf, 0, w_up0, w_down0)

    n_gates = 2 * LSTM_HEADS
    b_gate = jnp.pad(lstm_gate_bias[0], (0, LANES - n_gates))[None]
    z, gates = _lstm_in_call(xf, lstm_norm[0][None], w_in_main_t, w_in_t, b_gate,
                             lstm_conv_w[0], lstm_conv_b[0][None], seq)
    xf = _mlstm_call(z, gates, lstm_head_gain[0][None], w_out, xf, batch, seq)
    xf = ffn(xf, 1, w_up1, w_down1)
    return xf.reshape(batch, seq, d)
```

```python
import functools
from typing import NamedTuple, Optional

import jax
import jax.numpy as jnp
from jax import lax
from jax.experimental import pallas as pl
from jax.experimental.pallas import tpu as pltpu

F32 = jnp.float32
BF16 = jnp.bfloat16

D_MODEL = 2048
ATT_HEADS = 16
ATT_HEAD_DIM = D_MODEL // ATT_HEADS
DILATED_PATTERNS = ((128, 1), (512, 4), (2048, 16))
ATT_BLOCK = 128
ATT_GROUP = 4
LSTM_HEADS = 4
LSTM_V_DIM = D_MODEL // LSTM_HEADS
LSTM_QK_DIM = LSTM_V_DIM // 2
LSTM_QK_WIDTH = LSTM_HEADS * LSTM_QK_DIM
LSTM_MAIN_WIDTH = 2 * LSTM_QK_WIDTH + 2 * D_MODEL
LSTM_CONV = 4
FFN_DIM = ((8 * D_MODEL // 3 + 255) // 256) * 256
FFN_CONV = 3
NORM_EPS = 1e-6

LANES = 128
SUBLANES = 8
BF16_ROWS_PER_VREG = 2 * SUBLANES
HALO = BF16_ROWS_PER_VREG
VMEM_LIMIT = 56 * 1024 * 1024

LSTM_CHUNK = 256

NORM_ROWS = 128


def _rms(x, g):
    ms = jnp.mean(x * x, axis=-1, keepdims=True)
    return x * lax.rsqrt(ms + NORM_EPS) * g


def _norm_rows(h_ref, lead, x_ref, g, copy_ref=None):
    def body(r, carry):
        start = pl.multiple_of(r * NORM_ROWS, NORM_ROWS)
        x = x_ref[pl.ds(start, NORM_ROWS), :]
        h_ref[pl.ds(pl.multiple_of(start + lead, BF16_ROWS_PER_VREG), NORM_ROWS), :] = _rms(x, g).astype(BF16)
        if copy_ref is not None:
            copy_ref[pl.ds(start, NORM_ROWS), :] = x
        return carry

    lax.fori_loop(0, x_ref.shape[0] // NORM_ROWS, body, 0)


def _dot(a, b):
    return jnp.dot(a, b, preferred_element_type=F32)


def _dot_nt(a, b):
    return lax.dot_general(a, b, (((1,), (1,)), ((), ())), preferred_element_type=F32)


def _dot_tn(a, b):
    return lax.dot_general(a, b, (((0,), (0,)), ((), ())), preferred_element_type=F32)


def _params(sem):
    return pltpu.CompilerParams(dimension_semantics=sem, vmem_limit_bytes=VMEM_LIMIT)


class _Cast(NamedTuple):
    src: jax.Array
    layer: Optional[int]
    rows: int
    total_rows: Optional[int] = None


def _cast_plan(casts, step_of):
    in_specs, out_specs, out_shapes, n_blocks = [], [], [], []
    for c in casts:
        r, cols = c.src.shape[-2:]
        r = c.total_rows or r
        assert r % c.rows == 0 and c.rows % BF16_ROWS_PER_VREG == 0
        nb = r // c.rows

        def block(*ids, nb=nb):
            return jnp.minimum(step_of(*ids), nb - 1)

        if c.layer is None:
            in_specs.append(pl.BlockSpec((c.rows, cols), lambda *ids, block=block: (block(*ids), 0)))
        else:
            in_specs.append(pl.BlockSpec((None, c.rows, cols),
                                         lambda *ids, block=block, layer=c.layer: (layer, block(*ids), 0)))
        out_specs.append(pl.BlockSpec((c.rows, cols), lambda *ids, block=block: (block(*ids), 0)))
        out_shapes.append(jax.ShapeDtypeStruct((r, cols), BF16))
        n_blocks.append(nb)
    return in_specs, out_specs, out_shapes, tuple(n_blocks)


def _cast_step(step, src_refs, dst_refs, n_blocks):
    for src, dst, nb in zip(src_refs, dst_refs, n_blocks):
        @pl.when(step < nb)
        def _(src=src, dst=dst):
            for r0 in range(0, src.shape[0], BF16_ROWS_PER_VREG):
                sl = slice(r0, r0 + BF16_ROWS_PER_VREG)
                dst[sl, :] = src[sl, :].astype(BF16)


def _cast_part(src_refs, dst_refs, part, n_parts):
    for src, dst in zip(src_refs, dst_refs):
        rows = src.shape[0] // n_parts
        assert rows % BF16_ROWS_PER_VREG == 0
        for r0 in range(part * rows, (part + 1) * rows, BF16_ROWS_PER_VREG):
            sl = slice(r0, r0 + BF16_ROWS_PER_VREG)
            dst[sl, :] = src[sl, :].astype(BF16)


def _row_chunks(h_ref, w_refs, rows, lead, with_halo, consume, dot=_dot):
    n_chunks = (h_ref.shape[0] - lead) // rows
    assert not with_halo or lead == HALO

    def project(c, tails):
        if with_halo and c == 0:
            h = h_ref[0:rows + lead, :]
            return [dot(h, w[...]) for w in w_refs]
        h = h_ref[c * rows + lead:(c + 1) * rows + lead, :]
        us = [dot(h, w[...]) for w in w_refs]
        if with_halo:
            us = [jnp.concatenate([t, u], axis=0) for t, u in zip(tails, us)]
        return us

    us = project(0, None)
    for c in range(n_chunks):
        nxt = project(c + 1, [u[rows:, :] for u in us]) if c + 1 < n_chunks else None
        consume(c, us)
        us = nxt


def _row_chunks_conv(h_ref, w_refs, u_refs, rows, consume, dot=_dot):
    n_chunks = (h_ref.shape[0] - HALO) // rows

    def project(c):
        lo = 0 if c == 0 else c * rows + HALO
        hi = (c + 1) * rows + HALO
        h = h_ref[lo:hi, :]
        for w, u in zip(w_refs, u_refs):
            u[lo:hi, :] = dot(h, w[...])

    project(0)
    for c in range(n_chunks):
        if c + 1 < n_chunks:
            project(c + 1)
        consume(c)


def _qkv_kernel(*refs, rows, n_head_blocks, heads_per_block, cast_blocks):
    nc = len(cast_blocks)
    x_ref, g_ref, w_ref, qg_ref, kg_ref = refs[:5]
    cast_src, o_ref, cast_dst, h_ref = refs[5:5 + nc], refs[5 + nc], refs[6 + nc:6 + 2 * nc], refs[6 + 2 * nc]
    j = pl.program_id(1)
    n_chunks = h_ref.shape[0] // rows

    @pl.when(j == 0)
    def _():
        _norm_rows(h_ref, 0, x_ref, g_ref[...])

    @pl.when(j < 2 * n_head_blocks)
    def _():
        gain = jnp.where(j < n_head_blocks, qg_ref[...] * (ATT_HEAD_DIM ** -0.5), kg_ref[...])

        def qk_norm(c, us):
            _cast_part(cast_src, cast_dst, c, n_chunks)
            for hh in range(heads_per_block):
                sl = slice(hh * ATT_HEAD_DIM, (hh + 1) * ATT_HEAD_DIM)
                o_ref[c * rows:(c + 1) * rows, sl] = _rms(us[0][:, sl], gain)

        _row_chunks(h_ref, [w_ref], rows, 0, False, qk_norm)

    @pl.when(j >= 2 * n_head_blocks)
    def _():
        def copy(c, us):
            _cast_part(cast_src, cast_dst, c, n_chunks)
            o_ref[c * rows:(c + 1) * rows, :] = us[0]

        _row_chunks(h_ref, [w_ref], rows, 0, False, copy)


def _qkv_call(x, g, w, qg, kg, casts, *, tm=1024, tn=1024, rows=256):
    m = x.shape[0]
    n = w.shape[1]
    heads_per_block = tn // ATT_HEAD_DIM
    n_head_blocks = D_MODEL // tn
    n_col_blocks = n // tn
    cast_in, cast_out, cast_shapes, cast_blocks = _cast_plan(casts, lambda i, j: i * n_col_blocks + j)
    qkv, *cast = pl.pallas_call(
        functools.partial(_qkv_kernel, rows=rows, n_head_blocks=n_head_blocks, heads_per_block=heads_per_block,
                          cast_blocks=cast_blocks),
        grid=(m // tm, n_col_blocks),
        in_specs=[
            pl.BlockSpec((tm, D_MODEL), lambda i, j: (i, 0)),
            pl.BlockSpec((1, D_MODEL), lambda i, j: (0, 0)),
            pl.BlockSpec((D_MODEL, tn), lambda i, j: (0, j)),
            pl.BlockSpec((1, ATT_HEAD_DIM), lambda i, j: (0, 0)),
            pl.BlockSpec((1, ATT_HEAD_DIM), lambda i, j: (0, 0)),
        ] + cast_in,
        out_specs=[pl.BlockSpec((tm, tn), lambda i, j: (i, j))] + cast_out,
        out_shape=[jax.ShapeDtypeStruct((m, n), F32)] + cast_shapes,
        scratch_shapes=[pltpu.VMEM((tm, D_MODEL), BF16)],
        compiler_params=_params(("arbitrary", "arbitrary")),
        name="attn_qkv",
    )(x, g, w, qg, kg, *[c.src for c in casts])
    return qkv, cast


ROW_INTERLEAVE = 4


def _attn_kernel(*refs, seq, cast_blocks):
    nc = len(cast_blocks)
    q_ref, k_ref, v_ref = refs[:3]
    cast_src, o_ref, cast_dst = refs[3:3 + nc], refs[3 + nc], refs[4 + nc:4 + 2 * nc]
    z_ref, acc_ref, den_ref, mx_ref, bias_ref, s_ref = refs[4 + 2 * nc:]
    _cast_step(pl.program_id(0) * pl.num_programs(1) + pl.program_id(1), cast_src, cast_dst, cast_blocks)
    _attn_body(q_ref, k_ref, v_ref, o_ref, z_ref, acc_ref, den_ref, mx_ref, bias_ref, s_ref, seq=seq)


def _attn_body(q_ref, k_ref, v_ref, o_ref, z_ref, acc_ref, den_ref, mx_ref, bias_ref, s_ref, *, seq):
    blk, hd, grp, il = ATT_BLOCK, ATT_HEAD_DIM, ATT_GROUP, ROW_INTERLEAVE
    part = seq // il
    n_chunks = seq // blk
    assert part % blk == 0

    def chunk(c):
        return pl.ds(pl.multiple_of(c * blk, blk), blk)

    def natural_rows(c):
        lo = c // (part // blk)
        b0 = (c % (part // blk)) * blk
        return pl.ds(lo + il * b0, blk, stride=il)

    for ti, src_ref in enumerate((q_ref, k_ref, v_ref)):
        for c in range(n_chunks):
            z_ref[ti, c * blk:(c + 1) * blk, :] = src_ref[natural_rows(c), :]

    qz_ref, kz_ref, vz_ref = z_ref.at[0], z_ref.at[1], z_ref.at[2]
    row = lax.broadcasted_iota(jnp.int32, (blk, 2 * blk), 0)
    col = lax.broadcasted_iota(jnp.int32, (blk, 2 * blk), 1)
    ones = jnp.ones((grp, 2 * blk, hd), BF16)

    for gi, (window, dil) in enumerate(DILATED_PATTERNS):
        assert window // dil == blk and (dil % il == 0 or il % dil == 0)
        pieces = max(il // dil, 1)
        plen = blk // pieces
        stride = max(dil // il, 1)
        n_blocks = seq // (dil * blk)
        assert n_blocks & (n_blocks - 1) == 0 and (dil * n_blocks) % grp == 0 and plen % SUBLANES == 0
        log_nb = n_blocks.bit_length() - 1
        log_plen = plen.bit_length() - 1

        def step_in_block(i, pieces=pieces, plen=plen, log_plen=log_plen):
            return pieces * jnp.bitwise_and(i, plen - 1) + lax.shift_right_logical(i, log_plen)

        dist = step_in_block(row) - step_in_block(jnp.bitwise_and(col, blk - 1)) + jnp.where(col < blk, blk, 0)
        band = jnp.logical_and(dist >= 0, dist <= blk)
        bias_ref[2 * gi + 1] = jnp.where(band, 0.0, -jnp.inf)
        bias_ref[2 * gi] = jnp.where(jnp.logical_and(band, col >= blk), 0.0, -jnp.inf)

        def block_rows(a, n, dil=dil, pieces=pieces, plen=plen, stride=stride):
            if pieces > 1:
                return [pl.ds(pl.multiple_of((p * dil + a) * part + n * plen, SUBLANES), plen) for p in range(pieces)]
            lo = jnp.bitwise_and(a, il - 1)
            hi = lax.shift_right_logical(a, il.bit_length() - 1)
            start = lo * part + hi + stride * blk * n
            if stride == 1:
                return [pl.ds(pl.multiple_of(start, blk), blk)]
            return [pl.ds(start, blk, stride=stride)]

        def load(ref, parts):
            return jnp.concatenate([ref[d, :] for d in parts], axis=0)

        def store(ref, gi, parts, val, plen=plen):
            for p, d in enumerate(parts):
                ref[gi, d, :] = val[p * plen:(p + 1) * plen, :]

        def blocks_of(it, n_blocks=n_blocks, log_nb=log_nb, block_rows=block_rows):
            out = []
            for gg in range(grp):
                idx = jnp.asarray(it * grp + gg, jnp.int32)
                a = lax.shift_right_logical(idx, log_nb)
                n = jnp.bitwise_and(idx, n_blocks - 1)
                out.append((block_rows(a, n), block_rows(a, jnp.maximum(n - 1, 0)), jnp.minimum(n, 1)))
            return out

        def scores(it, gi=gi, blocks_of=blocks_of, load=load):
            qs, ks, bias = [], [], []
            for c, p, has_prev in blocks_of(it):
                qs.append(load(qz_ref, c))
                ks.append(jnp.concatenate([load(kz_ref, p), load(kz_ref, c)], axis=0))
                bias.append(bias_ref[2 * gi + has_prev])
            q = jnp.stack(qs).astype(BF16)
            k = jnp.stack(ks).astype(BF16)
            return jnp.einsum("gqd,gkd->gqk", q, k, preferred_element_type=F32) + jnp.stack(bias)

        def softmax_pv(it, s, gi=gi, blocks_of=blocks_of, load=load, store=store):
            blocks = blocks_of(it)
            vs = [jnp.concatenate([load(vz_ref, p), load(vz_ref, c)], axis=0) for c, p, _ in blocks]
            v = jnp.concatenate([jnp.stack(vs).astype(BF16), ones], axis=-1)
            mx = jnp.max(s, axis=-1, keepdims=True)
            p = jnp.exp(s - mx).astype(BF16)
            pv = jnp.einsum("gqk,gkd->gqd", p, v, preferred_element_type=F32)
            for gg, (c, _, _) in enumerate(blocks):
                store(acc_ref, gi, c, pv[gg, :, :hd])
                store(den_ref, gi, c, pv[gg, :, hd:])
                store(mx_ref, gi, c, jnp.broadcast_to(mx[gg], (blk, hd)))

        n_steps = dil * n_blocks // grp
        s_ref[0] = scores(0)
        for it in range(1, n_steps):
            s_ref[it & 1] = scores(it)
            softmax_pv(it - 1, s_ref[(it - 1) & 1])
        softmax_pv(n_steps - 1, s_ref[(n_steps - 1) & 1])

    nat_ref = z_ref.at[0]

    def combine(c, carry):
        sl = chunk(c)
        m0 = mx_ref[0, sl, :]
        m1 = mx_ref[1, sl, :]
        m2 = mx_ref[2, sl, :]
        mx = jnp.maximum(jnp.maximum(m0, m1), m2)
        w0 = jnp.exp(m0 - mx)
        w1 = jnp.exp(m1 - mx)
        w2 = jnp.exp(m2 - mx)
        num = w0 * acc_ref[0, sl, :] + w1 * acc_ref[1, sl, :] + w2 * acc_ref[2, sl, :]
        den = w0 * den_ref[0, sl, :] + w1 * den_ref[1, sl, :] + w2 * den_ref[2, sl, :]
        nat_ref[natural_rows(c), :] = num / den
        return carry

    lax.fori_loop(0, n_chunks, combine, 0)

    def emit(c, carry):
        o_ref[chunk(c), :] = nat_ref[chunk(c), :].astype(o_ref.dtype)
        return carry

    lax.fori_loop(0, n_chunks, emit, 0, unroll=2)


def _attn_call(qkv, batch, seq, casts):
    qkv = qkv.reshape(batch, seq, 3 * D_MODEL)
    hd = ATT_HEAD_DIM
    n_groups = len(DILATED_PATTERNS)
    cast_in, cast_out, cast_shapes, cast_blocks = _cast_plan(casts, lambda b, h: b * ATT_HEADS + h)
    out, *cast = pl.pallas_call(
        functools.partial(_attn_kernel, seq=seq, cast_blocks=cast_blocks),
        grid=(batch, ATT_HEADS),
        in_specs=[
            pl.BlockSpec((None, seq, hd), lambda b, h: (b, 0, h)),
            pl.BlockSpec((None, seq, hd), lambda b, h: (b, 0, ATT_HEADS + h)),
            pl.BlockSpec((None, seq, hd), lambda b, h: (b, 0, 2 * ATT_HEADS + h)),
        ] + cast_in,
        out_specs=[pl.BlockSpec((None, seq, hd), lambda b, h: (b, 0, h))] + cast_out,
        out_shape=[jax.ShapeDtypeStruct((batch, seq, D_MODEL), BF16)] + cast_shapes,
        scratch_shapes=[
            pltpu.VMEM((3, seq, hd), F32),
            pltpu.VMEM((n_groups, seq, hd), F32),
            pltpu.VMEM((n_groups, seq, hd), F32),
            pltpu.VMEM((n_groups, seq, hd), F32),
            pltpu.VMEM((2 * n_groups, ATT_BLOCK, 2 * ATT_BLOCK), F32),
            pltpu.VMEM((2, ATT_GROUP, ATT_BLOCK, 2 * ATT_BLOCK), F32),
        ],
        compiler_params=_params(("arbitrary", "arbitrary")),
        name="dilated_attn",
    )(qkv, qkv, qkv, *[c.src for c in casts])
    return out.reshape(batch * seq, D_MODEL), cast


def _proj_kernel(a_ref, w_ref, x_ref, o_ref, *, rows):
    for c in range(o_ref.shape[0] // rows):
        sl = slice(c * rows, (c + 1) * rows)
        o_ref[sl, :] = x_ref[sl, :] + _dot(a_ref[sl, :], w_ref[...])


def _proj_call(a, w, x, *, tm=512, rows=256):
    m, k = a.shape
    n = w.shape[1]
    return pl.pallas_call(
        functools.partial(_proj_kernel, rows=rows),
        grid=(m // tm,),
        in_specs=[
            pl.BlockSpec((tm, k), lambda i: (i, 0)),
            pl.BlockSpec((k, n), lambda i: (0, 0)),
            pl.BlockSpec((tm, n), lambda i: (i, 0)),
        ],
        out_specs=pl.BlockSpec((tm, n), lambda i: (i, 0)),
        out_shape=jax.ShapeDtypeStruct((m, n), F32),
        compiler_params=_params(("parallel",)),
        name="proj_residual",
    )(a, w, x)


def _fill_normed(h_ref, x_ref, xh_ref, g_ref, seq_start, copy_ref=None):
    g = g_ref[...]
    halo = jnp.where(seq_start, 0.0, _rms(xh_ref[...], g))
    h_ref[0:HALO, :] = halo.astype(BF16)
    _norm_rows(h_ref, HALO, x_ref, g, copy_ref)


def _causal_conv(u_ref, c, w_ref, b_ref, taps, rows):
    w = w_ref[...]
    y = b_ref[...]
    for j in range(taps):
        off = c * rows + HALO - (taps - 1) + j
        y = y + w[j:j + 1, :] * u_ref[off:off + rows, :]
    return y


def _ffn_kernel(x_ref, xh_ref, g_ref, wg_ref, wu_ref, cwg_ref, cwu_ref, cbg_ref, cbu_ref, wd_ref,
                o_ref, h_ref, ug_ref, uu_ref, *, rows, tiles_per_seq):
    i = pl.program_id(0)
    j = pl.program_id(1)

    @pl.when(j == 0)
    def _():
        _fill_normed(h_ref, x_ref, xh_ref, g_ref, i % tiles_per_seq == 0, copy_ref=o_ref)

    def down_proj(c):
        gate = _causal_conv(ug_ref, c, cwg_ref, cbg_ref, FFN_CONV, rows)
        up = _causal_conv(uu_ref, c, cwu_ref, cbu_ref, FFN_CONV, rows)
        act = (gate * up / (1.0 + jnp.exp(-gate))).astype(BF16)
        o_ref[c * rows:(c + 1) * rows, :] += _dot(act, wd_ref[...])

    _row_chunks_conv(h_ref, [wg_ref, wu_ref], [ug_ref, uu_ref], rows, down_proj)


def _ffn_call(x, g, w_up, conv_w, conv_b, w_down, layer, seq, *, tm=1024, tf=512, rows=512):
    m = x.shape[0]
    nf = FFN_DIM // tf
    halo_blocks_per_tile = tm // HALO
    return pl.pallas_call(
        functools.partial(_ffn_kernel, rows=rows, tiles_per_seq=seq // tm),
        grid=(m // tm, nf),
        in_specs=[
            pl.BlockSpec((tm, D_MODEL), lambda i, j: (i, 0)),
            pl.BlockSpec((HALO, D_MODEL), lambda i, j: (jnp.maximum(i * halo_blocks_per_tile - 1, 0), 0)),
            pl.BlockSpec((None, 1, D_MODEL), lambda i, j: (layer, 0, 0)),
            pl.BlockSpec((D_MODEL, tf), lambda i, j: (0, j)),
            pl.BlockSpec((D_MODEL, tf), lambda i, j: (0, nf + j)),
            pl.BlockSpec((None, FFN_CONV, tf), lambda i, j: (layer, 0, j)),
            pl.BlockSpec((None, FFN_CONV, tf), lambda i, j: (layer, 0, nf + j)),
            pl.BlockSpec((None, 1, tf), lambda i, j: (layer, 0, j)),
            pl.BlockSpec((None, 1, tf), lambda i, j: (layer, 0, nf + j)),
            pl.BlockSpec((tf, D_MODEL), lambda i, j: (j, 0)),
        ],
        out_specs=pl.BlockSpec((tm, D_MODEL), lambda i, j: (i, 0)),
        out_shape=jax.ShapeDtypeStruct((m, D_MODEL), F32),
        scratch_shapes=[
            pltpu.VMEM((tm + HALO, D_MODEL), BF16),
            pltpu.VMEM((tm + HALO, tf), F32),
            pltpu.VMEM((tm + HALO, tf), F32),
        ],
        compiler_params=_params(("parallel", "arbitrary")),
        name="conv_ffn",
    )(x, x, g, w_up, w_up, conv_w, conv_w, conv_b, conv_b, w_down)


def _lstm_in_kernel(x_ref, xh_ref, g_ref, w_ref, wgate_ref, bgate_ref, cw_ref, cb_ref,
                    z_ref, gates_ref, h_ref, u_ref, *, tn, tiles_per_seq, n_conv_blocks):
    i = pl.program_id(0)
    tm = x_ref.shape[0]
    _fill_normed(h_ref, x_ref, xh_ref, g_ref, i % tiles_per_seq == 0)
    wg = wgate_ref[...]
    wg = jnp.concatenate([wg, jnp.zeros((LANES - wg.shape[0], wg.shape[1]), F32)], axis=0).astype(BF16)
    gates_ref[...] = _dot_nt(h_ref[HALO:, :], wg) + bgate_ref[...]

    for cb in range(w_ref.shape[0] // tn):
        cols = pl.ds(cb * tn, tn)
        w = w_ref[cb * tn:(cb + 1) * tn, :]
        if cb >= n_conv_blocks:
            z_ref[:, cols] = _dot_nt(h_ref[HALO:, :], w)
            continue
        u = u_ref.at[cb % u_ref.shape[0]]
        u[...] = _dot_nt(h_ref[...], w)
        y = _causal_conv(u, 0, cw_ref.at[:, cols], cb_ref.at[:, cols], LSTM_CONV, tm)
        scale = 1.0 if cb < n_conv_blocks // 2 else LSTM_QK_DIM ** -0.5
        z_ref[:, cols] = y * scale / (1.0 + jnp.exp(-y))


def _lstm_in_call(x, g, w_main_t, w_in_t, b_gate, conv_w, conv_b, seq, *, tm=256, tn=1024):
    m = x.shape[0]
    n = LSTM_MAIN_WIDTH
    n_gates = 2 * LSTM_HEADS
    assert w_in_t.shape[0] == n + n_gates and n % n_gates == 0 and n_gates == SUBLANES
    n_conv_blocks = 2 * LSTM_QK_WIDTH // tn
    halo_blocks_per_tile = tm // HALO
    return pl.pallas_call(
        functools.partial(_lstm_in_kernel, tn=tn, tiles_per_seq=seq // tm, n_conv_blocks=n_conv_blocks),
        grid=(m // tm,),
        in_specs=[
            pl.BlockSpec((tm, D_MODEL), lambda i: (i, 0)),
            pl.BlockSpec((HALO, D_MODEL), lambda i: (jnp.maximum(i * halo_blocks_per_tile - 1, 0), 0)),
            pl.BlockSpec((1, D_MODEL), lambda i: (0, 0)),
            pl.BlockSpec((n, D_MODEL), lambda i: (0, 0), pipeline_mode=pl.Buffered(1)),
            pl.BlockSpec((n_gates, D_MODEL), lambda i: (n // n_gates, 0)),
            pl.BlockSpec((1, LANES), lambda i: (0, 0)),
            pl.BlockSpec((LSTM_CONV, 2 * LSTM_QK_WIDTH), lambda i: (0, 0)),
            pl.BlockSpec((1, 2 * LSTM_QK_WIDTH), lambda i: (0, 0)),
        ],
        out_specs=[
            pl.BlockSpec((tm, n), lambda i: (i, 0)),
            pl.BlockSpec((tm, LANES), lambda i: (i, 0)),
        ],
        out_shape=[
            jax.ShapeDtypeStruct((m, n), F32),
            jax.ShapeDtypeStruct((m, LANES), F32),
        ],
        scratch_shapes=[
            pltpu.VMEM((tm + HALO, D_MODEL), BF16),
            pltpu.VMEM((n_conv_blocks, tm + HALO, tn), F32),
        ],
        compiler_params=_params(("arbitrary",)),
        name="lstm_in",
    )(x, x, g, w_main_t, w_in_t, b_gate, conv_w, conv_b)


def _log_sigmoid(x):
    return -(jnp.maximum(-x, 0.0) + jnp.log1p(jnp.exp(-jnp.abs(x))))


def _cumsum_rows(x):
    n = x.shape[0]
    row = lax.broadcasted_iota(jnp.int32, x.shape, 0)
    shift = 1
    while shift < n:
        x = x + jnp.where(row >= shift, pltpu.roll(x, shift, axis=0), 0.0)
        shift *= 2
    return x


def _mlstm_kernel(q_ref, k_ref, v_ref, og_ref, gates_ref, hg_ref, wo_ref, x_ref, o_ref, c_ref, n_ref, m_ref, hs_ref,
                  *, chunks_per_seq):
    chunk = q_ref.shape[0]
    dk, dv, heads = LSTM_QK_DIM, LSTM_V_DIM, LSTM_HEADS
    step = pl.program_id(0)

    @pl.when(step % chunks_per_seq == 0)
    def _():
        c_ref[...] = jnp.zeros_like(c_ref)
        n_ref[...] = jnp.zeros_like(n_ref)
        m_ref[...] = jnp.zeros_like(m_ref)

    @pl.when(step == 0)
    def _():
        hs_ref[...] = jnp.zeros_like(hs_ref)

    o_ref[...] = x_ref[...] + _dot(hs_ref[...], wo_ref[...])

    gates = gates_ref[...]
    cum_f = _cumsum_rows(_log_sigmoid(gates))
    gates_t = gates.T
    cum_f_t = cum_f.T
    row = lax.broadcasted_iota(jnp.int32, (chunk, chunk), 0)
    col = lax.broadcasted_iota(jnp.int32, (chunk, chunk), 1)
    causal = col <= row

    for hd in range(heads):
        qs = slice(hd * dk, (hd + 1) * dk)
        vs = slice(hd * dv, (hd + 1) * dv)
        b_col = cum_f[:, heads + hd:heads + hd + 1]
        b_row = cum_f_t[heads + hd:heads + hd + 1, :]
        li_col = gates[:, hd:hd + 1]
        li_row = gates_t[hd:hd + 1, :]
        m_prev = m_ref[hd]
        c_prev = c_ref[hd]
        n_prev = n_ref[hd]
        q = q_ref[:, qs]
        k = k_ref[:, qs]
        v = v_ref[:, vs]
        qb = q.astype(BF16)
        kb = k.astype(BF16)

        dmat = jnp.where(causal, b_col - b_row + li_row, -jnp.inf)
        g = b_col + m_prev
        m_t = jnp.maximum(g, jnp.max(dmat, axis=-1, keepdims=True))
        p = jnp.exp(dmat - m_t)
        inter = jnp.exp(g - m_t)
        w = p * _dot_nt(qb, kb)
        num = inter * _dot(qb, c_prev.astype(BF16)) + _dot(w.astype(BF16), v.astype(BF16))
        den = inter * jnp.sum(q * n_prev, axis=-1, keepdims=True) + jnp.sum(w, axis=-1, keepdims=True)
        h = num / jnp.maximum(jnp.abs(den), jnp.exp(-m_t))

        b_last = b_col[chunk - 1:chunk, :]
        a = b_last - b_col + li_col
        m_new = jnp.maximum(b_last + m_prev, jnp.max(a, axis=0, keepdims=True))
        decay = jnp.exp(b_last + m_prev - m_new)
        wts = jnp.exp(a - m_new)
        c_ref[hd] = decay * c_prev + _dot_tn(kb, (wts * v).astype(BF16))
        n_ref[hd] = decay * n_prev + jnp.sum(wts * k, axis=0, keepdims=True)
        m_ref[hd] = m_new

        hs = _rms(h, hg_ref[:, vs]) * jax.nn.sigmoid(og_ref[:, vs])
        hs_ref[:, vs] = hs.astype(hs_ref.dtype)


def _mlstm_call(z, gates, head_gain, w_out, x, batch, seq):
    chunk = LSTM_CHUNK
    n_chunks = batch * seq // chunk
    qkw = LSTM_QK_WIDTH

    def cur(s):
        return jnp.minimum(s, n_chunks - 1)

    def lagged(s):
        return jnp.maximum(s - 1, 0)

    return pl.pallas_call(
        functools.partial(_mlstm_kernel, chunks_per_seq=seq // chunk),
        grid=(n_chunks + 1,),
        in_specs=[
            pl.BlockSpec((chunk, qkw), lambda s: (cur(s), 0)),
            pl.BlockSpec((chunk, qkw), lambda s: (cur(s), 1)),
            pl.BlockSpec((chunk, D_MODEL), lambda s: (cur(s), 2 * qkw // D_MODEL)),
            pl.BlockSpec((chunk, D_MODEL), lambda s: (cur(s), 2 * qkw // D_MODEL + 1)),
            pl.BlockSpec((chunk, LANES), lambda s: (cur(s), 0)),
            pl.BlockSpec((1, D_MODEL), lambda s: (0, 0)),
            pl.BlockSpec((D_MODEL, D_MODEL), lambda s: (0, 0), pipeline_mode=pl.Buffered(1)),
            pl.BlockSpec((chunk, D_MODEL), lambda s: (lagged(s), 0)),
        ],
        out_specs=pl.BlockSpec((chunk, D_MODEL), lambda s: (lagged(s), 0)),
        out_shape=jax.ShapeDtypeStruct((batch * seq, D_MODEL), F32),
        scratch_shapes=[
            pltpu.VMEM((LSTM_HEADS, LSTM_QK_DIM, LSTM_V_DIM), F32),
            pltpu.VMEM((LSTM_HEADS, 1, LSTM_QK_DIM), F32),
            pltpu.VMEM((LSTM_HEADS, 1, 1), F32),
            pltpu.VMEM((chunk, D_MODEL), BF16),
        ],
        compiler_params=_params(("arbitrary",)),
        name="mlstm_proj",
    )(z, z, z, z, gates, head_gain, w_out, x)


def kernel(x, attn_norm, attn_w_qkv, attn_q_gain, attn_k_gain, attn_w_o, lstm_norm, lstm_w_in, lstm_gate_bias,
           lstm_conv_w, lstm_conv_b, lstm_head_gain, lstm_w_out, ffn_norm, ffn_w_up, ffn_conv_w, ffn_conv_b,
           ffn_w_down):
    batch, seq, d = x.shape
    assert d == D_MODEL and ffn_norm.shape[0] == 2 and attn_norm.shape[0] == 1 and lstm_norm.shape[0] == 1
    m = batch * seq
    xf = x.reshape(m, d)

    def ffn(xin, layer, w_up, w_down):
        return _ffn_call(xin, ffn_norm[:, None, :], w_up, ffn_conv_w, ffn_conv_b[:, None, :], w_down, layer, seq)

    w_in_t = jnp.swapaxes(lstm_w_in[0], 0, 1)
    qkv, (w_up0, w_down0, w_o, w_in_main_t) = _qkv_call(
        xf, attn_norm[0][None], attn_w_qkv[0].astype(BF16), attn_q_gain[0][None], attn_k_gain[0][None],
        [_Cast(ffn_w_up, 0, 64), _Cast(ffn_w_down, 0, 128), _Cast(attn_w_o, 0, 64),
         _Cast(w_in_t, None, 128, LSTM_MAIN_WIDTH)])
    att, (w_up1, w_down1, w_out) = _attn_call(
        qkv, batch, seq, [_Cast(ffn_w_up, 1, 64), _Cast(ffn_w_down, 1, 176), _Cast(lstm_w_out, 0, 64)])
    xf = _proj_call(att, w_o, xf)
    xf = ffn(xf, 0, w_up0, w_down0)

    n_gates = 2 * LSTM_HEADS
    b_gate = jnp.pad(lstm_gate_bias[0], (0, LANES - n_gates))[None]
    z, gates = _lstm_in_call(xf, lstm_norm[0][None], w_in_main_t, w_in_t, b_gate,
                             lstm_conv_w[0], lstm_conv_b[0][None], seq)
    xf = _mlstm_call(z, gates, lstm_head_gain[0][None], w_out, xf, batch, seq)
    xf = ffn(xf, 1, w_up1, w_down1)
    return xf.reshape(batch, seq, d)
```

```python
import functools
from typing import NamedTuple, Optional

import jax
import jax.numpy as jnp
from jax import lax
from jax.experimental import pallas as pl
from jax.experimental.pallas import tpu as pltpu

F32 = jnp.float32
BF16 = jnp.bfloat16

D_MODEL = 2048
ATT_HEADS = 16
ATT_HEAD_DIM = D_MODEL // ATT_HEADS
DILATED_PATTERNS = ((128, 1), (512, 4), (2048, 16))
ATT_BLOCK = 128
ATT_GROUP = 4
LSTM_HEADS = 4
LSTM_V_DIM = D_MODEL // LSTM_HEADS
LSTM_QK_DIM = LSTM_V_DIM // 2
LSTM_QK_WIDTH = LSTM_HEADS * LSTM_QK_DIM
LSTM_MAIN_WIDTH = 2 * LSTM_QK_WIDTH + 2 * D_MODEL
LSTM_CONV = 4
FFN_DIM = ((8 * D_MODEL // 3 + 255) // 256) * 256
FFN_CONV = 3
NORM_EPS = 1e-6

LANES = 128
SUBLANES = 8
BF16_ROWS_PER_VREG = 2 * SUBLANES
HALO = BF16_ROWS_PER_VREG
VMEM_LIMIT = 56 * 1024 * 1024

LSTM_CHUNK = 256

NORM_ROWS = 128


def _rms(x, g):
    ms = jnp.mean(x * x, axis=-1, keepdims=True)
    return x * lax.rsqrt(ms + NORM_EPS) * g


def _norm_rows(h_ref, lead, x_ref, g, copy_ref=None):
    def body(r, carry):
        start = pl.multiple_of(r * NORM_ROWS, NORM_ROWS)
        x = x_ref[pl.ds(start, NORM_ROWS), :]
        h_ref[pl.ds(pl.multiple_of(start + lead, BF16_ROWS_PER_VREG), NORM_ROWS), :] = _rms(x, g).astype(BF16)
        if copy_ref is not None:
            copy_ref[pl.ds(start, NORM_ROWS), :] = x
        return carry

    lax.fori_loop(0, x_ref.shape[0] // NORM_ROWS, body, 0)


def _dot(a, b):
    return jnp.dot(a, b, preferred_element_type=F32)


def _dot_nt(a, b):
    return lax.dot_general(a, b, (((1,), (1,)), ((), ())), preferred_element_type=F32)


def _dot_tn(a, b):
    return lax.dot_general(a, b, (((0,), (0,)), ((), ())), preferred_element_type=F32)


def _params(sem):
    return pltpu.CompilerParams(dimension_semantics=sem, vmem_limit_bytes=VMEM_LIMIT)


class _Cast(NamedTuple):
    src: jax.Array
    layer: Optional[int]
    rows: int
    total_rows: Optional[int] = None


def _cast_plan(casts, step_of):
    in_specs, out_specs, out_shapes, n_blocks = [], [], [], []
    for c in casts:
        r, cols = c.src.shape[-2:]
        r = c.total_rows or r
        assert r % c.rows == 0 and c.rows % BF16_ROWS_PER_VREG == 0
        nb = r // c.rows

        def block(*ids, nb=nb):
            return jnp.minimum(step_of(*ids), nb - 1)

        if c.layer is None:
            in_specs.append(pl.BlockSpec((c.rows, cols), lambda *ids, block=block: (block(*ids), 0)))
        else:
            in_specs.append(pl.BlockSpec((None, c.rows, cols),
                                         lambda *ids, block=block, layer=c.layer: (layer, block(*ids), 0)))
        out_specs.append(pl.BlockSpec((c.rows, cols), lambda *ids, block=block: (block(*ids), 0)))
        out_shapes.append(jax.ShapeDtypeStruct((r, cols), BF16))
        n_blocks.append(nb)
    return in_specs, out_specs, out_shapes, tuple(n_blocks)


def _cast_step(step, src_refs, dst_refs, n_blocks):
    for src, dst, nb in zip(src_refs, dst_refs, n_blocks):
        @pl.when(step < nb)
        def _(src=src, dst=dst):
            for r0 in range(0, src.shape[0], BF16_ROWS_PER_VREG):
                sl = slice(r0, r0 + BF16_ROWS_PER_VREG)
                dst[sl, :] = src[sl, :].astype(BF16)


def _cast_part(src_refs, dst_refs, part, n_parts):
    for src, dst in zip(src_refs, dst_refs):
        rows = src.shape[0] // n_parts
        assert rows % BF16_ROWS_PER_VREG == 0
        for r0 in range(part * rows, (part + 1) * rows, BF16_ROWS_PER_VREG):
            sl = slice(r0, r0 + BF16_ROWS_PER_VREG)
            dst[sl, :] = src[sl, :].astype(BF16)


def _row_chunks(h_ref, w_refs, rows, lead, with_halo, consume, dot=_dot):
    n_chunks = (h_ref.shape[0] - lead) // rows
    assert not with_halo or lead == HALO

    def project(c, tails):
        if with_halo and c == 0:
            h = h_ref[0:rows + lead, :]
            return [dot(h, w[...]) for w in w_refs]
        h = h_ref[c * rows + lead:(c + 1) * rows + lead, :]
        us = [dot(h, w[...]) for w in w_refs]
        if with_halo:
            us = [jnp.concatenate([t, u], axis=0) for t, u in zip(tails, us)]
        return us

    us = project(0, None)
    for c in range(n_chunks):
        nxt = project(c + 1, [u[rows:, :] for u in us]) if c + 1 < n_chunks else None
        consume(c, us)
        us = nxt


def _row_chunks_conv(h_ref, w_refs, u_refs, rows, consume, dot=_dot):
    n_chunks = (h_ref.shape[0] - HALO) // rows

    def project(c):
        lo = 0 if c == 0 else c * rows + HALO
        hi = (c + 1) * rows + HALO
        h = h_ref[lo:hi, :]
        for w, u in zip(w_refs, u_refs):
            u[lo:hi, :] = dot(h, w[...])

    project(0)
    for c in range(n_chunks):
        if c + 1 < n_chunks:
            project(c + 1)
        consume(c)


def _qkv_kernel(*refs, rows, n_head_blocks, heads_per_block, cast_blocks):
    nc = len(cast_blocks)
    x_ref, g_ref, w_ref, qg_ref, kg_ref = refs[:5]
    cast_src, o_ref, cast_dst, h_ref = refs[5:5 + nc], refs[5 + nc], refs[6 + nc:6 + 2 * nc], refs[6 + 2 * nc]
    j = pl.program_id(1)
    n_chunks = h_ref.shape[0] // rows

    @pl.when(j == 0)
    def _():
        _norm_rows(h_ref, 0, x_ref, g_ref[...])

    @pl.when(j < 2 * n_head_blocks)
    def _():
        gain = jnp.where(j < n_head_blocks, qg_ref[...] * (ATT_HEAD_DIM ** -0.5), kg_ref[...])

        def qk_norm(c, us):
            _cast_part(cast_src, cast_dst, c, n_chunks)
            for hh in range(heads_per_block):
                sl = slice(hh * ATT_HEAD_DIM, (hh + 1) * ATT_HEAD_DIM)
                o_ref[c * rows:(c + 1) * rows, sl] = _rms(us[0][:, sl], gain)

        _row_chunks(h_ref, [w_ref], rows, 0, False, qk_norm)

    @pl.when(j >= 2 * n_head_blocks)
    def _():
        def copy(c, us):
            _cast_part(cast_src, cast_dst, c, n_chunks)
            o_ref[c * rows:(c + 1) * rows, :] = us[0]

        _row_chunks(h_ref, [w_ref], rows, 0, False, copy)


def _qkv_call(x, g, w, qg, kg, casts, *, tm=1024, tn=1024, rows=256):
    m = x.shape[0]
    n = w.shape[1]
    heads_per_block = tn // ATT_HEAD_DIM
    n_head_blocks = D_MODEL // tn
    n_col_blocks = n // tn
    cast_in, cast_out, cast_shapes, cast_blocks = _cast_plan(casts, lambda i, j: i * n_col_blocks + j)
    qkv, *cast = pl.pallas_call(
        functools.partial(_qkv_kernel, rows=rows, n_head_blocks=n_head_blocks, heads_per_block=heads_per_block,
                          cast_blocks=cast_blocks),
        grid=(m // tm, n_col_blocks),
        in_specs=[
            pl.BlockSpec((tm, D_MODEL), lambda i, j: (i, 0)),
            pl.BlockSpec((1, D_MODEL), lambda i, j: (0, 0)),
            pl.BlockSpec((D_MODEL, tn), lambda i, j: (0, j)),
            pl.BlockSpec((1, ATT_HEAD_DIM), lambda i, j: (0, 0)),
            pl.BlockSpec((1, ATT_HEAD_DIM), lambda i, j: (0, 0)),
        ] + cast_in,
        out_specs=[pl.BlockSpec((tm, tn), lambda i, j: (i, j))] + cast_out,
        out_shape=[jax.ShapeDtypeStruct((m, n), F32)] + cast_shapes,
        scratch_shapes=[pltpu.VMEM((tm, D_MODEL), BF16)],
        compiler_params=_params(("arbitrary", "arbitrary")),
        name="attn_qkv",
    )(x, g, w, qg, kg, *[c.src for c in casts])
    return qkv, cast


ROW_INTERLEAVE = 4


def _attn_kernel(*refs, seq, cast_blocks):
    nc = len(cast_blocks)
    q_ref, k_ref, v_ref = refs[:3]
    cast_src, o_ref, cast_dst = refs[3:3 + nc], refs[3 + nc], refs[4 + nc:4 + 2 * nc]
    z_ref, acc_ref, den_ref, mx_ref, bias_ref, s_ref = refs[4 + 2 * nc:]
    _cast_step(pl.program_id(0) * pl.num_programs(1) + pl.program_id(1), cast_src, cast_dst, cast_blocks)
    _attn_body(q_ref, k_ref, v_ref, o_ref, z_ref, acc_ref, den_ref, mx_ref, bias_ref, s_ref, seq=seq)


def _attn_body(q_ref, k_ref, v_ref, o_ref, z_ref, acc_ref, den_ref, mx_ref, bias_ref, s_ref, *, seq):
    blk, hd, grp, il = ATT_BLOCK, ATT_HEAD_DIM, ATT_GROUP, ROW_INTERLEAVE
    part = seq // il
    n_chunks = seq // blk
    assert part % blk == 0

    def chunk(c):
        return pl.ds(pl.multiple_of(c * blk, blk), blk)

    def natural_rows(c):
        lo = c // (part // blk)
        b0 = (c % (part // blk)) * blk
        return pl.ds(lo + il * b0, blk, stride=il)

    for ti, src_ref in enumerate((q_ref, k_ref, v_ref)):
        for c in range(n_chunks):
            z_ref[ti, c * blk:(c + 1) * blk, :] = src_ref[natural_rows(c), :]

    qz_ref, kz_ref, vz_ref = z_ref.at[0], z_ref.at[1], z_ref.at[2]
    row = lax.broadcasted_iota(jnp.int32, (blk, 2 * blk), 0)
    col = lax.broadcasted_iota(jnp.int32, (blk, 2 * blk), 1)
    ones = jnp.ones((grp, 2 * blk, hd), BF16)

    for gi, (window, dil) in enumerate(DILATED_PATTERNS):
        assert window // dil == blk and (dil % il == 0 or il % dil == 0)
        pieces = max(il // dil, 1)
        plen = blk // pieces
        stride = max(dil // il, 1)
        n_blocks = seq // (dil * blk)
        assert n_blocks & (n_blocks - 1) == 0 and (dil * n_blocks) % grp == 0 and plen % SUBLANES == 0
        log_nb = n_blocks.bit_length() - 1
        log_plen = plen.bit_length() - 1

        def step_in_block(i, pieces=pieces, plen=plen, log_plen=log_plen):
            return pieces * jnp.bitwise_and(i, plen - 1) + lax.shift_right_logical(i, log_plen)

        dist = step_in_block(row) - step_in_block(jnp.bitwise_and(col, blk - 1)) + jnp.where(col < blk, blk, 0)
        band = jnp.logical_and(dist >= 0, dist <= blk)
        bias_ref[2 * gi + 1] = jnp.where(band, 0.0, -jnp.inf)
        bias_ref[2 * gi] = jnp.where(jnp.logical_and(band, col >= blk), 0.0, -jnp.inf)

        def block_rows(a, n, dil=dil, pieces=pieces, plen=plen, stride=stride):
            if pieces > 1:
                return [pl.ds(pl.multiple_of((p * dil + a) * part + n * plen, SUBLANES), plen) for p in range(pieces)]
            lo = jnp.bitwise_and(a, il - 1)
            hi = lax.shift_right_logical(a, il.bit_length() - 1)
            start = lo * part + hi + stride * blk * n
            if stride == 1:
                return [pl.ds(pl.multiple_of(start, blk), blk)]
            return [pl.ds(start, blk, stride=stride)]

        def load(ref, parts):
            return jnp.concatenate([ref[d, :] for d in parts], axis=0)

        def store(ref, gi, parts, val, plen=plen):
            for p, d in enumerate(parts):
                ref[gi, d, :] = val[p * plen:(p + 1) * plen, :]

        def blocks_of(it, n_blocks=n_blocks, log_nb=log_nb, block_rows=block_rows):
            out = []
            for gg in range(grp):
                idx = jnp.asarray(it * grp + gg, jnp.int32)
                a = lax.shift_right_logical(idx, log_nb)
                n = jnp.bitwise_and(idx, n_blocks - 1)
                out.append((block_rows(a, n), block_rows(a, jnp.maximum(n - 1, 0)), jnp.minimum(n, 1)))
            return out

        def scores(it, gi=gi, blocks_of=blocks_of, load=load):
            qs, ks, bias = [], [], []
            for c, p, has_prev in blocks_of(it):
                qs.append(load(qz_ref, c))
                ks.append(jnp.concatenate([load(kz_ref, p), load(kz_ref, c)], axis=0))
                bias.append(bias_ref[2 * gi + has_prev])
            q = jnp.stack(qs).astype(BF16)
            k = jnp.stack(ks).astype(BF16)
            return jnp.einsum("gqd,gkd->gqk", q, k, preferred_element_type=F32) + jnp.stack(bias)

        def softmax_pv(it, s, gi=gi, blocks_of=blocks_of, load=load, store=store):
            blocks = blocks_of(it)
            vs = [jnp.concatenate([load(vz_ref, p), load(vz_ref, c)], axis=0) for c, p, _ in blocks]
            v = jnp.concatenate([jnp.stack(vs).astype(BF16), ones], axis=-1)
            mx = jnp.max(s, axis=-1, keepdims=True)
            p = jnp.exp(s - mx).astype(BF16)
            pv = jnp.einsum("gqk,gkd->gqd", p, v, preferred_element_type=F32)
            for gg, (c, _, _) in enumerate(blocks):
                store(acc_ref, gi, c, pv[gg, :, :hd])
                store(den_ref, gi, c, pv[gg, :, hd:])
                store(mx_ref, gi, c, jnp.broadcast_to(mx[gg], (blk, hd)))

        n_steps = dil * n_blocks // grp
        s_ref[0] = scores(0)
        for it in range(1, n_steps):
            s_ref[it & 1] = scores(it)
            softmax_pv(it - 1, s_ref[(it - 1) & 1])
        softmax_pv(n_steps - 1, s_ref[(n_steps - 1) & 1])

    nat_ref = z_ref.at[0]

    def combine(c, carry):
        sl = chunk(c)
        m0 = mx_ref[0, sl, :]
        m1 = mx_ref[1, sl, :]
        m2 = mx_ref[2, sl, :]
        mx = jnp.maximum(jnp.maximum(m0, m1), m2)
        w0 = jnp.exp(m0 - mx)
        w1 = jnp.exp(m1 - mx)
        w2 = jnp.exp(m2 - mx)
        num = w0 * acc_ref[0, sl, :] + w1 * acc_ref[1, sl, :] + w2 * acc_ref[2, sl, :]
        den = w0 * den_ref[0, sl, :] + w1 * den_ref[1, sl, :] + w2 * den_ref[2, sl, :]
        nat_ref[natural_rows(c), :] = num / den
        return carry

    lax.fori_loop(0, n_chunks, combine, 0, unroll=4)

    def emit(c, carry):
        o_ref[chunk(c), :] = nat_ref[chunk(c), :].astype(o_ref.dtype)
        return carry

    lax.fori_loop(0, n_chunks, emit, 0, unroll=2)


def _attn_call(qkv, batch, seq, casts):
    qkv = qkv.reshape(batch, seq, 3 * D_MODEL)
    hd = ATT_HEAD_DIM
    n_groups = len(DILATED_PATTERNS)
    cast_in, cast_out, cast_shapes, cast_blocks = _cast_plan(casts, lambda b, h: b * ATT_HEADS + h)
    out, *cast = pl.pallas_call(
        functools.partial(_attn_kernel, seq=seq, cast_blocks=cast_blocks),
        grid=(batch, ATT_HEADS),
        in_specs=[
            pl.BlockSpec((None, seq, hd), lambda b, h: (b, 0, h)),
            pl.BlockSpec((None, seq, hd), lambda b, h: (b, 0, ATT_HEADS + h)),
            pl.BlockSpec((None, seq, hd), lambda b, h: (b, 0, 2 * ATT_HEADS + h)),
        ] + cast_in,
        out_specs=[pl.BlockSpec((None, seq, hd), lambda b, h: (b, 0, h))] + cast_out,
        out_shape=[jax.ShapeDtypeStruct((batch, seq, D_MODEL), BF16)] + cast_shapes,
        scratch_shapes=[
            pltpu.VMEM((3, seq, hd), F32),
            pltpu.VMEM((n_groups, seq, hd), F32),
            pltpu.VMEM((n_groups, seq, hd), F32),
            pltpu.VMEM((n_groups, seq, hd), F32),
            pltpu.VMEM((2 * n_groups, ATT_BLOCK, 2 * ATT_BLOCK), F32),
            pltpu.VMEM((2, ATT_GROUP, ATT_BLOCK, 2 * ATT_BLOCK), F32),
        ],
        compiler_params=_params(("arbitrary", "arbitrary")),
        name="dilated_attn",
    )(qkv, qkv, qkv, *[c.src for c in casts])
    return out.reshape(batch * seq, D_MODEL), cast


def _proj_kernel(a_ref, w_ref, x_ref, o_ref, *, rows):
    for c in range(o_ref.shape[0] // rows):
        sl = slice(c * rows, (c + 1) * rows)
        o_ref[sl, :] = x_ref[sl, :] + _dot(a_ref[sl, :], w_ref[...])


def _proj_call(a, w, x, *, tm=512, rows=256):
    m, k = a.shape
    n = w.shape[1]
    return pl.pallas_call(
        functools.partial(_proj_kernel, rows=rows),
        grid=(m // tm,),
        in_specs=[
            pl.BlockSpec((tm, k), lambda i: (i, 0)),
            pl.BlockSpec((k, n), lambda i: (0, 0)),
            pl.BlockSpec((tm, n), lambda i: (i, 0)),
        ],
        out_specs=pl.BlockSpec((tm, n), lambda i: (i, 0)),
        out_shape=jax.ShapeDtypeStruct((m, n), F32),
        compiler_params=_params(("parallel",)),
        name="proj_residual",
    )(a, w, x)


def _fill_normed(h_ref, x_ref, xh_ref, g_ref, seq_start, copy_ref=None):
    g = g_ref[...]
    halo = jnp.where(seq_start, 0.0, _rms(xh_ref[...], g))
    h_ref[0:HALO, :] = halo.astype(BF16)
    _norm_rows(h_ref, HALO, x_ref, g, copy_ref)


def _causal_conv(u_ref, c, w_ref, b_ref, taps, rows):
    w = w_ref[...]
    y = b_ref[...]
    for j in range(taps):
        off = c * rows + HALO - (taps - 1) + j
        y = y + w[j:j + 1, :] * u_ref[off:off + rows, :]
    return y


def _ffn_kernel(x_ref, xh_ref, g_ref, wg_ref, wu_ref, cwg_ref, cwu_ref, cbg_ref, cbu_ref, wd_ref,
                o_ref, h_ref, ug_ref, uu_ref, *, rows, tiles_per_seq):
    i = pl.program_id(0)
    j = pl.program_id(1)

    @pl.when(j == 0)
    def _():
        _fill_normed(h_ref, x_ref, xh_ref, g_ref, i % tiles_per_seq == 0, copy_ref=o_ref)

    def down_proj(c):
        gate = _causal_conv(ug_ref, c, cwg_ref, cbg_ref, FFN_CONV, rows)
        up = _causal_conv(uu_ref, c, cwu_ref, cbu_ref, FFN_CONV, rows)
        act = (gate * up / (1.0 + jnp.exp(-gate))).astype(BF16)
        o_ref[c * rows:(c + 1) * rows, :] += _dot(act, wd_ref[...])

    _row_chunks_conv(h_ref, [wg_ref, wu_ref], [ug_ref, uu_ref], rows, down_proj)


def _ffn_call(x, g, w_up, conv_w, conv_b, w_down, layer, seq, *, tm=1024, tf=512, rows=512):
    m = x.shape[0]
    nf = FFN_DIM // tf
    halo_blocks_per_tile = tm // HALO
    return pl.pallas_call(
        functools.partial(_ffn_kernel, rows=rows, tiles_per_seq=seq // tm),
        grid=(m // tm, nf),
        in_specs=[
            pl.BlockSpec((tm, D_MODEL), lambda i, j: (i, 0)),
            pl.BlockSpec((HALO, D_MODEL), lambda i, j: (jnp.maximum(i * halo_blocks_per_tile - 1, 0), 0)),
            pl.BlockSpec((None, 1, D_MODEL), lambda i, j: (layer, 0, 0)),
            pl.BlockSpec((D_MODEL, tf), lambda i, j: (0, j)),
            pl.BlockSpec((D_MODEL, tf), lambda i, j: (0, nf + j)),
            pl.BlockSpec((None, FFN_CONV, tf), lambda i, j: (layer, 0, j)),
            pl.BlockSpec((None, FFN_CONV, tf), lambda i, j: (layer, 0, nf + j)),
            pl.BlockSpec((None, 1, tf), lambda i, j: (layer, 0, j)),
            pl.BlockSpec((None, 1, tf), lambda i, j: (layer, 0, nf + j)),
            pl.BlockSpec((tf, D_MODEL), lambda i, j: (j, 0)),
        ],
        out_specs=pl.BlockSpec((tm, D_MODEL), lambda i, j: (i, 0)),
        out_shape=jax.ShapeDtypeStruct((m, D_MODEL), F32),
        scratch_shapes=[
            pltpu.VMEM((tm + HALO, D_MODEL), BF16),
            pltpu.VMEM((tm + HALO, tf), F32),
            pltpu.VMEM((tm + HALO, tf), F32),
        ],
        compiler_params=_params(("parallel", "arbitrary")),
        name="conv_ffn",
    )(x, x, g, w_up, w_up, conv_w, conv_w, conv_b, conv_b, w_down)


def _lstm_in_kernel(x_ref, xh_ref, g_ref, w_ref, wgate_ref, bgate_ref, cw_ref, cb_ref,
                    z_ref, gates_ref, h_ref, u_ref, *, tn, tiles_per_seq, n_conv_blocks):
    i = pl.program_id(0)
    tm = x_ref.shape[0]
    _fill_normed(h_ref, x_ref, xh_ref, g_ref, i % tiles_per_seq == 0)
    wg = wgate_ref[...]
    wg = jnp.concatenate([wg, jnp.zeros((LANES - wg.shape[0], wg.shape[1]), F32)], axis=0).astype(BF16)
    gates_ref[...] = _dot_nt(h_ref[HALO:, :], wg) + bgate_ref[...]

    for cb in range(w_ref.shape[0] // tn):
        cols = pl.ds(cb * tn, tn)
        w = w_ref[cb * tn:(cb + 1) * tn, :]
        if cb >= n_conv_blocks:
            z_ref[:, cols] = _dot_nt(h_ref[HALO:, :], w)
            continue
        u = u_ref.at[cb % u_ref.shape[0]]
        u[...] = _dot_nt(h_ref[...], w)
        y = _causal_conv(u, 0, cw_ref.at[:, cols], cb_ref.at[:, cols], LSTM_CONV, tm)
        scale = 1.0 if cb < n_conv_blocks // 2 else LSTM_QK_DIM ** -0.5
        z_ref[:, cols] = y * scale / (1.0 + jnp.exp(-y))


def _lstm_in_call(x, g, w_main_t, w_in_t, b_gate, conv_w, conv_b, seq, *, tm=256, tn=1024):
    m = x.shape[0]
    n = LSTM_MAIN_WIDTH
    n_gates = 2 * LSTM_HEADS
    assert w_in_t.shape[0] == n + n_gates and n % n_gates == 0 and n_gates == SUBLANES
    n_conv_blocks = 2 * LSTM_QK_WIDTH // tn
    halo_blocks_per_tile = tm // HALO
    return pl.pallas_call(
        functools.partial(_lstm_in_kernel, tn=tn, tiles_per_seq=seq // tm, n_conv_blocks=n_conv_blocks),
        grid=(m // tm,),
        in_specs=[
            pl.BlockSpec((tm, D_MODEL), lambda i: (i, 0)),
            pl.BlockSpec((HALO, D_MODEL), lambda i: (jnp.maximum(i * halo_blocks_per_tile - 1, 0), 0)),
            pl.BlockSpec((1, D_MODEL), lambda i: (0, 0)),
            pl.BlockSpec((n, D_MODEL), lambda i: (0, 0), pipeline_mode=pl.Buffered(1)),
            pl.BlockSpec((n_gates, D_MODEL), lambda i: (n // n_gates, 0)),
            pl.BlockSpec((1, LANES), lambda i: (0, 0)),
            pl.BlockSpec((LSTM_CONV, 2 * LSTM_QK_WIDTH), lambda i: (0, 0)),
            pl.BlockSpec((1, 2 * LSTM_QK_WIDTH), lambda i: (0, 0)),
        ],
        out_specs=[
            pl.BlockSpec((tm, n), lambda i: (i, 0)),
            pl.BlockSpec((tm, LANES), lambda i: (i, 0)),
        ],
        out_shape=[
            jax.ShapeDtypeStruct((m, n), F32),
            jax.ShapeDtypeStruct((m, LANES), F32),
        ],
        scratch_shapes=[
            pltpu.VMEM((tm + HALO, D_MODEL), BF16),
            pltpu.VMEM((n_conv_blocks, tm + HALO, tn), F32),
        ],
        compiler_params=_params(("arbitrary",)),
        name="lstm_in",
    )(x, x, g, w_main_t, w_in_t, b_gate, conv_w, conv_b)


def _log_sigmoid(x):
    return -(jnp.maximum(-x, 0.0) + jnp.log1p(jnp.exp(-jnp.abs(x))))


def _cumsum_rows(x):
    n = x.shape[0]
    row = lax.broadcasted_iota(jnp.int32, x.shape, 0)
    shift = 1
    while shift < n:
        x = x + jnp.where(row >= shift, pltpu.roll(x, shift, axis=0), 0.0)
        shift *= 2
    return x


def _mlstm_kernel(q_ref, k_ref, v_ref, og_ref, gates_ref, hg_ref, wo_ref, x_ref, o_ref, c_ref, n_ref, m_ref, hs_ref,
                  *, chunks_per_seq):
    chunk = q_ref.shape[0]
    dk, dv, heads = LSTM_QK_DIM, LSTM_V_DIM, LSTM_HEADS
    step = pl.program_id(0)

    @pl.when(step % chunks_per_seq == 0)
    def _():
        c_ref[...] = jnp.zeros_like(c_ref)
        n_ref[...] = jnp.zeros_like(n_ref)
        m_ref[...] = jnp.zeros_like(m_ref)

    @pl.when(step == 0)
    def _():
        hs_ref[...] = jnp.zeros_like(hs_ref)

    o_ref[...] = x_ref[...] + _dot(hs_ref[...], wo_ref[...])

    gates = gates_ref[...]
    cum_f = _cumsum_rows(_log_sigmoid(gates))
    gates_t = gates.T
    cum_f_t = cum_f.T
    row = lax.broadcasted_iota(jnp.int32, (chunk, chunk), 0)
    col = lax.broadcasted_iota(jnp.int32, (chunk, chunk), 1)
    causal = col <= row

    for hd in range(heads):
        qs = slice(hd * dk, (hd + 1) * dk)
        vs = slice(hd * dv, (hd + 1) * dv)
        b_col = cum_f[:, heads + hd:heads + hd + 1]
        b_row = cum_f_t[heads + hd:heads + hd + 1, :]
        li_col = gates[:, hd:hd + 1]
        li_row = gates_t[hd:hd + 1, :]
        m_prev = m_ref[hd]
        c_prev = c_ref[hd]
        n_prev = n_ref[hd]
        q = q_ref[:, qs]
        k = k_ref[:, qs]
        v = v_ref[:, vs]
        qb = q.astype(BF16)
        kb = k.astype(BF16)

        dmat = jnp.where(causal, b_col - b_row + li_row, -jnp.inf)
        g = b_col + m_prev
        m_t = jnp.maximum(g, jnp.max(dmat, axis=-1, keepdims=True))
        p = jnp.exp(dmat - m_t)
        inter = jnp.exp(g - m_t)
        w = p * _dot_nt(qb, kb)
        num = inter * _dot(qb, c_prev.astype(BF16)) + _dot(w.astype(BF16), v.astype(BF16))
        den = inter * jnp.sum(q * n_prev, axis=-1, keepdims=True) + jnp.sum(w, axis=-1, keepdims=True)
        h = num / jnp.maximum(jnp.abs(den), jnp.exp(-m_t))

        b_last = b_col[chunk - 1:chunk, :]
        a = b_last - b_col + li_col
        m_new = jnp.maximum(b_last + m_prev, jnp.max(a, axis=0, keepdims=True))
        decay = jnp.exp(b_last + m_prev - m_new)
        wts = jnp.exp(a - m_new)
        c_ref[hd] = decay * c_prev + _dot_tn(kb, (wts * v).astype(BF16))
        n_ref[hd] = decay * n_prev + jnp.sum(wts * k, axis=0, keepdims=True)
        m_ref[hd] = m_new

        hs = _rms(h, hg_ref[:, vs]) * jax.nn.sigmoid(og_ref[:, vs])
        hs_ref[:, vs] = hs.astype(hs_ref.dtype)


def _mlstm_call(z, gates, head_gain, w_out, x, batch, seq):
    chunk = LSTM_CHUNK
    n_chunks = batch * seq // chunk
    qkw = LSTM_QK_WIDTH

    def cur(s):
        return jnp.minimum(s, n_chunks - 1)

    def lagged(s):
        return jnp.maximum(s - 1, 0)

    return pl.pallas_call(
        functools.partial(_mlstm_kernel, chunks_per_seq=seq // chunk),
        grid=(n_chunks + 1,),
        in_specs=[
            pl.BlockSpec((chunk, qkw), lambda s: (cur(s), 0)),
            pl.BlockSpec((chunk, qkw), lambda s: (cur(s), 1)),
            pl.BlockSpec((chunk, D_MODEL), lambda s: (cur(s), 2 * qkw // D_MODEL)),
            pl.BlockSpec((chunk, D_MODEL), lambda s: (cur(s), 2 * qkw // D_MODEL + 1)),
            pl.BlockSpec((chunk, LANES), lambda s: (cur(s), 0)),
            pl.BlockSpec((1, D_MODEL), lambda s: (0, 0)),
            pl.BlockSpec((D_MODEL, D_MODEL), lambda s: (0, 0), pipeline_mode=pl.Buffered(1)),
            pl.BlockSpec((chunk, D_MODEL), lambda s: (lagged(s), 0)),
        ],
        out_specs=pl.BlockSpec((chunk, D_MODEL), lambda s: (lagged(s), 0)),
        out_shape=jax.ShapeDtypeStruct((batch * seq, D_MODEL), F32),
        scratch_shapes=[
            pltpu.VMEM((LSTM_HEADS, LSTM_QK_DIM, LSTM_V_DIM), F32),
            pltpu.VMEM((LSTM_HEADS, 1, LSTM_QK_DIM), F32),
            pltpu.VMEM((LSTM_HEADS, 1, 1), F32),
            pltpu.VMEM((chunk, D_MODEL), BF16),
        ],
        compiler_params=_params(("arbitrary",)),
        name="mlstm_proj",
    )(z, z, z, z, gates, head_gain, w_out, x)


def kernel(x, attn_norm, attn_w_qkv, attn_q_gain, attn_k_gain, attn_w_o, lstm_norm, lstm_w_in, lstm_gate_bias,
           lstm_conv_w, lstm_conv_b, lstm_head_gain, lstm_w_out, ffn_norm, ffn_w_up, ffn_conv_w, ffn_conv_b,
           ffn_w_down):
    batch, seq, d = x.shape
    assert d == D_MODEL and ffn_norm.shape[0] == 2 and attn_norm.shape[0] == 1 and lstm_norm.shape[0] == 1
    m = batch * seq
    xf = x.reshape(m, d)

    def ffn(xin, layer, w_up, w_down):
        return _ffn_call(xin, ffn_norm[:, None, :], w_up, ffn_conv_w, ffn_conv_b[:, None, :], w_down, layer, seq)

    w_in_t = jnp.swapaxes(lstm_w_in[0], 0, 1)
    qkv, (w_up0, w_down0, w_o, w_in_main_t) = _qkv_call(
        xf, attn_norm[0][None], attn_w_qkv[0].astype(BF16), attn_q_gain[0][None], attn_k_gain[0][None],
        [_Cast(ffn_w_up, 0, 64), _Cast(ffn_w_down, 0, 128), _Cast(attn_w_o, 0, 64),
         _Cast(w_in_t, None, 128, LSTM_MAIN_WIDTH)])
    att, (w_up1, w_down1, w_out) = _attn_call(
        qkv, batch, seq, [_Cast(ffn_w_up, 1, 64), _Cast(ffn_w_down, 1, 176), _Cast(lstm_w_out, 0, 64)])
    xf = _proj_call(att, w_o, xf)
    xf = ffn(xf, 0, w_up0, w_down0)

    n_gates = 2 * LSTM_HEADS
    b_gate = jnp.pad(lstm_gate_bias[0], (0, LANES - n_gates))[None]
    z, gates = _lstm_in_call(xf, lstm_norm[0][None], w_in_main_t, w_in_t, b_gate,
                             lstm_conv_w[0], lstm_conv_b[0][None], seq)
    xf = _mlstm_call(z, gates, lstm_head_gain[0][None], w_out, xf, batch, seq)
    xf = ffn(xf, 1, w_up1, w_down1)
    return xf.reshape(batch, seq, d)
```
